```python
import math
import jax
import jax.numpy as jnp
from jax import lax
import numpy as np

D_MODEL = 2048
BATCH = 4
SEQ = 2048
DEPTH = 1
DEC_BATCH = 16
DEC_SEQ = 64
PAST_LEN = 2048

CHUNK = 64
D_SSM = D_MODEL // 2
SSM_GROUP_WIDTH = 16
SSM_GROUPS = D_SSM // SSM_GROUP_WIDTH
SSM_STATE = 64
D_FOX = D_MODEL // 2
FOX_HEAD_DIM = 64
FOX_HEADS = D_FOX // FOX_HEAD_DIM
Q_BLOCK = 128
N_EXPERTS = 64
TOP_K = 8
N_EXPERT_GROUPS = 8
TOPK_EXPERT_GROUPS = 4
EXPERT_DIM = D_MODEL // 4
SHARED_DIM = EXPERT_DIM
ROUTED_SCALE = 2.5
MOE_TOKEN_BLOCK = 512
LN_EPS = 1e-5
ALPHA = (2.0 * DEPTH) ** 0.25
BETA = (8.0 * DEPTH) ** -0.25
IN_COLS = D_SSM + 3 * D_FOX + FOX_HEADS + 2 * D_MODEL
IN_SPLITS = (D_SSM, D_SSM + D_FOX, D_SSM + 2 * D_FOX, D_SSM + 3 * D_FOX,
             D_SSM + 3 * D_FOX + FOX_HEADS, D_SSM + 3 * D_FOX + FOX_HEADS + D_MODEL)

kernel_name = 'hybrid_s5_fox_moe_stream_step'


def _layer_norm(x, g, b):
    xf = x.astype(jnp.float32)
    mu = jnp.mean(xf, axis=-1, keepdims=True)
    xc = xf - mu
    var = jnp.mean(xc * xc, axis=-1, keepdims=True)
    return xc * lax.rsqrt(var + LN_EPS) * g.astype(jnp.float32) + b.astype(jnp.float32)


def _diag_combine(e1, e2):
    a1, b1 = e1
    a2, b2 = e2
    return (a1 * a2, a2 * b1 + b2)


def _s5(u, lam_re, lam_im, log_dt, b_re, b_im, c_re, c_im, d, x0):
    f32 = jnp.float32
    lam = lax.complex(jnp.minimum(lam_re.astype(f32), -1e-4), lam_im.astype(f32))
    dt = jnp.exp(log_dt.astype(f32))[:, None]
    lam_bar = jnp.exp(lam * dt)
    b_bar = ((lam_bar - 1.0) / lam)[..., None] * lax.complex(b_re.astype(f32), b_im.astype(f32))
    c = lax.complex(c_re.astype(f32), c_im.astype(f32))
    uf = u.astype(f32)
    bu = jnp.einsum('blgw,gpw->blgp', uf.astype(jnp.complex64), b_bar)
    a = jnp.broadcast_to(lam_bar, (1,) + bu.shape[1:])
    a_cum, xs = lax.associative_scan(_diag_combine, (a, bu), axis=1)
    xs = xs + a_cum * x0[:, None]
    y = jnp.real(jnp.einsum('blgp,gwp->blgw', xs, c)) + d.astype(f32) * uf
    return y, xs[:, -1]


def _fox_attention(q, k, v, cum_q, cum_k, q_offset):
    bsz, t, h, dh = q.shape
    s = k.shape[1]
    blk = Q_BLOCK if t % Q_BLOCK == 0 else t
    nb = t // blk
    qb = q.reshape(bsz, nb, blk, h, dh).transpose(1, 0, 2, 3, 4)
    cqb = cum_q.reshape(bsz, nb, blk, h).transpose(1, 0, 3, 2)
    qpos = (q_offset + jnp.arange(t)).reshape(nb, blk)
    kpos = jnp.arange(s)
    ck = cum_k.transpose(0, 2, 1)
    scale = dh ** -0.5

    def one_block(args):
        qi, ci, pi = args
        logits = jnp.einsum('bqhd,bkhd->bhqk', qi, k).astype(jnp.float32) * scale
        logits = logits + ci[..., :, None] - ck[..., None, :]
        logits = jnp.where(kpos[None, :] <= pi[:, None], logits, -jnp.inf)
        probs = jax.nn.softmax(logits, axis=-1)
        return jnp.einsum('bhqk,bkhd->bqhd', probs.astype(v.dtype), v)

    out = lax.map(one_block, (qb, cqb, qpos))
    return out.transpose(1, 0, 2, 3, 4).reshape(bsz, t, h, dh)


def _moe(xf, w_router, router_bias, w_gate, w_up, w_down, w_sg, w_su, w_sd):
    f32 = jnp.float32
    n, d = xf.shape
    nb = -(-n // MOE_TOKEN_BLOCK)
    xp = jnp.pad(xf, ((0, nb * MOE_TOKEN_BLOCK - n), (0, 0))).reshape(nb, MOE_TOKEN_BLOCK, d)
    gsize = N_EXPERTS // N_EXPERT_GROUPS

    def one_block(xb):
        scores = jax.nn.sigmoid(xb.astype(f32) @ w_router.astype(f32))
        sel = scores + router_bias.astype(f32)
        grp_score = lax.top_k(sel.reshape(-1, N_EXPERT_GROUPS, gsize), 2)[0].sum(-1)
        _, gidx = lax.top_k(grp_score, TOPK_EXPERT_GROUPS)
        gmask = jax.nn.one_hot(gidx, N_EXPERT_GROUPS, dtype=f32).sum(1) > 0
        sel = jnp.where(jnp.repeat(gmask, gsize, axis=1), sel, -jnp.inf)
        _, eidx = lax.top_k(sel, TOP_K)
        w = jnp.take_along_axis(scores, eidx, axis=1)
        w = w / jnp.sum(w, axis=-1, keepdims=True) * ROUTED_SCALE
        comb = jnp.sum(jax.nn.one_hot(eidx, N_EXPERTS, dtype=f32) * w[..., None], axis=1)
        g = jnp.einsum('td,edf->tef', xb, w_gate)
        u = jnp.einsum('td,edf->tef', xb, w_up)
        act = jax.nn.silu(g) * u * comb[..., None]
        routed = jnp.einsum('tef,efd->td', act, w_down)
        shared = (jax.nn.silu(xb @ w_sg) * (xb @ w_su)) @ w_sd
        return (routed + shared).astype(f32)

    return lax.map(one_block, xp).reshape(nb * MOE_TOKEN_BLOCK, d)[:n]


def _trunk_layer(x, c, past_k, past_v, past_logf, ssm_x0, lp):
    bsz, t, _ = x.shape
    p0 = past_k.shape[1]
    ada = jax.nn.silu(c) @ lp['w_ada'] + lp['b_ada']
    sh1, sc1, g1, sh2, sc2, g2 = jnp.split(ada[:, None, :], 6, axis=-1)
    h = x * (1.0 + sc1) + sh1
    z = h @ lp['w_in']
    u, q, k, v, f_logit, gate_ssm, gate_fox = jnp.split(z, IN_SPLITS, axis=-1)
    y_s, ssm_last = _s5(u.reshape(bsz, t, SSM_GROUPS, SSM_GROUP_WIDTH), lp['ssm_lambda_re'], lp['ssm_lambda_im'],
                        lp['ssm_log_dt'], lp['ssm_b_re'], lp['ssm_b_im'], lp['ssm_c_re'], lp['ssm_c_im'], lp['ssm_d'], ssm_x0)
    glu_a, glu_b = jnp.split(jax.nn.gelu(y_s.reshape(bsz, t, D_SSM)) @ lp['w_glu'], 2, axis=-1)
    br_ssm = glu_a * jax.nn.sigmoid(glu_b)
    q = q.reshape(bsz, t, FOX_HEADS, FOX_HEAD_DIM)
    k = k.reshape(bsz, t, FOX_HEADS, FOX_HEAD_DIM)
    v = v.reshape(bsz, t, FOX_HEADS, FOX_HEAD_DIM)
    logf = jax.nn.log_sigmoid((f_logit + lp['b_f']).astype(jnp.float32))
    k_all = jnp.concatenate([past_k, k], axis=1)
    v_all = jnp.concatenate([past_v, v], axis=1)
    cum = jnp.cumsum(jnp.concatenate([past_logf.astype(jnp.float32), logf], axis=1), axis=1)
    attn = _fox_attention(q, k_all, v_all, cum[:, p0:], cum, p0)
    br_fox = attn.reshape(bsz, t, D_FOX) @ lp['w_fox_o']
    merged = jax.nn.sigmoid(gate_ssm) * br_ssm + jax.nn.sigmoid(gate_fox) * br_fox
    mix = merged @ lp['w_out']
    x = _layer_norm(ALPHA * x + (1.0 + g1) * mix, lp['ln1_g'], lp['ln1_b'])
    h2 = x * (1.0 + sc2) + sh2
    ffn = _moe(h2.reshape(bsz * t, D_MODEL), lp['w_router'], lp['router_bias'], lp['w_exp_gate'], lp['w_exp_up'],
               lp['w_exp_down'], lp['w_sh_gate'], lp['w_sh_up'], lp['w_sh_down']).reshape(bsz, t, D_MODEL)
    x = _layer_norm(ALPHA * x + (1.0 + g2) * ffn, lp['ln2_g'], lp['ln2_b'])
    return x, k, v, logf, jnp.real(ssm_last), jnp.imag(ssm_last)


def setup_inputs(seed: int = 0) -> dict:
    key = jax.random.key(seed)
    ks = iter(jax.random.split(key, 48))
    f32 = jnp.float32

    def nrm(shape, scale):
        return scale * jax.random.normal(next(ks), shape, f32)

    L, G, P, W = DEPTH, SSM_GROUPS, SSM_STATE, SSM_GROUP_WIDTH
    E, F = N_EXPERTS, EXPERT_DIM
    inv = D_MODEL ** -0.5
    return {
        'x_prompt': nrm((BATCH, SEQ, D_MODEL), 1.0),
        'x_sample': nrm((DEC_BATCH, DEC_SEQ, D_MODEL), 1.0),
        'cache_k': nrm((L, DEC_BATCH, PAST_LEN, FOX_HEADS, FOX_HEAD_DIM), 1.0),
        'cache_v': nrm((L, DEC_BATCH, PAST_LEN, FOX_HEADS, FOX_HEAD_DIM), 1.0),
        'cache_logf': jax.nn.log_sigmoid(3.0 + nrm((L, DEC_BATCH, PAST_LEN, FOX_HEADS), 1.0)),
        'state_ssm_re': nrm((L, DEC_BATCH, G, P), 0.1),
        'state_ssm_im': nrm((L, DEC_BATCH, G, P), 0.1),
        'c_prompt': nrm((BATCH, D_MODEL), 1.0),
        'c_sample': nrm((DEC_BATCH, D_MODEL), 1.0),
        'w_ada': nrm((L, D_MODEL, 6 * D_MODEL), 0.1 * inv),
        'b_ada': nrm((L, 6 * D_MODEL), 0.02),
        'w_in': nrm((L, D_MODEL, IN_COLS), inv),
        'b_f': jnp.linspace(1.0, 6.0, FOX_HEADS, dtype=f32) + nrm((L, FOX_HEADS), 0.1),
        'ssm_lambda_re': -0.5 + nrm((L, G, P), 0.01),
        'ssm_lambda_im': jnp.pi * jnp.arange(P, dtype=f32) + nrm((L, G, P), 0.01),
        'ssm_log_dt': jax.random.uniform(next(ks), (L, G), f32, math.log(1e-3), math.log(1e-1)),
        'ssm_b_re': nrm((L, G, P, W), (2.0 * W) ** -0.5),
        'ssm_b_im': nrm((L, G, P, W), (2.0 * W) ** -0.5),
        'ssm_c_re': nrm((L, G, W, P), (2.0 * P) ** -0.5),
        'ssm_c_im': nrm((L, G, W, P), (2.0 * P) ** -0.5),
        'ssm_d': nrm((L, G, W), 1.0),
        'w_glu': nrm((L, D_SSM, 2 * D_MODEL), D_SSM ** -0.5),
        'w_fox_o': nrm((L, D_FOX, D_MODEL), D_FOX ** -0.5),
        'w_out': nrm((L, D_MODEL, D_MODEL), BETA * inv),
        'ln1_g': 1.0 + nrm((L, D_MODEL), 0.02),
        'ln1_b': nrm((L, D_MODEL), 0.02),
        'w_router': nrm((L, D_MODEL, E), inv),
        'router_bias': nrm((L, E), 0.01),
        'w_exp_gate': nrm((L, E, D_MODEL, F), inv),
        'w_exp_up': nrm((L, E, D_MODEL, F), inv),
        'w_exp_down': nrm((L, E, F, D_MODEL), BETA * F ** -0.5),
        'w_sh_gate': nrm((L, D_MODEL, SHARED_DIM), inv),
        'w_sh_up': nrm((L, D_MODEL, SHARED_DIM), inv),
        'w_sh_down': nrm((L, SHARED_DIM, D_MODEL), BETA * SHARED_DIM ** -0.5),
        'ln2_g': 1.0 + nrm((L, D_MODEL), 0.02),
        'ln2_b': nrm((L, D_MODEL), 0.02),
    }


def reference(x_prompt, x_sample, cache_k, cache_v, cache_logf, state_ssm_re, state_ssm_im, c_prompt, c_sample,
              w_ada, b_ada, w_in, b_f, ssm_lambda_re, ssm_lambda_im, ssm_log_dt, ssm_b_re, ssm_b_im, ssm_c_re,
              ssm_c_im, ssm_d, w_glu, w_fox_o, w_out, ln1_g, ln1_b, w_router, router_bias, w_exp_gate, w_exp_up,
              w_exp_down, w_sh_gate, w_sh_up, w_sh_down, ln2_g, ln2_b):
    f32 = jnp.float32
    bp = x_prompt.shape[0]
    yp, ys = x_prompt, x_sample
    outs_p, outs_s = [], []
    for l in range(DEPTH):
        lp = {'w_ada': w_ada[l], 'b_ada': b_ada[l], 'w_in': w_in[l], 'b_f': b_f[l],
              'ssm_lambda_re': ssm_lambda_re[l], 'ssm_lambda_im': ssm_lambda_im[l], 'ssm_log_dt': ssm_log_dt[l],
              'ssm_b_re': ssm_b_re[l], 'ssm_b_im': ssm_b_im[l], 'ssm_c_re': ssm_c_re[l], 'ssm_c_im': ssm_c_im[l],
              'ssm_d': ssm_d[l], 'w_glu': w_glu[l], 'w_fox_o': w_fox_o[l], 'w_out': w_out[l],
              'ln1_g': ln1_g[l], 'ln1_b': ln1_b[l], 'w_router': w_router[l], 'router_bias': router_bias[l],
              'w_exp_gate': w_exp_gate[l], 'w_exp_up': w_exp_up[l], 'w_exp_down': w_exp_down[l],
              'w_sh_gate': w_sh_gate[l], 'w_sh_up': w_sh_up[l], 'w_sh_down': w_sh_down[l],
              'ln2_g': ln2_g[l], 'ln2_b': ln2_b[l]}
        empty_kv = jnp.zeros((bp, 0, FOX_HEADS, FOX_HEAD_DIM), x_prompt.dtype)
        empty_logf = jnp.zeros((bp, 0, FOX_HEADS), f32)
        x0_p = jnp.zeros((bp, SSM_GROUPS, SSM_STATE), jnp.complex64)
        res_p = _trunk_layer(yp, c_prompt, empty_kv, empty_kv, empty_logf, x0_p, lp)
        x0_s = lax.complex(state_ssm_re[l].astype(f32), state_ssm_im[l].astype(f32))
        res_s = _trunk_layer(ys, c_sample, cache_k[l], cache_v[l], cache_logf[l], x0_s, lp)
        yp, ys = res_p[0], res_s[0]
        outs_p.append(res_p)
        outs_s.append(res_s)
    new_k_prompt = jnp.stack([o[1] for o in outs_p])
    new_v_prompt = jnp.stack([o[2] for o in outs_p])
    new_logf_prompt = jnp.stack([o[3] for o in outs_p])
    new_ssm_re_prompt = jnp.stack([o[4] for o in outs_p])
    new_ssm_im_prompt = jnp.stack([o[5] for o in outs_p])
    new_k_sample = jnp.stack([o[1] for o in outs_s])
    new_v_sample = jnp.stack([o[2] for o in outs_s])
    new_logf_sample = jnp.stack([o[3] for o in outs_s])
    new_ssm_re_sample = jnp.stack([o[4] for o in outs_s])
    new_ssm_im_sample = jnp.stack([o[5] for o in outs_s])
    return (yp, ys, new_k_prompt, new_v_prompt, new_logf_prompt, new_ssm_re_prompt, new_ssm_im_prompt,
            new_k_sample, new_v_sample, new_logf_sample, new_ssm_re_sample, new_ssm_im_sample)
```

```python
import functools

import jax
import jax.numpy as jnp
import numpy as np
from jax import lax
from jax.experimental import pallas as pl
from jax.experimental.pallas import tpu as pltpu

F32 = jnp.float32
BF16 = jnp.bfloat16
I32 = jnp.int32

V7X_VMEM_BYTES = 64 * 1024 * 1024
VMEM_LIMIT_BYTES = V7X_VMEM_BYTES - 8 * 1024 * 1024
LANES = 128
SUBLANES = 8

SSM_GROUP_WIDTH = 16
SSM_STATE = 64
SSM_GROUPS_PER_SLAB = 8
FOX_HEAD_DIM = 64
N_EXPERT_GROUPS = 8
TOPK_EXPERT_GROUPS = 4
TOP_K = 8
ROUTED_SCALE = 2.5
LN_EPS = 1e-5
ADA_BLOCK = 64

NT_DIMS = (((1,), (1,)), ((), ()))


def _params(sem, vmem=VMEM_LIMIT_BYTES):
    return pltpu.CompilerParams(dimension_semantics=sem, vmem_limit_bytes=vmem)


def _dot(a, b):
    return jnp.dot(a, b, preferred_element_type=F32)


def _dot_nt(a, b):
    return lax.dot_general(a, b, NT_DIMS, preferred_element_type=F32)


def _split_bf16(x):
    hi = x.astype(BF16)
    lo = (x - hi.astype(F32)).astype(BF16)
    return hi, lo


def _log_sigmoid(x):
    return jnp.minimum(x, 0.0) - jnp.log1p(jnp.exp(-jnp.abs(x)))


def _gelu_tanh(x):
    c = np.float32(np.sqrt(2.0 / np.pi))
    return x * (0.5 * (1.0 + jnp.tanh(c * (x + 0.044715 * (x * x * x)))))


def _layer_norm(y, g, b):
    mu = jnp.mean(y, axis=-1, keepdims=True)
    yc = y - mu
    var = jnp.mean(yc * yc, axis=-1, keepdims=True)
    return yc * lax.rsqrt(var + LN_EPS) * g + b


def _ada_kernel(c_ref, w_ref, b_ref, o_ref):
    c = c_ref[...]
    a_hi, a_lo = _split_bf16(c * jax.nn.sigmoid(c))
    w_hi, w_lo = _split_bf16(w_ref[...])
    acc = _dot(a_hi, w_lo) + _dot(a_lo, w_hi)
    o_ref[...] = acc + _dot(a_hi, w_hi) + b_ref[...]


def _ada(c_pad, w_ada, b_ada):
    m, d = c_pad.shape
    n = w_ada.shape[1]
    tn = 1024
    return pl.pallas_call(
        _ada_kernel,
        out_shape=jax.ShapeDtypeStruct((m, n), F32),
        grid=(n // tn,),
        in_specs=[pl.BlockSpec((m, d), lambda j: (0, 0)),
                  pl.BlockSpec((d, tn), lambda j: (0, j)),
                  pl.BlockSpec((1, tn), lambda j: (0, j))],
        out_specs=pl.BlockSpec((m, tn), lambda j: (0, j)),
        compiler_params=_params(("arbitrary",)),
        name="ada",
    )(c_pad, w_ada, b_ada.reshape(1, n))


def _inproj_kernel(x_ref, ada_ref, w_ref, wf_ref, bf_ref, z_ref, qkv_ref, lf_ref, h_scr,
                   *, nsub, q_scale):
    j = pl.program_id(1)

    @pl.when(j == 0)
    def _():
        for s in range(nsub):
            rows = slice(s * ADA_BLOCK, (s + 1) * ADA_BLOCK)
            sh = ada_ref[s, 0:1, :]
            sc = ada_ref[s, 1:2, :]
            h_scr[rows, :] = (x_ref[rows, :] * (1.0 + sc) + sh).astype(BF16)
        f = _dot_nt(wf_ref[...], h_scr[...])
        lf_ref[...] = _log_sigmoid(f + bf_ref[...])

    zt = _dot(h_scr[...], w_ref[...])
    z_ref[...] = zt

    @pl.when(j == 1)
    def _():
        qkv_ref[...] = (zt * q_scale).astype(BF16)

    @pl.when((j == 2) | (j == 3))
    def _():
        qkv_ref[...] = zt.astype(BF16)


def _inproj(x_all, ada_blk, w_main, wf_t, bf_col, tm=512):
    n, d = x_all.shape
    ds = d // 2
    h = wf_t.shape[0]
    nsub = tm // ADA_BLOCK
    ncol = w_main.shape[1] // ds
    kern = functools.partial(_inproj_kernel, nsub=nsub, q_scale=FOX_HEAD_DIM ** -0.5)
    return pl.pallas_call(
        kern,
        out_shape=(jax.ShapeDtypeStruct((n, w_main.shape[1]), F32),
                   jax.ShapeDtypeStruct((n, 3 * ds), BF16),
                   jax.ShapeDtypeStruct((h, n), F32)),
        grid=(n // tm, ncol),
        in_specs=[pl.BlockSpec((tm, d), lambda i, j: (i, 0)),
                  pl.BlockSpec((nsub, 6, d), lambda i, j: (i, 0, 0)),
                  pl.BlockSpec((d, ds), lambda i, j: (0, j)),
                  pl.BlockSpec((h, d), lambda i, j: (0, 0)),
                  pl.BlockSpec((h, 1), lambda i, j: (0, 0))],
        out_specs=(pl.BlockSpec((tm, ds), lambda i, j: (i, j)),
                   pl.BlockSpec((tm, ds), lambda i, j: (i, jnp.clip(j - 1, 0, 2))),
                   pl.BlockSpec((h, tm), lambda i, j: (0, i))),
        scratch_shapes=[pltpu.VMEM((tm, d), BF16)],
        compiler_params=_params(("arbitrary", "arbitrary")),
        name="inproj",
    )(x_all, ada_blk, w_main, wf_t, bf_col)


def _ssm_disc_kernel(lr_ref, li_ref, ldt_ref, br_ref, bi_ref,
                     lbr_ref, lbi_ref, bbr_ref, bbi_ref):
    lr = jnp.minimum(lr_ref[...], -1e-4)
    li = li_ref[...]
    dt = jnp.exp(ldt_ref[...])
    er = jnp.exp(lr * dt)
    lbr = er * jnp.cos(li * dt)
    lbi = er * jnp.sin(li * dt)
    lbr_ref[...] = lbr
    lbi_ref[...] = lbi
    nr = lbr - 1.0
    den = lr * lr + li * li
    qr = (nr * lr + lbi * li) / den
    qi = (lbi * lr - nr * li) / den
    b_r = br_ref[...]
    b_i = bi_ref[...]
    bbr_ref[...] = qr * b_r - qi * b_i
    bbi_ref[...] = qr * b_i + qi * b_r


def _ssm_disc(lam_re, lam_im, log_dt, b_re, b_im):
    g, p, w = b_re.shape
    rep = lambda a: jnp.repeat(a, w, axis=1)
    shp = jax.ShapeDtypeStruct((g, p * w), F32)
    lbr, lbi, bbr, bbi = pl.pallas_call(
        _ssm_disc_kernel, out_shape=(shp, shp, shp, shp), name="ssm_disc",
    )(rep(lam_re), rep(lam_im), log_dt.reshape(g, 1),
      b_re.reshape(g, p * w), b_im.reshape(g, p * w))
    return (lbr[:, ::w], lbi[:, ::w], bbr.reshape(g, p, w), bbi.reshape(g, p, w))


def _to_state_layout(re, im):
    s, g, p = re.shape
    ns = g // SSM_GROUPS_PER_SLAB
    r = re.reshape(s, ns, 1, SSM_GROUPS_PER_SLAB * p)
    i = im.reshape(s, ns, 1, SSM_GROUPS_PER_SLAB * p)
    return jnp.concatenate([r, i], axis=2).reshape(s, 2 * g * p)


def _from_state_layout(x, g, p):
    s = x.shape[0]
    y = x.reshape(s, g // SSM_GROUPS_PER_SLAB, 2, SSM_GROUPS_PER_SLAB, p)
    return y[:, :, 0].reshape(s, g, p), y[:, :, 1].reshape(s, g, p)


def _block_diag_slabs(a):
    g, m, n = a.shape
    k = SSM_GROUPS_PER_SLAB
    a4 = a.reshape(g // k, k, m, n)
    eye = jnp.eye(k, dtype=bool)
    out = jnp.where(eye[None, :, None, :, None], a4[:, :, :, None, :], 0.0)
    return out.reshape(g // k, k * m, k * n)


def _ssm_kernel(*refs, n_refs, rpr, s_blk, tt, n_slab, sw):
    u_refs = refs[:n_refs]
    x0_ref, lam_ref, bd_ref, cd_ref, d_ref = refs[n_refs:n_refs + 5]
    o_refs = refs[n_refs + 5:2 * n_refs + 5]
    xl_ref = refs[2 * n_refs + 5]
    u16, bu, st = refs[2 * n_refs + 6:]
    tb = pl.program_id(1)
    uw = SSM_GROUPS_PER_SLAB * SSM_GROUP_WIDTH

    @pl.when(tb == 0)
    def _():
        st[...] = x0_ref[...]

    for s in range(n_refs):
        u16[s * rpr:(s + 1) * rpr, :] = u_refs[s][...].astype(BF16)
    npl = sw // LANES
    nph = npl // 2
    for j in range(n_slab):
        res = _dot(u16[:, j * uw:(j + 1) * uw], bd_ref[j])
        for q in range(npl):
            bu[j * npl + q] = res[:, q * LANES:(q + 1) * LANES]

    for j in range(n_slab):
        c0 = j * sw
        ar = [jnp.broadcast_to(lam_ref[:, c0 + q * LANES:c0 + (q + 1) * LANES], (s_blk, LANES))
              for q in range(npl)]

        def body(t, carry, j=j, ar=ar):
            rows = pl.ds(t, s_blk, stride=tt)
            new = [None] * npl
            for q in range(nph):
                xr, xi = carry[q], carry[nph + q]
                a_r, a_i = ar[q], ar[nph + q]
                nr = a_r * xr - a_i * xi + bu[j * npl + q, rows, :]
                ni = a_r * xi + a_i * xr + bu[j * npl + nph + q, rows, :]
                bu[j * npl + q, rows, :] = nr
                bu[j * npl + nph + q, rows, :] = ni
                new[q], new[nph + q] = nr, ni
            return tuple(new)

        init = tuple(st[:, c0 + q * LANES:c0 + (q + 1) * LANES] for q in range(npl))
        fin = lax.fori_loop(0, tt, body, init, unroll=8)
        for q in range(npl):
            st[:, c0 + q * LANES:c0 + (q + 1) * LANES] = fin[q]

    for j in range(n_slab):
        xr16 = jnp.concatenate([bu[j * npl + q] for q in range(nph)], axis=1).astype(BF16)
        xi16 = jnp.concatenate([bu[j * npl + nph + q] for q in range(nph)], axis=1).astype(BF16)
        y = _dot(xr16, cd_ref[j, 0]) - _dot(xi16, cd_ref[j, 1])
        cols = slice(j * uw, (j + 1) * uw)
        for s in range(n_refs):
            ys = y[s * rpr:(s + 1) * rpr, :] + d_ref[:, cols] * u_refs[s][:, cols]
            o_refs[s][:, cols] = _gelu_tanh(ys).astype(BF16)

    @pl.when(tb == pl.num_programs(1) - 1)
    def _():
        xl_ref[...] = st[...]


def _ssm(z32, x0_lay, lam_lay, bd, cd, d_row, *, row0, n_seq, seq_len, s_blk, tt):
    n_slab = bd.shape[0]
    uw = bd.shape[1]
    sw = bd.shape[2]
    ds = n_slab * uw
    state_w = n_slab * sw
    n_tb = seq_len // tt
    n_sg = n_seq // s_blk
    if n_tb == 1:
        n_refs, rpr = 1, s_blk * tt
        assert row0 % rpr == 0
        in_maps = [lambda sg, tb: (row0 // rpr + sg, 0)]
        out_shape = [jax.ShapeDtypeStruct((n_seq * seq_len, ds), BF16)]
        out_maps = [lambda sg, tb: (sg, 0)]
    else:
        assert n_sg == 1 and row0 == 0
        n_refs, rpr = s_blk, tt
        in_maps = [functools.partial(lambda sg, tb, s: (s * n_tb + tb, 0), s=s) for s in range(s_blk)]
        out_shape = [jax.ShapeDtypeStruct((seq_len, ds), BF16)] * s_blk
        out_maps = [lambda sg, tb: (tb, 0)] * s_blk
    rows = n_refs * rpr
    kern = functools.partial(_ssm_kernel, n_refs=n_refs, rpr=rpr, s_blk=s_blk, tt=tt,
                             n_slab=n_slab, sw=sw)
    outs = pl.pallas_call(
        kern,
        out_shape=tuple(out_shape) + (jax.ShapeDtypeStruct((n_seq, state_w), F32),),
        grid=(n_sg, n_tb),
        in_specs=[pl.BlockSpec((rpr, ds), m) for m in in_maps] + [
            pl.BlockSpec((s_blk, state_w), lambda sg, tb: (sg, 0)),
            pl.BlockSpec((1, state_w), lambda sg, tb: (0, 0)),
            pl.BlockSpec(bd.shape, lambda sg, tb: (0, 0, 0)),
            pl.BlockSpec(cd.shape, lambda sg, tb: (0, 0, 0, 0)),
            pl.BlockSpec((1, ds), lambda sg, tb: (0, 0))],
        out_specs=tuple(pl.BlockSpec((rpr, ds), m) for m in out_maps) + (
            pl.BlockSpec((s_blk, state_w), lambda sg, tb: (sg, 0)),),
        scratch_shapes=[pltpu.VMEM((rows, ds), BF16),
                        pltpu.VMEM((state_w // LANES, rows, LANES), F32),
                        pltpu.VMEM((s_blk, state_w), F32)],
        compiler_params=_params(("arbitrary", "arbitrary")),
        name="ssm",
    )(*([z32] * n_refs), x0_lay, lam_lay, bd, cd, d_row)
    return list(outs[:-1]), outs[-1]


def _cumsum_kernel(x_ref, o_ref, *, blk):
    r, t = x_ref.shape
    row = lax.broadcasted_iota(I32, (blk, blk), 0)
    col = lax.broadcasted_iota(I32, (blk, blk), 1)
    upper = jnp.where(row <= col, 1.0, 0.0).astype(BF16)
    carry = jnp.zeros((r, 1), F32)
    for c in range(t // blk):
        x = x_ref[:, c * blk:(c + 1) * blk]
        h1 = x.astype(BF16)
        r1 = x - h1.astype(F32)
        h2 = r1.astype(BF16)
        h3 = (r1 - h2.astype(F32)).astype(BF16)
        s = (_dot(h3, upper) + _dot(h2, upper)) + _dot(h1, upper) + carry
        o_ref[:, c * blk:(c + 1) * blk] = s
        carry = s[:, blk - 1:blk]


def _cumsum_lanes(x, blk=256, tr=64):
    r, t = x.shape
    tr = min(tr, r)
    return pl.pallas_call(
        functools.partial(_cumsum_kernel, blk=blk),
        out_shape=jax.ShapeDtypeStruct((r, t), F32),
        grid=(r // tr,),
        in_specs=[pl.BlockSpec((tr, t), lambda i: (i, 0))],
        out_specs=pl.BlockSpec((tr, t), lambda i: (i, 0)),
        compiler_params=_params(("arbitrary",)),
        name="cumsum",
    )(x)


def _attn_step(q, k, v, ck, m_scr, l_scr, acc_scr, mask):
    lane = lax.broadcasted_iota(I32, (1, 2 * FOX_HEAD_DIM), 1)
    lo = lane < FOX_HEAD_DIM
    pvs, alphas = [], []
    for h in range(2):
        sel = lo if h == 0 else jnp.logical_not(lo)
        qh = jnp.where(sel, q, jnp.zeros_like(q))
        s = _dot_nt(qh, k) - ck[h:h + 1, :]
        if mask is not None:
            s = jnp.where(mask, s, -jnp.inf)
        m_prev = m_scr[h]
        m_new = jnp.maximum(m_prev, jnp.max(s, axis=-1, keepdims=True))
        alpha = jnp.exp(m_prev - m_new)
        p = jnp.exp(s - m_new[:, :1])
        l_scr[h] = alpha * l_scr[h] + jnp.sum(p, axis=-1, keepdims=True)
        m_scr[h] = m_new
        pvs.append(_dot(p.astype(BF16), v))
        alphas.append(alpha)
    acc_scr[...] = (jnp.where(lo, alphas[0], alphas[1]) * acc_scr[...]
                    + jnp.where(lo, pvs[0], pvs[1]))


def _attn_init(m_scr, l_scr, acc_scr):
    m_scr[...] = jnp.full(m_scr.shape, -jnp.inf, F32)
    l_scr[...] = jnp.zeros(l_scr.shape, F32)
    acc_scr[...] = jnp.zeros(acc_scr.shape, F32)


def _attn_finish(o_ref, l_scr, acc_scr):
    lane = lax.broadcasted_iota(I32, (1, 2 * FOX_HEAD_DIM), 1)
    l = jnp.where(lane < FOX_HEAD_DIM, l_scr[0], l_scr[1])
    o_ref[...] = (acc_scr[...] / l).astype(o_ref.dtype)


def _causal_mask(tq, tk):
    return (lax.broadcasted_iota(I32, (tq, tk), 1) <= lax.broadcasted_iota(I32, (tq, tk), 0))


def _attn_prompt_kernel(q_ref, k_ref, v_ref, ck_ref, o_ref, m_scr, l_scr, acc_scr, *, tq):
    qi, ki = pl.program_id(2), pl.program_id(3)

    @pl.when(ki == 0)
    def _():
        _attn_init(m_scr, l_scr, acc_scr)

    @pl.when(ki < qi)
    def _():
        _attn_step(q_ref[...], k_ref[...], v_ref[...], ck_ref[0, 0], m_scr, l_scr, acc_scr, None)

    @pl.when(ki == qi)
    def _():
        _attn_step(q_ref[...], k_ref[...], v_ref[...], ck_ref[0, 0], m_scr, l_scr, acc_scr,
                   _causal_mask(tq, tq))
        _attn_finish(o_ref, l_scr, acc_scr)


def _attn_prompt(qkv16, ck, *, n_batch, seq_len, n_hp, tq=512):
    nq = seq_len // tq
    lw = 2 * FOX_HEAD_DIM
    kv_map = lambda c: (lambda b, hp, qi, ki: (b * nq + jnp.minimum(ki, qi), c * n_hp + hp))
    return pl.pallas_call(
        functools.partial(_attn_prompt_kernel, tq=tq),
        out_shape=jax.ShapeDtypeStruct((n_batch * seq_len, n_hp * lw), BF16),
        grid=(n_batch, n_hp, nq, nq),
        in_specs=[pl.BlockSpec((tq, lw), lambda b, hp, qi, ki: (b * nq + qi, hp)),
                  pl.BlockSpec((tq, lw), kv_map(1)),
                  pl.BlockSpec((tq, lw), kv_map(2)),
                  pl.BlockSpec((1, 1, 2, tq), lambda b, hp, qi, ki: (b, hp, 0, jnp.minimum(ki, qi)))],
        out_specs=pl.BlockSpec((tq, lw), lambda b, hp, qi, ki: (b * nq + qi, hp)),
        scratch_shapes=[pltpu.VMEM((2, tq, lw), F32), pltpu.VMEM((2, tq, lw), F32),
                        pltpu.VMEM((tq, lw), F32)],
        compiler_params=_params(("arbitrary",) * 4),
        name="attn_prompt",
    )(qkv16, qkv16, qkv16, ck)


def _attn_sample_kernel(q_ref, kp_ref, vp_ref, kn_ref, vn_ref, ckp_ref, ckn_ref, o_ref,
                        m_scr, l_scr, acc_scr, *, nkp, tq):
    ki = pl.program_id(2)

    @pl.when(ki == 0)
    def _():
        _attn_init(m_scr, l_scr, acc_scr)

    @pl.when(ki < nkp)
    def _():
        _attn_step(q_ref[...], kp_ref[0].astype(BF16), vp_ref[0].astype(BF16), ckp_ref[0, 0],
                   m_scr, l_scr, acc_scr, None)

    @pl.when(ki == nkp)
    def _():
        _attn_step(q_ref[...], kn_ref[...], vn_ref[...], ckn_ref[0, 0], m_scr, l_scr, acc_scr,
                   _causal_mask(tq, tq))
        _attn_finish(o_ref, l_scr, acc_scr)


def _attn_sample(qkv16, k_past, v_past, ck_past, ck_new, *, row0, n_batch, seq_len, n_hp, tk=512):
    past = k_past.shape[1]
    nkp = past // tk
    lw = 2 * FOX_HEAD_DIM
    rb0 = row0 // seq_len
    new_map = lambda c: (lambda b, hp, ki: (rb0 + b, c * n_hp + hp))
    past_map = lambda b, hp, ki: (b, jnp.minimum(ki, nkp - 1), hp)
    return pl.pallas_call(
        functools.partial(_attn_sample_kernel, nkp=nkp, tq=seq_len),
        out_shape=jax.ShapeDtypeStruct((n_batch * seq_len, n_hp * lw), BF16),
        grid=(n_batch, n_hp, nkp + 1),
        in_specs=[pl.BlockSpec((seq_len, lw), new_map(0)),
                  pl.BlockSpec((1, tk, lw), past_map),
                  pl.BlockSpec((1, tk, lw), past_map),
                  pl.BlockSpec((seq_len, lw), new_map(1)),
                  pl.BlockSpec((seq_len, lw), new_map(2)),
                  pl.BlockSpec((1, 1, 2, tk), lambda b, hp, ki: (b, hp, 0, jnp.minimum(ki, nkp - 1))),
                  pl.BlockSpec((1, 1, 2, seq_len), lambda b, hp, ki: (b, hp, 0, 0))],
        out_specs=pl.BlockSpec((seq_len, lw), lambda b, hp, ki: (b, hp)),
        scratch_shapes=[pltpu.VMEM((2, seq_len, lw), F32), pltpu.VMEM((2, seq_len, lw), F32),
                        pltpu.VMEM((seq_len, lw), F32)],
        compiler_params=_params(("arbitrary",) * 3),
        name="attn_sample",
    )(qkv16, k_past, v_past, qkv16, qkv16, ck_past, ck_new)


def _glu_kernel(g_ref, wa_ref, wb_ref, o_ref):
    g = g_ref[...]
    o_ref[...] = _dot(g, wa_ref[...]) * jax.nn.sigmoid(_dot(g, wb_ref[...]))


def _glu(g16, w_glu16, tm=512, tn=1024):
    n, ds = g16.shape
    d = w_glu16.shape[1] // 2
    tn = min(tn, d)
    nb = d // tn
    return pl.pallas_call(
        _glu_kernel,
        out_shape=jax.ShapeDtypeStruct((n, d), F32),
        grid=(n // tm, nb),
        in_specs=[pl.BlockSpec((tm, ds), lambda i, j: (i, 0)),
                  pl.BlockSpec((ds, tn), lambda i, j: (0, j)),
                  pl.BlockSpec((ds, tn), lambda i, j: (0, nb + j))],
        out_specs=pl.BlockSpec((tm, tn), lambda i, j: (i, j)),
        compiler_params=_params(("arbitrary", "arbitrary")),
        name="glu",
    )(g16, w_glu16, w_glu16)


def _post_kernel(att_ref, brs_ref, gs_ref, gf_ref, x_ref, ada_ref, wfo_ref, wo_ref, lg_ref, lb_ref,
                 o_ref, *, nsub, alpha):
    br_fox = _dot(att_ref[...], wfo_ref[...])
    merged = jax.nn.sigmoid(gs_ref[...]) * brs_ref[...] + jax.nn.sigmoid(gf_ref[...]) * br_fox
    mix = _dot(merged.astype(BF16), wo_ref[...])
    for s in range(nsub):
        rows = slice(s * ADA_BLOCK, (s + 1) * ADA_BLOCK)
        g1 = ada_ref[s, 2:3, :]
        y = alpha * x_ref[rows, :] + (1.0 + g1) * mix[rows, :]
        o_ref[rows, :] = _layer_norm(y, lg_ref[...], lb_ref[...])


def _post(attn16, br_ssm, z32, x_all, ada_blk, w_fox16, w_out16, ln_g, ln_b, *, alpha, tm=256):
    n, d = x_all.shape
    ds = d // 2
    nsub = tm // ADA_BLOCK
    return pl.pallas_call(
        functools.partial(_post_kernel, nsub=nsub, alpha=alpha),
        out_shape=jax.ShapeDtypeStruct((n, d), F32),
        grid=(n // tm,),
        in_specs=[pl.BlockSpec((tm, ds), lambda i: (i, 0)),
                  pl.BlockSpec((tm, d), lambda i: (i, 0)),
                  pl.BlockSpec((tm, d), lambda i: (i, 2)),
                  pl.BlockSpec((tm, d), lambda i: (i, 3)),
                  pl.BlockSpec((tm, d), lambda i: (i, 0)),
                  pl.BlockSpec((nsub, 6, d), lambda i: (i, 0, 0)),
                  pl.BlockSpec((ds, d), lambda i: (0, 0)),
                  pl.BlockSpec((d, d), lambda i: (0, 0)),
                  pl.BlockSpec((1, d), lambda i: (0, 0)),
                  pl.BlockSpec((1, d), lambda i: (0, 0))],
        out_specs=pl.BlockSpec((tm, d), lambda i: (i, 0)),
        compiler_params=_params(("arbitrary",)),
        name="post_mix",
    )(attn16, br_ssm, z32, z32, x_all, ada_blk, w_fox16, w_out16, ln_g, ln_b)


def _router_kernel(x_ref, ada_ref, wr_ref, rb_ref, h_ref, e_ref, w_ref, *, nsub):
    ng = N_EXPERT_GROUPS
    for s in range(nsub):
        rows = slice(s * ADA_BLOCK, (s + 1) * ADA_BLOCK)
        h_ref[rows, :] = x_ref[rows, :] * (1.0 + ada_ref[s, 4:5, :]) + ada_ref[s, 3:4, :]
    h_hi, h_lo = _split_bf16(h_ref[...])
    w_hi, w_lo = _split_bf16(wr_ref[...])
    logits = (_dot_nt(w_hi, h_lo) + _dot_nt(w_lo, h_hi)) + _dot_nt(w_hi, h_hi)
    scores = jax.nn.sigmoid(logits)
    sel = scores + rb_ref[...]
    gsz = sel.shape[0] // ng
    tm = sel.shape[1]
    xs = [sel[j * ng:(j + 1) * ng, :] for j in range(gsz)]
    sc = [scores[j * ng:(j + 1) * ng, :] for j in range(gsz)]
    neg = -jnp.inf

    def lmax(v):
        out = v[0]
        for a in v[1:]:
            out = jnp.maximum(out, a)
        return out

    def lmin(v):
        out = v[0]
        for a in v[1:]:
            out = jnp.minimum(out, a)
        return out

    m1 = lmax(xs)
    i1 = lmin([jnp.where(xs[j] == m1, j, gsz) for j in range(gsz)])
    m2 = lmax([jnp.where(i1 == j, neg, xs[j]) for j in range(gsz)])
    cur = m1 + m2
    giota = lax.broadcasted_iota(I32, (ng, tm), 0)
    gsel = jnp.zeros((ng, tm), F32)
    for _ in range(TOPK_EXPERT_GROUPS):
        m = jnp.max(cur, axis=0, keepdims=True)
        gi = jnp.min(jnp.where(cur == m, giota, ng), axis=0, keepdims=True)
        hit = giota == gi
        gsel = jnp.where(hit, 1.0, gsel)
        cur = jnp.where(hit, neg, cur)
    gmask = gsel > 0.0
    xs = [jnp.where(gmask, x, neg) for x in xs]
    eid = [giota * gsz + j for j in range(gsz)]
    n_e = ng * gsz
    vals = []
    for r in range(TOP_K):
        m = jnp.max(lmax(xs), axis=0, keepdims=True)
        ci = jnp.min(lmin([jnp.where(xs[j] == m, eid[j], n_e) for j in range(gsz)]),
                     axis=0, keepdims=True)
        hits = [eid[j] == ci for j in range(gsz)]
        v = sum(jnp.where(hits[j], sc[j], 0.0) for j in range(gsz))
        vals.append(jnp.sum(v, axis=0, keepdims=True))
        xs = [jnp.where(hits[j], neg, xs[j]) for j in range(gsz)]
        e_ref[r:r + 1, :] = ci
    tot = sum(vals)
    for r in range(TOP_K):
        w_ref[r:r + 1, :] = vals[r] / tot * ROUTED_SCALE


def _router(x1, ada_blk, wr_perm, rb_perm, tm=512):
    n, d = x1.shape
    e = wr_perm.shape[0]
    nsub = tm // ADA_BLOCK
    return pl.pallas_call(
        functools.partial(_router_kernel, nsub=nsub),
        out_shape=(jax.ShapeDtypeStruct((n, d), F32),
                   jax.ShapeDtypeStruct((TOP_K, n), I32),
                   jax.ShapeDtypeStruct((TOP_K, n), F32)),
        grid=(n // tm,),
        in_specs=[pl.BlockSpec((tm, d), lambda i: (i, 0)),
                  pl.BlockSpec((nsub, 6, d), lambda i: (i, 0, 0)),
                  pl.BlockSpec((e, d), lambda i: (0, 0)),
                  pl.BlockSpec((e, 1), lambda i: (0, 0))],
        out_specs=(pl.BlockSpec((tm, d), lambda i: (i, 0)),
                   pl.BlockSpec((TOP_K, tm), lambda i: (0, i)),
                   pl.BlockSpec((TOP_K, tm), lambda i: (0, i))),
        compiler_params=_params(("arbitrary",)),
        name="router",
    )(x1, ada_blk, wr_perm, rb_perm)


def _plan_kernel(e_ref, pos_ref, cnt_ref, rank_scr, *, n_e, blk):
    n = e_ref.shape[1]
    nblk = n // blk
    row = lax.broadcasted_iota(I32, (blk, blk), 0)
    col = lax.broadcasted_iota(I32, (blk, blk), 1)
    upper = jnp.where(row <= col, 1.0, 0.0).astype(BF16)
    eid = lax.broadcasted_iota(I32, (n_e, blk), 0)

    def count_body(cb, carry):
        cols = pl.ds(pl.multiple_of(cb * blk, blk), blk)
        e_blk = e_ref[:, cols]
        hit = jnp.zeros((n_e, blk), F32)
        for k in range(TOP_K):
            hit = hit + jnp.where(e_blk[k:k + 1, :] == eid, 1.0, 0.0)
        cs = _dot(hit.astype(BF16), upper) + carry
        rank_scr[:, cols] = cs - hit
        return cs[:, blk - 1:blk]

    counts = lax.fori_loop(0, nblk, count_body, jnp.zeros((n_e, 1), F32))
    cnt_ref[...] = jnp.broadcast_to(counts, cnt_ref.shape).astype(I32)

    hi = jnp.floor(counts * (1.0 / 128.0))
    lo = counts - hi * 128.0
    er = lax.broadcasted_iota(I32, (n_e, n_e), 0)
    ec = lax.broadcasted_iota(I32, (n_e, n_e), 1)
    lower = jnp.where(ec < er, 1.0, 0.0).astype(BF16)
    wide = lambda v: jnp.broadcast_to(v, (n_e, LANES)).astype(BF16)
    starts = (_dot(lower, wide(hi)) * 128.0 + _dot(lower, wide(lo)))[:, :1]

    def pos_body(cb, c):
        cols = pl.ds(pl.multiple_of(cb * blk, blk), blk)
        e_blk = e_ref[:, cols]
        val = rank_scr[:, cols] + starts
        for k in range(TOP_K):
            p = jnp.sum(jnp.where(e_blk[k:k + 1, :] == eid, val, 0.0), axis=0, keepdims=True)
            pos_ref[k:k + 1, cols] = p.astype(I32)
        return c

    lax.fori_loop(0, nblk, pos_body, 0)


def _plan(eidx_t, n_e, blk=256):
    k, n = eidx_t.shape
    return pl.pallas_call(
        functools.partial(_plan_kernel, n_e=n_e, blk=blk),
        out_shape=(jax.ShapeDtypeStruct((k, n), I32), jax.ShapeDtypeStruct((n_e, LANES), I32)),
        scratch_shapes=[pltpu.VMEM((n_e, n), F32)],
        compiler_params=pltpu.CompilerParams(vmem_limit_bytes=VMEM_LIMIT_BYTES),
        name="moe_plan",
    )(eidx_t)


def _visit_tables(counts, tm, n_rows):
    n_e = counts.shape[0]
    ends = jnp.cumsum(counts)
    starts = ends - counts
    first = starts // tm
    nvis = jnp.where(counts > 0, (ends - 1) // tm - first + 1, 0)
    vend = jnp.cumsum(nvis)
    vstart = vend - nvis
    total = vend[-1]
    n_visits = n_rows // tm + n_e - 1
    v = jnp.minimum(jnp.arange(n_visits, dtype=I32), total - 1)
    e = jnp.sum((vend[None, :] <= v[:, None]).astype(I32), axis=1)
    tile = (first[e] + (v - vstart[e])).astype(I32)
    valid = (jnp.arange(n_visits, dtype=I32) < total).astype(I32)
    bounds = jnp.concatenate([starts, ends[-1:]]).astype(I32)
    return tile, e, valid, bounds


def _dispatch_kernel(pos_ref, h_ref, xs_ref, sem, *, tm):
    i = pl.program_id(0)

    def issue(r, c):
        base = (i * tm + r) * TOP_K
        for k in range(TOP_K):
            p = pos_ref[base + k]
            pltpu.make_async_copy(h_ref.at[pl.ds(r, 1), :], xs_ref.at[pl.ds(p, 1), :], sem).start()
        return c

    lax.fori_loop(0, tm, issue, 0)
    for k in range(TOP_K):
        pltpu.make_async_copy(h_ref, xs_ref.at[pl.ds(0, tm), :], sem).wait()


def _dispatch(pos_flat, h2, tm=256):
    n, d = h2.shape
    return pl.pallas_call(
        functools.partial(_dispatch_kernel, tm=tm),
        out_shape=jax.ShapeDtypeStruct((n * TOP_K, d), h2.dtype),
        grid_spec=pltpu.PrefetchScalarGridSpec(
            num_scalar_prefetch=1,
            grid=(n // tm,),
            in_specs=[pl.BlockSpec((tm, d), lambda i, pos: (i, 0))],
            out_specs=pl.BlockSpec(memory_space=pl.ANY),
            scratch_shapes=[pltpu.SemaphoreType.DMA(())]),
        compiler_params=_params(("arbitrary",)),
        name="moe_dispatch",
    )(pos_flat, h2)


def _gmm_kernel(vt_ref, ve_ref, vv_ref, bd_ref, xs_ref, wg_ref, wu_ref, wd_ref, ys_ref,
                wg16, wu16, wd16, *, tm):
    v = pl.program_id(0)
    t = vt_ref[v]
    e = ve_ref[v]
    pv = jnp.maximum(v - 1, 0)
    first = v == 0

    @pl.when(first | (e != ve_ref[pv]))
    def _():
        wg16[...] = wg_ref[0].astype(BF16)
        wu16[...] = wu_ref[0].astype(BF16)
        wd16[...] = wd_ref[0].astype(BF16)

    @pl.when(first | (t != vt_ref[pv]))
    def _():
        ys_ref[...] = jnp.zeros(ys_ref.shape, F32)

    @pl.when(vv_ref[v] == 1)
    def _():
        x = xs_ref[...].astype(BF16)
        g = _dot(x, wg16[...])
        u = _dot(x, wu16[...])
        act = (g * jax.nn.sigmoid(g) * u).astype(BF16)
        y = _dot(act, wd16[...])
        row = t * tm + lax.broadcasted_iota(I32, (tm, 1), 0)
        mine = (row >= bd_ref[e]) & (row < bd_ref[e + 1])
        ys_ref[...] += jnp.where(mine, y, 0.0)


def _gmm(tile, expert, valid, bounds, xs, w_gate, w_up, w_down, tm=256):
    nk, d = xs.shape
    n_e, _, f = w_gate.shape
    n_visits = tile.shape[0]
    return pl.pallas_call(
        functools.partial(_gmm_kernel, tm=tm),
        out_shape=jax.ShapeDtypeStruct((nk, d), F32),
        grid_spec=pltpu.PrefetchScalarGridSpec(
            num_scalar_prefetch=4,
            grid=(n_visits,),
            in_specs=[pl.BlockSpec((tm, d), lambda v, vt, ve, vv, bd: (vt[v], 0)),
                      pl.BlockSpec((1, d, f), lambda v, vt, ve, vv, bd: (ve[v], 0, 0)),
                      pl.BlockSpec((1, d, f), lambda v, vt, ve, vv, bd: (ve[v], 0, 0)),
                      pl.BlockSpec((1, f, d), lambda v, vt, ve, vv, bd: (ve[v], 0, 0))],
            out_specs=pl.BlockSpec((tm, d), lambda v, vt, ve, vv, bd: (vt[v], 0)),
            scratch_shapes=[pltpu.VMEM((d, f), BF16), pltpu.VMEM((d, f), BF16),
                            pltpu.VMEM((f, d), BF16)]),
        compiler_params=_params(("arbitrary",)),
        name="moe_experts",
    )(tile, expert, valid, bounds, xs, w_gate, w_up, w_down)


def _combine_kernel(pos_ref, ys_ref, w_ref, h_ref, x_ref, ada_ref, wsg_ref, wsu_ref, wsd_ref,
                    lg_ref, lb_ref, o_ref, buf, sem, *, tm, nsub, alpha):
    i = pl.program_id(0)

    def issue(r, c):
        base = (i * tm + r) * TOP_K
        for k in range(TOP_K):
            p = pos_ref[base + k]
            pltpu.make_async_copy(ys_ref.at[pl.ds(p, 1), :], buf.at[k, pl.ds(r, 1), :], sem).start()
        return c

    lax.fori_loop(0, tm, issue, 0)

    h16 = h_ref[...].astype(BF16)
    g = _dot(h16, wsg_ref[...])
    u = _dot(h16, wsu_ref[...])
    ffn = _dot((g * jax.nn.sigmoid(g) * u).astype(BF16), wsd_ref[...])

    for k in range(TOP_K):
        pltpu.make_async_copy(ys_ref.at[pl.ds(0, tm), :], buf.at[k], sem).wait()
    for k in range(TOP_K):
        ffn = ffn + w_ref[:, k:k + 1] * buf[k]
    for s in range(nsub):
        rows = slice(s * ADA_BLOCK, (s + 1) * ADA_BLOCK)
        g2 = ada_ref[s, 5:6, :]
        y = alpha * x_ref[rows, :] + (1.0 + g2) * ffn[rows, :]
        o_ref[rows, :] = _layer_norm(y, lg_ref[...], lb_ref[...])


def _combine(pos_flat, ys, w_tok, h2, x1, ada_blk, wsg16, wsu16, wsd16, ln_g, ln_b, *, alpha, tm=128):
    n, d = x1.shape
    f = wsg16.shape[1]
    nsub = tm // ADA_BLOCK
    return pl.pallas_call(
        functools.partial(_combine_kernel, tm=tm, nsub=nsub, alpha=alpha),
        out_shape=jax.ShapeDtypeStruct((n, d), F32),
        grid_spec=pltpu.PrefetchScalarGridSpec(
            num_scalar_prefetch=1,
            grid=(n // tm,),
            in_specs=[pl.BlockSpec(memory_space=pl.ANY),
                      pl.BlockSpec((tm, TOP_K), lambda i, pos: (i, 0)),
                      pl.BlockSpec((tm, d), lambda i, pos: (i, 0)),
                      pl.BlockSpec((tm, d), lambda i, pos: (i, 0)),
                      pl.BlockSpec((nsub, 6, d), lambda i, pos: (i, 0, 0)),
                      pl.BlockSpec((d, f), lambda i, pos: (0, 0)),
                      pl.BlockSpec((d, f), lambda i, pos: (0, 0)),
                      pl.BlockSpec((f, d), lambda i, pos: (0, 0)),
                      pl.BlockSpec((1, d), lambda i, pos: (0, 0)),
                      pl.BlockSpec((1, d), lambda i, pos: (0, 0))],
            out_specs=pl.BlockSpec((tm, d), lambda i, pos: (i, 0)),
            scratch_shapes=[pltpu.VMEM((TOP_K, tm, d), ys.dtype), pltpu.SemaphoreType.DMA(())]),
        compiler_params=_params(("arbitrary",)),
        name="moe_combine",
    )(pos_flat, ys, w_tok, h2, x1, ada_blk, wsg16, wsu16, wsd16, ln_g, ln_b)


def _layer(x_all, c_pad, blk_batch, past_k, past_v, past_logf, st_re, st_im, lp, dims):
    bp, tp, bs, ts, d = dims
    n_p, n_s = bp * tp, bs * ts
    n = n_p + n_s
    ds = d // 2
    n_h = ds // FOX_HEAD_DIM
    n_hp = n_h // 2
    g_n, p_n = lp["ssm_lambda_re"].shape
    n_e = lp["w_router"].shape[1]
    depth_alpha = lp["alpha"]

    ada = _ada(c_pad, lp["w_ada"], lp["b_ada"])
    ada_blk = ada.reshape(ada.shape[0], 6, d)[blk_batch]

    w_in = lp["w_in"]
    w_main = jnp.concatenate([w_in[:, :4 * ds], w_in[:, 4 * ds + n_h:]], axis=1).astype(BF16)
    wf_t = w_in[:, 4 * ds:4 * ds + n_h].T.astype(BF16)
    z32, qkv16, logf_t = _inproj(x_all, ada_blk, w_main, wf_t, lp["b_f"].reshape(n_h, 1))

    lbr, lbi, bbr, bbi = _ssm_disc(lp["ssm_lambda_re"], lp["ssm_lambda_im"], lp["ssm_log_dt"],
                                   lp["ssm_b_re"], lp["ssm_b_im"])
    lam_lay = _to_state_layout(lbr[None], lbi[None])
    bd = jnp.concatenate([_block_diag_slabs(bbr.transpose(0, 2, 1)),
                          _block_diag_slabs(bbi.transpose(0, 2, 1))], axis=2).astype(BF16)
    cd = jnp.stack([_block_diag_slabs(lp["ssm_c_re"].transpose(0, 2, 1)),
                    _block_diag_slabs(lp["ssm_c_im"].transpose(0, 2, 1))], axis=1).astype(BF16)
    d_row = lp["ssm_d"].reshape(1, ds)
    x0_p = jnp.zeros((bp, 2 * g_n * p_n), F32)
    x0_s = _to_state_layout(st_re, st_im)
    g_p, xl_p = _ssm(z32, x0_p, lam_lay, bd, cd, d_row, row0=0, n_seq=bp, seq_len=tp,
                     s_blk=bp, tt=min(128, tp))
    g_s, xl_s = _ssm(z32, x0_s, lam_lay, bd, cd, d_row, row0=n_p, n_seq=bs, seq_len=ts,
                     s_blk=min(8, bs), tt=ts)
    g16 = jnp.concatenate(g_p + g_s, axis=0)
    br_ssm = _glu(g16, lp["w_glu"].astype(BF16))

    lf_p = logf_t[:, :n_p].reshape(n_h, bp, tp).transpose(1, 0, 2)
    lf_s = logf_t[:, n_p:].reshape(n_h, bs, ts).transpose(1, 0, 2)
    ck_p = _cumsum_lanes(lf_p.reshape(bp * n_h, tp)).reshape(bp, n_hp, 2, tp)
    past = past_logf.shape[1]
    cat = jnp.concatenate([past_logf.astype(F32).transpose(0, 2, 1), lf_s], axis=2)
    width = -(-(past + ts) // 256) * 256
    cat = jnp.pad(cat, ((0, 0), (0, 0), (0, width - past - ts)))
    ck_s = _cumsum_lanes(cat.reshape(bs * n_h, width)).reshape(bs, n_hp, 2, width)
    attn_p = _attn_prompt(qkv16, ck_p, n_batch=bp, seq_len=tp, n_hp=n_hp, tq=min(512, tp))
    attn_s = _attn_sample(qkv16, past_k.reshape(bs, past, ds), past_v.reshape(bs, past, ds),
                          ck_s[..., :past], ck_s[..., past:past + ts],
                          row0=n_p, n_batch=bs, seq_len=ts, n_hp=n_hp, tk=min(512, past))
    attn16 = jnp.concatenate([attn_p, attn_s], axis=0)

    x1 = _post(attn16, br_ssm, z32, x_all, ada_blk, lp["w_fox_o"].astype(BF16),
               lp["w_out"].astype(BF16), lp["ln1_g"].reshape(1, d), lp["ln1_b"].reshape(1, d),
               alpha=depth_alpha)

    gsz = n_e // N_EXPERT_GROUPS
    perm = lambda a: a.reshape(N_EXPERT_GROUPS, gsz, -1).transpose(1, 0, 2).reshape(n_e, -1)
    h2, eidx_t, w_t = _router(x1, ada_blk, perm(lp["w_router"].T), perm(lp["router_bias"].reshape(n_e, 1)))
    pos_t, counts = _plan(eidx_t, n_e)
    pos_flat = pos_t.T.reshape(n * TOP_K)
    tm_e = 256
    tile, expert, valid, bounds = _visit_tables(counts[:, 0], tm_e, n * TOP_K)
    xs = _dispatch(pos_flat, h2)
    ys = _gmm(tile, expert, valid, bounds, xs, lp["w_exp_gate"], lp["w_exp_up"], lp["w_exp_down"], tm=tm_e)
    x2 = _combine(pos_flat, ys, w_t.T, h2, x1, ada_blk, lp["w_sh_gate"].astype(BF16),
                  lp["w_sh_up"].astype(BF16), lp["w_sh_down"].astype(BF16),
                  lp["ln2_g"].reshape(1, d), lp["ln2_b"].reshape(1, d), alpha=depth_alpha)

    k_new = z32[:, 2 * ds:3 * ds]
    v_new = z32[:, 3 * ds:4 * ds]
    logf = logf_t.T
    ssm_p = _from_state_layout(xl_p, g_n, p_n)
    ssm_s = _from_state_layout(xl_s, g_n, p_n)
    return x2, k_new, v_new, logf, ssm_p, ssm_s


def kernel(x_prompt, x_sample, cache_k, cache_v, cache_logf, state_ssm_re, state_ssm_im, c_prompt, c_sample, w_ada, b_ada, w_in, b_f, ssm_lambda_re, ssm_lambda_im, ssm_log_dt, ssm_b_re, ssm_b_im, ssm_c_re, ssm_c_im, ssm_d, w_glu, w_fox_o, w_out, ln1_g, ln1_b, w_router, router_bias, w_exp_gate, w_exp_up, w_exp_down, w_sh_gate, w_sh_up, w_sh_down, ln2_g, ln2_b):
    bp, tp, d = x_prompt.shape
    bs, ts, _ = x_sample.shape
    depth = w_ada.shape[0]
    n_p, n_s = bp * tp, bs * ts
    n_h = d // 2 // FOX_HEAD_DIM
    assert tp % ADA_BLOCK == 0 and ts == ADA_BLOCK
    alpha = (2.0 * depth) ** 0.25

    x_all = jnp.concatenate([x_prompt.reshape(n_p, d), x_sample.reshape(n_s, d)], axis=0)
    c_all = jnp.concatenate([c_prompt, c_sample], axis=0)
    c_pad = jnp.pad(c_all, ((0, -(bp + bs) % 16), (0, 0)))
    blk_batch = np.concatenate([np.repeat(np.arange(bp), tp // ADA_BLOCK),
                                bp + np.repeat(np.arange(bs), ts // ADA_BLOCK)])
    dims = (bp, tp, bs, ts, d)
    outs_p, outs_s = [], []
    for l in range(depth):
        lp = dict(w_ada=w_ada[l], b_ada=b_ada[l], w_in=w_in[l], b_f=b_f[l],
                  ssm_lambda_re=ssm_lambda_re[l], ssm_lambda_im=ssm_lambda_im[l],
                  ssm_log_dt=ssm_log_dt[l], ssm_b_re=ssm_b_re[l], ssm_b_im=ssm_b_im[l],
                  ssm_c_re=ssm_c_re[l], ssm_c_im=ssm_c_im[l], ssm_d=ssm_d[l], w_glu=w_glu[l],
                  w_fox_o=w_fox_o[l], w_out=w_out[l], ln1_g=ln1_g[l], ln1_b=ln1_b[l],
                  w_router=w_router[l], router_bias=router_bias[l], w_exp_gate=w_exp_gate[l],
                  w_exp_up=w_exp_up[l], w_exp_down=w_exp_down[l], w_sh_gate=w_sh_gate[l],
                  w_sh_up=w_sh_up[l], w_sh_down=w_sh_down[l], ln2_g=ln2_g[l], ln2_b=ln2_b[l],
                  alpha=alpha)
        x_all, k_new, v_new, logf, ssm_p, ssm_s = _layer(
            x_all, c_pad, blk_batch, cache_k[l], cache_v[l], cache_logf[l],
            state_ssm_re[l].astype(F32), state_ssm_im[l].astype(F32), lp, dims)
        hd = FOX_HEAD_DIM
        outs_p.append((k_new[:n_p].reshape(bp, tp, n_h, hd), v_new[:n_p].reshape(bp, tp, n_h, hd),
                       logf[:n_p].reshape(bp, tp, n_h), ssm_p[0], ssm_p[1]))
        outs_s.append((k_new[n_p:].reshape(bs, ts, n_h, hd), v_new[n_p:].reshape(bs, ts, n_h, hd),
                       logf[n_p:].reshape(bs, ts, n_h), ssm_s[0], ssm_s[1]))
    stack = lambda outs, i: jnp.stack([o[i] for o in outs])
    return (x_all[:n_p].reshape(bp, tp, d), x_all[n_p:].reshape(bs, ts, d),
            stack(outs_p, 0), stack(outs_p, 1), stack(outs_p, 2), stack(outs_p, 3), stack(outs_p, 4),
            stack(outs_s, 0), stack(outs_s, 1), stack(outs_s, 2), stack(outs_s, 3), stack(outs_s, 4))
```

```python
import functools

import jax
import jax.numpy as jnp
import numpy as np
from jax import lax
from jax.experimental import pallas as pl
from jax.experimental.pallas import tpu as pltpu

F32 = jnp.float32
BF16 = jnp.bfloat16
I32 = jnp.int32

V7X_VMEM_BYTES = 64 * 1024 * 1024
VMEM_LIMIT_BYTES = V7X_VMEM_BYTES - 8 * 1024 * 1024
LANES = 128
SUBLANES = 8

SSM_GROUP_WIDTH = 16
SSM_STATE = 64
SSM_GROUPS_PER_SLAB = 8
FOX_HEAD_DIM = 64
N_EXPERT_GROUPS = 8
TOPK_EXPERT_GROUPS = 4
TOP_K = 8
ROUTED_SCALE = 2.5
LN_EPS = 1e-5
ADA_BLOCK = 64

NT_DIMS = (((1,), (1,)), ((), ()))


def _params(sem, vmem=VMEM_LIMIT_BYTES):
    return pltpu.CompilerParams(dimension_semantics=sem, vmem_limit_bytes=vmem)


def _dot(a, b):
    return jnp.dot(a, b, preferred_element_type=F32)


def _dot_nt(a, b):
    return lax.dot_general(a, b, NT_DIMS, preferred_element_type=F32)


def _split_bf16(x):
    hi = x.astype(BF16)
    lo = (x - hi.astype(F32)).astype(BF16)
    return hi, lo


def _log_sigmoid(x):
    return jnp.minimum(x, 0.0) - jnp.log1p(jnp.exp(-jnp.abs(x)))


def _gelu_tanh(x):
    c = np.float32(np.sqrt(2.0 / np.pi))
    return x * (0.5 * (1.0 + jnp.tanh(c * (x + 0.044715 * (x * x * x)))))


def _layer_norm(y, g, b):
    mu = jnp.mean(y, axis=-1, keepdims=True)
    yc = y - mu
    var = jnp.mean(yc * yc, axis=-1, keepdims=True)
    return yc * lax.rsqrt(var + LN_EPS) * g + b


def _ada_kernel(c_ref, w_ref, b_ref, o_ref):
    c = c_ref[...]
    a_hi, a_lo = _split_bf16(c * jax.nn.sigmoid(c))
    w_hi, w_lo = _split_bf16(w_ref[...])
    acc = _dot(a_hi, w_lo) + _dot(a_lo, w_hi)
    o_ref[...] = acc + _dot(a_hi, w_hi) + b_ref[...]


def _ada(c_pad, w_ada, b_ada):
    m, d = c_pad.shape
    n = w_ada.shape[1]
    tn = 1024
    return pl.pallas_call(
        _ada_kernel,
        out_shape=jax.ShapeDtypeStruct((m, n), F32),
        grid=(n // tn,),
        in_specs=[pl.BlockSpec((m, d), lambda j: (0, 0)),
                  pl.BlockSpec((d, tn), lambda j: (0, j)),
                  pl.BlockSpec((1, tn), lambda j: (0, j))],
        out_specs=pl.BlockSpec((m, tn), lambda j: (0, j)),
        compiler_params=_params(("arbitrary",)),
        name="ada",
    )(c_pad, w_ada, b_ada.reshape(1, n))


def _inproj_kernel(x_ref, ada_ref, w_ref, wf_ref, bf_ref, z_ref, qkv_ref, lf_ref, h_scr,
                   *, nsub, q_scale):
    j = pl.program_id(1)

    @pl.when(j == 0)
    def _():
        for s in range(nsub):
            rows = slice(s * ADA_BLOCK, (s + 1) * ADA_BLOCK)
            sh = ada_ref[s, 0:1, :]
            sc = ada_ref[s, 1:2, :]
            h_scr[rows, :] = (x_ref[rows, :] * (1.0 + sc) + sh).astype(BF16)
        f = _dot_nt(wf_ref[...], h_scr[...])
        lf_ref[...] = _log_sigmoid(f + bf_ref[...])

    zt = _dot(h_scr[...], w_ref[...])
    z_ref[...] = zt

    @pl.when(j == 1)
    def _():
        qkv_ref[...] = (zt * q_scale).astype(BF16)

    @pl.when((j == 2) | (j == 3))
    def _():
        qkv_ref[...] = zt.astype(BF16)


def _inproj(x_all, ada_blk, w_main, wf_t, bf_col, tm=512):
    n, d = x_all.shape
    ds = d // 2
    h = wf_t.shape[0]
    nsub = tm // ADA_BLOCK
    ncol = w_main.shape[1] // ds
    kern = functools.partial(_inproj_kernel, nsub=nsub, q_scale=FOX_HEAD_DIM ** -0.5)
    return pl.pallas_call(
        kern,
        out_shape=(jax.ShapeDtypeStruct((n, w_main.shape[1]), F32),
                   jax.ShapeDtypeStruct((n, 3 * ds), BF16),
                   jax.ShapeDtypeStruct((h, n), F32)),
        grid=(n // tm, ncol),
        in_specs=[pl.BlockSpec((tm, d), lambda i, j: (i, 0)),
                  pl.BlockSpec((nsub, 6, d), lambda i, j: (i, 0, 0)),
                  pl.BlockSpec((d, ds), lambda i, j: (0, j)),
                  pl.BlockSpec((h, d), lambda i, j: (0, 0)),
                  pl.BlockSpec((h, 1), lambda i, j: (0, 0))],
        out_specs=(pl.BlockSpec((tm, ds), lambda i, j: (i, j)),
                   pl.BlockSpec((tm, ds), lambda i, j: (i, jnp.clip(j - 1, 0, 2))),
                   pl.BlockSpec((h, tm), lambda i, j: (0, i))),
        scratch_shapes=[pltpu.VMEM((tm, d), BF16)],
        compiler_params=_params(("arbitrary", "arbitrary")),
        name="inproj",
    )(x_all, ada_blk, w_main, wf_t, bf_col)


def _ssm_disc_kernel(lr_ref, li_ref, ldt_ref, br_ref, bi_ref,
                     lbr_ref, lbi_ref, bbr_ref, bbi_ref):
    lr = jnp.minimum(lr_ref[...], -1e-4)
    li = li_ref[...]
    dt = jnp.exp(ldt_ref[...])
    er = jnp.exp(lr * dt)
    lbr = er * jnp.cos(li * dt)
    lbi = er * jnp.sin(li * dt)
    lbr_ref[...] = lbr
    lbi_ref[...] = lbi
    nr = lbr - 1.0
    den = lr * lr + li * li
    qr = (nr * lr + lbi * li) / den
    qi = (lbi * lr - nr * li) / den
    b_r = br_ref[...]
    b_i = bi_ref[...]
    bbr_ref[...] = qr * b_r - qi * b_i
    bbi_ref[...] = qr * b_i + qi * b_r


def _ssm_disc(lam_re, lam_im, log_dt, b_re, b_im):
    g, p, w = b_re.shape
    rep = lambda a: jnp.repeat(a, w, axis=1)
    shp = jax.ShapeDtypeStruct((g, p * w), F32)
    lbr, lbi, bbr, bbi = pl.pallas_call(
        _ssm_disc_kernel, out_shape=(shp, shp, shp, shp), name="ssm_disc",
    )(rep(lam_re), rep(lam_im), log_dt.reshape(g, 1),
      b_re.reshape(g, p * w), b_im.reshape(g, p * w))
    return (lbr[:, ::w], lbi[:, ::w], bbr.reshape(g, p, w), bbi.reshape(g, p, w))


def _to_state_layout(re, im):
    s, g, p = re.shape
    ns = g // SSM_GROUPS_PER_SLAB
    r = re.reshape(s, ns, 1, SSM_GROUPS_PER_SLAB * p)
    i = im.reshape(s, ns, 1, SSM_GROUPS_PER_SLAB * p)
    return jnp.concatenate([r, i], axis=2).reshape(s, 2 * g * p)


def _from_state_layout(x, g, p):
    s = x.shape[0]
    y = x.reshape(s, g // SSM_GROUPS_PER_SLAB, 2, SSM_GROUPS_PER_SLAB, p)
    return y[:, :, 0].reshape(s, g, p), y[:, :, 1].reshape(s, g, p)


def _block_diag_slabs(a):
    g, m, n = a.shape
    k = SSM_GROUPS_PER_SLAB
    a4 = a.reshape(g // k, k, m, n)
    eye = jnp.eye(k, dtype=bool)
    out = jnp.where(eye[None, :, None, :, None], a4[:, :, :, None, :], 0.0)
    return out.reshape(g // k, k * m, k * n)


def _ssm_kernel(*refs, n_refs, rpr, s_blk, tt, n_slab, sw):
    u_refs = refs[:n_refs]
    x0_ref, lam_ref, bd_ref, cd_ref, d_ref = refs[n_refs:n_refs + 5]
    o_refs = refs[n_refs + 5:2 * n_refs + 5]
    xl_ref = refs[2 * n_refs + 5]
    u16, bu, st = refs[2 * n_refs + 6:]
    tb = pl.program_id(1)
    uw = SSM_GROUPS_PER_SLAB * SSM_GROUP_WIDTH

    @pl.when(tb == 0)
    def _():
        st[...] = x0_ref[...]

    for s in range(n_refs):
        u16[s * rpr:(s + 1) * rpr, :] = u_refs[s][...].astype(BF16)
    npl = sw // LANES
    nph = npl // 2
    for j in range(n_slab):
        res = _dot(u16[:, j * uw:(j + 1) * uw], bd_ref[j])
        for q in range(npl):
            bu[j * npl + q] = res[:, q * LANES:(q + 1) * LANES]

    for j in range(n_slab):
        c0 = j * sw
        ar = [jnp.broadcast_to(lam_ref[:, c0 + q * LANES:c0 + (q + 1) * LANES], (s_blk, LANES))
              for q in range(npl)]

        def body(t, carry, j=j, ar=ar):
            rows = pl.ds(t, s_blk, stride=tt)
            new = [None] * npl
            for q in range(nph):
                xr, xi = carry[q], carry[nph + q]
                a_r, a_i = ar[q], ar[nph + q]
                nr = a_r * xr - a_i * xi + bu[j * npl + q, rows, :]
                ni = a_r * xi + a_i * xr + bu[j * npl + nph + q, rows, :]
                bu[j * npl + q, rows, :] = nr
                bu[j * npl + nph + q, rows, :] = ni
                new[q], new[nph + q] = nr, ni
            return tuple(new)

        init = tuple(st[:, c0 + q * LANES:c0 + (q + 1) * LANES] for q in range(npl))
        fin = lax.fori_loop(0, tt, body, init, unroll=8)
        for q in range(npl):
            st[:, c0 + q * LANES:c0 + (q + 1) * LANES] = fin[q]

    for j in range(n_slab):
        xr16 = jnp.concatenate([bu[j * npl + q] for q in range(nph)], axis=1).astype(BF16)
        xi16 = jnp.concatenate([bu[j * npl + nph + q] for q in range(nph)], axis=1).astype(BF16)
        y = _dot(xr16, cd_ref[j, 0]) - _dot(xi16, cd_ref[j, 1])
        cols = slice(j * uw, (j + 1) * uw)
        for s in range(n_refs):
            ys = y[s * rpr:(s + 1) * rpr, :] + d_ref[:, cols] * u_refs[s][:, cols]
            o_refs[s][:, cols] = _gelu_tanh(ys).astype(BF16)

    @pl.when(tb == pl.num_programs(1) - 1)
    def _():
        xl_ref[...] = st[...]


def _ssm(z32, x0_lay, lam_lay, bd, cd, d_row, *, row0, n_seq, seq_len, s_blk, tt):
    n_slab = bd.shape[0]
    uw = bd.shape[1]
    sw = bd.shape[2]
    ds = n_slab * uw
    state_w = n_slab * sw
    n_tb = seq_len // tt
    n_sg = n_seq // s_blk
    if n_tb == 1:
        n_refs, rpr = 1, s_blk * tt
        assert row0 % rpr == 0
        in_maps = [lambda sg, tb: (row0 // rpr + sg, 0)]
        out_shape = [jax.ShapeDtypeStruct((n_seq * seq_len, ds), BF16)]
        out_maps = [lambda sg, tb: (sg, 0)]
    else:
        assert n_sg == 1 and row0 == 0
        n_refs, rpr = s_blk, tt
        in_maps = [functools.partial(lambda sg, tb, s: (s * n_tb + tb, 0), s=s) for s in range(s_blk)]
        out_shape = [jax.ShapeDtypeStruct((seq_len, ds), BF16)] * s_blk
        out_maps = [lambda sg, tb: (tb, 0)] * s_blk
    rows = n_refs * rpr
    kern = functools.partial(_ssm_kernel, n_refs=n_refs, rpr=rpr, s_blk=s_blk, tt=tt,
                             n_slab=n_slab, sw=sw)
    outs = pl.pallas_call(
        kern,
        out_shape=tuple(out_shape) + (jax.ShapeDtypeStruct((n_seq, state_w), F32),),
        grid=(n_sg, n_tb),
        in_specs=[pl.BlockSpec((rpr, ds), m) for m in in_maps] + [
            pl.BlockSpec((s_blk, state_w), lambda sg, tb: (sg, 0)),
            pl.BlockSpec((1, state_w), lambda sg, tb: (0, 0)),
            pl.BlockSpec(bd.shape, lambda sg, tb: (0, 0, 0)),
            pl.BlockSpec(cd.shape, lambda sg, tb: (0, 0, 0, 0)),
            pl.BlockSpec((1, ds), lambda sg, tb: (0, 0))],
        out_specs=tuple(pl.BlockSpec((rpr, ds), m) for m in out_maps) + (
            pl.BlockSpec((s_blk, state_w), lambda sg, tb: (sg, 0)),),
        scratch_shapes=[pltpu.VMEM((rows, ds), BF16),
                        pltpu.VMEM((state_w // LANES, rows, LANES), F32),
                        pltpu.VMEM((s_blk, state_w), F32)],
        compiler_params=_params(("arbitrary", "arbitrary")),
        name="ssm",
    )(*([z32] * n_refs), x0_lay, lam_lay, bd, cd, d_row)
    return list(outs[:-1]), outs[-1]


def _cumsum_kernel(x_ref, o_ref, *, blk):
    r, t = x_ref.shape
    row = lax.broadcasted_iota(I32, (blk, blk), 0)
    col = lax.broadcasted_iota(I32, (blk, blk), 1)
    upper = jnp.where(row <= col, 1.0, 0.0).astype(BF16)
    carry = jnp.zeros((r, 1), F32)
    for c in range(t // blk):
        x = x_ref[:, c * blk:(c + 1) * blk]
        h1 = x.astype(BF16)
        r1 = x - h1.astype(F32)
        h2 = r1.astype(BF16)
        h3 = (r1 - h2.astype(F32)).astype(BF16)
        s = (_dot(h3, upper) + _dot(h2, upper)) + _dot(h1, upper) + carry
        o_ref[:, c * blk:(c + 1) * blk] = s
        carry = s[:, blk - 1:blk]


def _cumsum_lanes(x, blk=256, tr=64):
    r, t = x.shape
    tr = min(tr, r)
    return pl.pallas_call(
        functools.partial(_cumsum_kernel, blk=blk),
        out_shape=jax.ShapeDtypeStruct((r, t), F32),
        grid=(r // tr,),
        in_specs=[pl.BlockSpec((tr, t), lambda i: (i, 0))],
        out_specs=pl.BlockSpec((tr, t), lambda i: (i, 0)),
        compiler_params=_params(("arbitrary",)),
        name="cumsum",
    )(x)


def _attn_step(q, k, v, ck, m_scr, l_scr, acc_scr, mask):
    lane = lax.broadcasted_iota(I32, (1, 2 * FOX_HEAD_DIM), 1)
    lo = lane < FOX_HEAD_DIM
    pvs, alphas = [], []
    for h in range(2):
        sel = lo if h == 0 else jnp.logical_not(lo)
        qh = jnp.where(sel, q, jnp.zeros_like(q))
        s = _dot_nt(qh, k) - ck[h:h + 1, :]
        if mask is not None:
            s = jnp.where(mask, s, -jnp.inf)
        m_prev = m_scr[h]
        m_new = jnp.maximum(m_prev, jnp.max(s, axis=-1, keepdims=True))
        alpha = jnp.exp(m_prev - m_new)
        p = jnp.exp(s - m_new[:, :1])
        l_scr[h] = alpha * l_scr[h] + jnp.sum(p, axis=-1, keepdims=True)
        m_scr[h] = m_new
        pvs.append(_dot(p.astype(BF16), v))
        alphas.append(alpha)
    acc_scr[...] = (jnp.where(lo, alphas[0], alphas[1]) * acc_scr[...]
                    + jnp.where(lo, pvs[0], pvs[1]))


def _attn_init(m_scr, l_scr, acc_scr):
    m_scr[...] = jnp.full(m_scr.shape, -jnp.inf, F32)
    l_scr[...] = jnp.zeros(l_scr.shape, F32)
    acc_scr[...] = jnp.zeros(acc_scr.shape, F32)


def _attn_finish(o_ref, l_scr, acc_scr):
    lane = lax.broadcasted_iota(I32, (1, 2 * FOX_HEAD_DIM), 1)
    l = jnp.where(lane < FOX_HEAD_DIM, l_scr[0], l_scr[1])
    o_ref[...] = (acc_scr[...] / l).astype(o_ref.dtype)


def _causal_mask(tq, tk):
    return (lax.broadcasted_iota(I32, (tq, tk), 1) <= lax.broadcasted_iota(I32, (tq, tk), 0))


def _attn_prompt_kernel(q_ref, k_ref, v_ref, ck_ref, o_ref, m_scr, l_scr, acc_scr, *, tq):
    qi, ki = pl.program_id(2), pl.program_id(3)

    @pl.when(ki == 0)
    def _():
        _attn_init(m_scr, l_scr, acc_scr)

    @pl.when(ki < qi)
    def _():
        _attn_step(q_ref[...], k_ref[...], v_ref[...], ck_ref[0, 0], m_scr, l_scr, acc_scr, None)

    @pl.when(ki == qi)
    def _():
        _attn_step(q_ref[...], k_ref[...], v_ref[...], ck_ref[0, 0], m_scr, l_scr, acc_scr,
                   _causal_mask(tq, tq))
        _attn_finish(o_ref, l_scr, acc_scr)


def _attn_prompt(qkv16, ck, *, n_batch, seq_len, n_hp, tq=512):
    nq = seq_len // tq
    lw = 2 * FOX_HEAD_DIM
    kv_map = lambda c: (lambda b, hp, qi, ki: (b * nq + jnp.minimum(ki, qi), c * n_hp + hp))
    return pl.pallas_call(
        functools.partial(_attn_prompt_kernel, tq=tq),
        out_shape=jax.ShapeDtypeStruct((n_batch * seq_len, n_hp * lw), BF16),
        grid=(n_batch, n_hp, nq, nq),
        in_specs=[pl.BlockSpec((tq, lw), lambda b, hp, qi, ki: (b * nq + qi, hp)),
                  pl.BlockSpec((tq, lw), kv_map(1)),
                  pl.BlockSpec((tq, lw), kv_map(2)),
                  pl.BlockSpec((1, 1, 2, tq), lambda b, hp, qi, ki: (b, hp, 0, jnp.minimum(ki, qi)))],
        out_specs=pl.BlockSpec((tq, lw), lambda b, hp, qi, ki: (b * nq + qi, hp)),
        scratch_shapes=[pltpu.VMEM((2, tq, lw), F32), pltpu.VMEM((2, tq, lw), F32),
                        pltpu.VMEM((tq, lw), F32)],
        compiler_params=_params(("arbitrary",) * 4),
        name="attn_prompt",
    )(qkv16, qkv16, qkv16, ck)


def _attn_sample_kernel(q_ref, kp_ref, vp_ref, kn_ref, vn_ref, ckp_ref, ckn_ref, o_ref, *, ts, nh):
    hd = FOX_HEAD_DIM
    lw = nh * hd
    lane = lax.broadcasted_iota(I32, (1, lw), 1)
    sels = [(lane >= h * hd) & (lane < (h + 1) * hd) for h in range(nh)]
    q = q_ref[...]
    qbd = jnp.concatenate([jnp.where(sels[h], q, jnp.zeros_like(q)) for h in range(nh)], axis=0)

    def update(state, k, v, ck, mask):
        m_prev, l_prev, acc = state
        s = _dot_nt(qbd, k)
        rows = []
        for h in range(nh):
            sh = s[h * ts:(h + 1) * ts, :] - ck[h:h + 1, :]
            rows.append(sh if mask is None else jnp.where(mask, sh, -jnp.inf))
        s = jnp.concatenate(rows, axis=0)
        m_new = jnp.maximum(m_prev, jnp.max(s, axis=-1, keepdims=True))
        alpha = jnp.exp(m_prev - m_new)
        p = jnp.exp(s - m_new)
        l_new = alpha * l_prev + jnp.sum(p, axis=-1, keepdims=True)
        return m_new, l_new, alpha * acc + _dot(p.astype(BF16), v)

    state = (jnp.full((nh * ts, 1), -jnp.inf, F32), jnp.zeros((nh * ts, 1), F32),
             jnp.zeros((nh * ts, lw), F32))
    state = update(state, kp_ref[0], vp_ref[0], ckp_ref[0, 0], None)
    _, l, acc = update(state, kn_ref[...], vn_ref[...], ckn_ref[0, 0], _causal_mask(ts, ts))
    res = acc / l
    out = res[0:ts, :]
    for h in range(1, nh):
        out = jnp.where(sels[h], res[h * ts:(h + 1) * ts, :], out)
    o_ref[...] = out.astype(o_ref.dtype)


def _attn_sample(qkv16, k_past, v_past, ck_past, ck_new, *, row0, n_batch, seq_len, n_h, nh=4):
    past = k_past.shape[1]
    lw = nh * FOX_HEAD_DIM
    ng = n_h // nh
    rb0 = row0 // seq_len
    new_map = lambda c: (lambda b, g: (rb0 + b, c * ng + g))
    return pl.pallas_call(
        functools.partial(_attn_sample_kernel, ts=seq_len, nh=nh),
        out_shape=jax.ShapeDtypeStruct((n_batch * seq_len, ng * lw), BF16),
        grid=(n_batch, ng),
        in_specs=[pl.BlockSpec((seq_len, lw), new_map(0)),
                  pl.BlockSpec((1, past, lw), lambda b, g: (b, 0, g)),
                  pl.BlockSpec((1, past, lw), lambda b, g: (b, 0, g)),
                  pl.BlockSpec((seq_len, lw), new_map(1)),
                  pl.BlockSpec((seq_len, lw), new_map(2)),
                  pl.BlockSpec((1, 1, nh, past), lambda b, g: (b, g, 0, 0)),
                  pl.BlockSpec((1, 1, nh, seq_len), lambda b, g: (b, g, 0, 0))],
        out_specs=pl.BlockSpec((seq_len, lw), lambda b, g: (b, g)),
        compiler_params=_params(("arbitrary",) * 2),
        name="attn_sample",
    )(qkv16, k_past, v_past, qkv16, qkv16, ck_past, ck_new)


def _glu_kernel(g_ref, wa_ref, wb_ref, o_ref):
    g = g_ref[...]
    o_ref[...] = _dot(g, wa_ref[...]) * jax.nn.sigmoid(_dot(g, wb_ref[...]))


def _glu(g16, w_glu16, tm=512, tn=1024):
    n, ds = g16.shape
    d = w_glu16.shape[1] // 2
    tn = min(tn, d)
    nb = d // tn
    return pl.pallas_call(
        _glu_kernel,
        out_shape=jax.ShapeDtypeStruct((n, d), F32),
        grid=(n // tm, nb),
        in_specs=[pl.BlockSpec((tm, ds), lambda i, j: (i, 0)),
                  pl.BlockSpec((ds, tn), lambda i, j: (0, j)),
                  pl.BlockSpec((ds, tn), lambda i, j: (0, nb + j))],
        out_specs=pl.BlockSpec((tm, tn), lambda i, j: (i, j)),
        compiler_params=_params(("arbitrary", "arbitrary")),
        name="glu",
    )(g16, w_glu16, w_glu16)


def _post_kernel(att_ref, brs_ref, gs_ref, gf_ref, x_ref, ada_ref, wfo_ref, wo_ref, lg_ref, lb_ref,
                 o_ref, *, nsub, alpha):
    br_fox = _dot(att_ref[...], wfo_ref[...])
    merged = jax.nn.sigmoid(gs_ref[...]) * brs_ref[...] + jax.nn.sigmoid(gf_ref[...]) * br_fox
    mix = _dot(merged.astype(BF16), wo_ref[...])
    for s in range(nsub):
        rows = slice(s * ADA_BLOCK, (s + 1) * ADA_BLOCK)
        g1 = ada_ref[s, 2:3, :]
        y = alpha * x_ref[rows, :] + (1.0 + g1) * mix[rows, :]
        o_ref[rows, :] = _layer_norm(y, lg_ref[...], lb_ref[...])


def _post(attn16, br_ssm, z32, x_all, ada_blk, w_fox16, w_out16, ln_g, ln_b, *, alpha, tm=256):
    n, d = x_all.shape
    ds = d // 2
    nsub = tm // ADA_BLOCK
    return pl.pallas_call(
        functools.partial(_post_kernel, nsub=nsub, alpha=alpha),
        out_shape=jax.ShapeDtypeStruct((n, d), F32),
        grid=(n // tm,),
        in_specs=[pl.BlockSpec((tm, ds), lambda i: (i, 0)),
                  pl.BlockSpec((tm, d), lambda i: (i, 0)),
                  pl.BlockSpec((tm, d), lambda i: (i, 2)),
                  pl.BlockSpec((tm, d), lambda i: (i, 3)),
                  pl.BlockSpec((tm, d), lambda i: (i, 0)),
                  pl.BlockSpec((nsub, 6, d), lambda i: (i, 0, 0)),
                  pl.BlockSpec((ds, d), lambda i: (0, 0)),
                  pl.BlockSpec((d, d), lambda i: (0, 0)),
                  pl.BlockSpec((1, d), lambda i: (0, 0)),
                  pl.BlockSpec((1, d), lambda i: (0, 0))],
        out_specs=pl.BlockSpec((tm, d), lambda i: (i, 0)),
        compiler_params=_params(("arbitrary",)),
        name="post_mix",
    )(attn16, br_ssm, z32, z32, x_all, ada_blk, w_fox16, w_out16, ln_g, ln_b)


def _router_kernel(x_ref, ada_ref, wr_ref, rb_ref, h_ref, e_ref, w_ref, *, nsub):
    ng = N_EXPERT_GROUPS
    for s in range(nsub):
        rows = slice(s * ADA_BLOCK, (s + 1) * ADA_BLOCK)
        h_ref[rows, :] = x_ref[rows, :] * (1.0 + ada_ref[s, 4:5, :]) + ada_ref[s, 3:4, :]
    h_hi, h_lo = _split_bf16(h_ref[...])
    w_hi, w_lo = _split_bf16(wr_ref[...])
    logits = (_dot_nt(w_hi, h_lo) + _dot_nt(w_lo, h_hi)) + _dot_nt(w_hi, h_hi)
    scores = jax.nn.sigmoid(logits)
    sel = scores + rb_ref[...]
    gsz = sel.shape[0] // ng
    tm = sel.shape[1]
    xs = [sel[j * ng:(j + 1) * ng, :] for j in range(gsz)]
    sc = [scores[j * ng:(j + 1) * ng, :] for j in range(gsz)]
    neg = -jnp.inf

    def lmax(v):
        out = v[0]
        for a in v[1:]:
            out = jnp.maximum(out, a)
        return out

    def lmin(v):
        out = v[0]
        for a in v[1:]:
            out = jnp.minimum(out, a)
        return out

    m1 = lmax(xs)
    i1 = lmin([jnp.where(xs[j] == m1, j, gsz) for j in range(gsz)])
    m2 = lmax([jnp.where(i1 == j, neg, xs[j]) for j in range(gsz)])
    cur = m1 + m2
    giota = lax.broadcasted_iota(I32, (ng, tm), 0)
    gsel = jnp.zeros((ng, tm), F32)
    for _ in range(TOPK_EXPERT_GROUPS):
        m = jnp.max(cur, axis=0, keepdims=True)
        gi = jnp.min(jnp.where(cur == m, giota, ng), axis=0, keepdims=True)
        hit = giota == gi
        gsel = jnp.where(hit, 1.0, gsel)
        cur = jnp.where(hit, neg, cur)
    gmask = gsel > 0.0
    xs = [jnp.where(gmask, x, neg) for x in xs]
    eid = [giota * gsz + j for j in range(gsz)]
    n_e = ng * gsz
    vals = []
    for r in range(TOP_K):
        m = jnp.max(lmax(xs), axis=0, keepdims=True)
        ci = jnp.min(lmin([jnp.where(xs[j] == m, eid[j], n_e) for j in range(gsz)]),
                     axis=0, keepdims=True)
        hits = [eid[j] == ci for j in range(gsz)]
        v = sum(jnp.where(hits[j], sc[j], 0.0) for j in range(gsz))
        vals.append(jnp.sum(v, axis=0, keepdims=True))
        xs = [jnp.where(hits[j], neg, xs[j]) for j in range(gsz)]
        e_ref[r:r + 1, :] = ci
    tot = sum(vals)
    for r in range(TOP_K):
        w_ref[r:r + 1, :] = vals[r] / tot * ROUTED_SCALE


def _router(x1, ada_blk, wr_perm, rb_perm, tm=512):
    n, d = x1.shape
    e = wr_perm.shape[0]
    nsub = tm // ADA_BLOCK
    return pl.pallas_call(
        functools.partial(_router_kernel, nsub=nsub),
        out_shape=(jax.ShapeDtypeStruct((n, d), F32),
                   jax.ShapeDtypeStruct((TOP_K, n), I32),
                   jax.ShapeDtypeStruct((TOP_K, n), F32)),
        grid=(n // tm,),
        in_specs=[pl.BlockSpec((tm, d), lambda i: (i, 0)),
                  pl.BlockSpec((nsub, 6, d), lambda i: (i, 0, 0)),
                  pl.BlockSpec((e, d), lambda i: (0, 0)),
                  pl.BlockSpec((e, 1), lambda i: (0, 0))],
        out_specs=(pl.BlockSpec((tm, d), lambda i: (i, 0)),
                   pl.BlockSpec((TOP_K, tm), lambda i: (0, i)),
                   pl.BlockSpec((TOP_K, tm), lambda i: (0, i))),
        compiler_params=_params(("arbitrary",)),
        name="router",
    )(x1, ada_blk, wr_perm, rb_perm)


def _plan_kernel(e_ref, pos_ref, cnt_ref, rank_scr, *, n_e, blk):
    n = e_ref.shape[1]
    nblk = n // blk
    row = lax.broadcasted_iota(I32, (blk, blk), 0)
    col = lax.broadcasted_iota(I32, (blk, blk), 1)
    upper = jnp.where(row <= col, 1.0, 0.0).astype(BF16)
    eid = lax.broadcasted_iota(I32, (n_e, blk), 0)

    def count_body(cb, carry):
        cols = pl.ds(pl.multiple_of(cb * blk, blk), blk)
        e_blk = e_ref[:, cols]
        hit = jnp.zeros((n_e, blk), F32)
        for k in range(TOP_K):
            hit = hit + jnp.where(e_blk[k:k + 1, :] == eid, 1.0, 0.0)
        cs = _dot(hit.astype(BF16), upper) + carry
        rank_scr[:, cols] = cs - hit
        return cs[:, blk - 1:blk]

    counts = lax.fori_loop(0, nblk, count_body, jnp.zeros((n_e, 1), F32))
    cnt_ref[...] = jnp.broadcast_to(counts, cnt_ref.shape).astype(I32)

    hi = jnp.floor(counts * (1.0 / 128.0))
    lo = counts - hi * 128.0
    er = lax.broadcasted_iota(I32, (n_e, n_e), 0)
    ec = lax.broadcasted_iota(I32, (n_e, n_e), 1)
    lower = jnp.where(ec < er, 1.0, 0.0).astype(BF16)
    wide = lambda v: jnp.broadcast_to(v, (n_e, LANES)).astype(BF16)
    starts = (_dot(lower, wide(hi)) * 128.0 + _dot(lower, wide(lo)))[:, :1]

    def pos_body(cb, c):
        cols = pl.ds(pl.multiple_of(cb * blk, blk), blk)
        e_blk = e_ref[:, cols]
        val = rank_scr[:, cols] + starts
        for k in range(TOP_K):
            p = jnp.sum(jnp.where(e_blk[k:k + 1, :] == eid, val, 0.0), axis=0, keepdims=True)
            pos_ref[k:k + 1, cols] = p.astype(I32)
        return c

    lax.fori_loop(0, nblk, pos_body, 0)


def _plan(eidx_t, n_e, blk=256):
    k, n = eidx_t.shape
    return pl.pallas_call(
        functools.partial(_plan_kernel, n_e=n_e, blk=blk),
        out_shape=(jax.ShapeDtypeStruct((k, n), I32), jax.ShapeDtypeStruct((n_e, LANES), I32)),
        scratch_shapes=[pltpu.VMEM((n_e, n), F32)],
        compiler_params=pltpu.CompilerParams(vmem_limit_bytes=VMEM_LIMIT_BYTES),
        name="moe_plan",
    )(eidx_t)


def _visit_tables(counts, tm, n_rows):
    n_e = counts.shape[0]
    ends = jnp.cumsum(counts)
    starts = ends - counts
    first = starts // tm
    nvis = jnp.where(counts > 0, (ends - 1) // tm - first + 1, 0)
    vend = jnp.cumsum(nvis)
    vstart = vend - nvis
    total = vend[-1]
    n_visits = n_rows // tm + n_e - 1
    v = jnp.minimum(jnp.arange(n_visits, dtype=I32), total - 1)
    e = jnp.sum((vend[None, :] <= v[:, None]).astype(I32), axis=1)
    tile = (first[e] + (v - vstart[e])).astype(I32)
    valid = (jnp.arange(n_visits, dtype=I32) < total).astype(I32)
    bounds = jnp.concatenate([starts, ends[-1:]]).astype(I32)
    nonempty = counts > 0
    slot = (jnp.cumsum(nonempty.astype(I32)) - 1) % 2
    ar = jnp.arange(n_e, dtype=I32)
    later = jnp.where((ar[None, :] > ar[:, None]) & nonempty[None, :], ar[None, :], n_e)
    nxt = jnp.min(later, axis=1)
    nxt = jnp.where(nxt == n_e, -1, nxt).astype(I32)
    return tile, e, valid, slot[e].astype(I32), nxt[e], bounds


def _dispatch_kernel(pos_ref, h_ref, xs_ref, sem, *, tm):
    i = pl.program_id(0)

    def issue(r, c):
        base = (i * tm + r) * TOP_K
        for k in range(TOP_K):
            p = pos_ref[base + k]
            pltpu.make_async_copy(h_ref.at[pl.ds(r, 1), :], xs_ref.at[pl.ds(p, 1), :],
                                  sem).start(priority=k % 2)
        return c

    lax.fori_loop(0, tm, issue, 0)
    for k in range(TOP_K):
        pltpu.make_async_copy(h_ref, xs_ref.at[pl.ds(0, tm), :], sem).wait()


def _dispatch(pos_flat, h2, tm=256):
    n, d = h2.shape
    return pl.pallas_call(
        functools.partial(_dispatch_kernel, tm=tm),
        out_shape=jax.ShapeDtypeStruct((n * TOP_K, d), h2.dtype),
        grid_spec=pltpu.PrefetchScalarGridSpec(
            num_scalar_prefetch=1,
            grid=(n // tm,),
            in_specs=[pl.BlockSpec((tm, d), lambda i, pos: (i, 0))],
            out_specs=pl.BlockSpec(memory_space=pl.ANY),
            scratch_shapes=[pltpu.SemaphoreType.DMA(())]),
        compiler_params=_params(("arbitrary",)),
        name="moe_dispatch",
    )(pos_flat, h2)


def _gmm_kernel(vt_ref, ve_ref, vv_ref, vs_ref, nx_ref, bd_ref,
                xs_ref, wg_hbm, wu_hbm, wd_hbm, ys_ref,
                wg32, wu32, wd32, wg16, wu16, wd16, sem, *, tm):
    v = pl.program_id(0)
    t = vt_ref[v]
    e = ve_ref[v]
    slot = vs_ref[v]
    pv = jnp.maximum(v - 1, 0)
    first = v == 0
    valid = vv_ref[v] == 1
    r0 = t * tm
    lo = bd_ref[e]
    hi = bd_ref[e + 1]
    whole = (r0 >= lo) & (r0 + tm <= hi)

    def weight_copies(expert, s):
        return (pltpu.make_async_copy(wg_hbm.at[expert], wg32.at[s], sem.at[s, 0]),
                pltpu.make_async_copy(wu_hbm.at[expert], wu32.at[s], sem.at[s, 1]),
                pltpu.make_async_copy(wd_hbm.at[expert], wd32.at[s], sem.at[s, 2]))

    @pl.when(first)
    def _():
        for c in weight_copies(e, slot):
            c.start()

    @pl.when(first | (e != ve_ref[pv]))
    def _():
        for c in weight_copies(e, slot):
            c.wait()
        nxt = nx_ref[v]

        @pl.when(nxt >= 0)
        def _():
            for c in weight_copies(nxt, 1 - slot):
                c.start()

        wg16[...] = wg32[slot].astype(BF16)
        wu16[...] = wu32[slot].astype(BF16)
        wd16[...] = wd32[slot].astype(BF16)

    @pl.when(valid & jnp.logical_not(whole) & (first | (t != vt_ref[pv])))
    def _():
        ys_ref[...] = jnp.zeros(ys_ref.shape, F32)

    @pl.when(valid)
    def _():
        x = xs_ref[...].astype(BF16)
        g = _dot(x, wg16[...])
        u = _dot(x, wu16[...])
        act = (g * jax.nn.sigmoid(g) * u).astype(BF16)
        y = _dot(act, wd16[...])

        @pl.when(whole)
        def _():
            ys_ref[...] = y

        @pl.when(jnp.logical_not(whole))
        def _():
            row = r0 + lax.broadcasted_iota(I32, (tm, 1), 0)
            ys_ref[...] += jnp.where((row >= lo) & (row < hi), y, 0.0)


def _gmm(tile, expert, valid, slot, nxt, bounds, xs, w_gate, w_up, w_down, tm=256):
    nk, d = xs.shape
    n_e, _, f = w_gate.shape
    n_visits = tile.shape[0]
    row_map = lambda v, vt, ve, vv, vs, nx, bd: (vt[v], 0)
    return pl.pallas_call(
        functools.partial(_gmm_kernel, tm=tm),
        out_shape=jax.ShapeDtypeStruct((nk, d), F32),
        grid_spec=pltpu.PrefetchScalarGridSpec(
            num_scalar_prefetch=6,
            grid=(n_visits,),
            in_specs=[pl.BlockSpec((tm, d), row_map),
                      pl.BlockSpec(memory_space=pl.ANY),
                      pl.BlockSpec(memory_space=pl.ANY),
                      pl.BlockSpec(memory_space=pl.ANY)],
            out_specs=pl.BlockSpec((tm, d), row_map),
            scratch_shapes=[pltpu.VMEM((2, d, f), F32), pltpu.VMEM((2, d, f), F32),
                            pltpu.VMEM((2, f, d), F32),
                            pltpu.VMEM((d, f), BF16), pltpu.VMEM((d, f), BF16),
                            pltpu.VMEM((f, d), BF16),
                            pltpu.SemaphoreType.DMA((2, 3))]),
        compiler_params=_params(("arbitrary",)),
        name="moe_experts",
    )(tile, expert, valid, slot, nxt, bounds, xs, w_gate, w_up, w_down)


def _combine_kernel(pos_ref, ys_ref, w_ref, h_ref, x_ref, ada_ref, wsg_ref, wsu_ref, wsd_ref,
                    lg_ref, lb_ref, o_ref, buf, sem, *, tm, nsub, alpha):
    i = pl.program_id(0)

    def issue(r, c):
        base = (i * tm + r) * TOP_K
        for k in range(TOP_K):
            p = pos_ref[base + k]
            pltpu.make_async_copy(ys_ref.at[pl.ds(p, 1), :], buf.at[k, pl.ds(r, 1), :],
                                  sem).start(priority=k % 2)
        return c

    lax.fori_loop(0, tm, issue, 0)

    h16 = h_ref[...].astype(BF16)
    g = _dot(h16, wsg_ref[...])
    u = _dot(h16, wsu_ref[...])
    ffn = _dot((g * jax.nn.sigmoid(g) * u).astype(BF16), wsd_ref[...])

    for k in range(TOP_K):
        pltpu.make_async_copy(ys_ref.at[pl.ds(0, tm), :], buf.at[k], sem).wait()
    for k in range(TOP_K):
        ffn = ffn + w_ref[:, k:k + 1] * buf[k]
    for s in range(nsub):
        rows = slice(s * ADA_BLOCK, (s + 1) * ADA_BLOCK)
        g2 = ada_ref[s, 5:6, :]
        y = alpha * x_ref[rows, :] + (1.0 + g2) * ffn[rows, :]
        o_ref[rows, :] = _layer_norm(y, lg_ref[...], lb_ref[...])


def _combine(pos_flat, ys, w_tok, h2, x1, ada_blk, wsg16, wsu16, wsd16, ln_g, ln_b, *, alpha, tm=128):
    n, d = x1.shape
    f = wsg16.shape[1]
    nsub = tm // ADA_BLOCK
    return pl.pallas_call(
        functools.partial(_combine_kernel, tm=tm, nsub=nsub, alpha=alpha),
        out_shape=jax.ShapeDtypeStruct((n, d), F32),
        grid_spec=pltpu.PrefetchScalarGridSpec(
            num_scalar_prefetch=1,
            grid=(n // tm,),
            in_specs=[pl.BlockSpec(memory_space=pl.ANY),
                      pl.BlockSpec((tm, TOP_K), lambda i, pos: (i, 0)),
                      pl.BlockSpec((tm, d), lambda i, pos: (i, 0)),
                      pl.BlockSpec((tm, d), lambda i, pos: (i, 0)),
                      pl.BlockSpec((nsub, 6, d), lambda i, pos: (i, 0, 0)),
                      pl.BlockSpec((d, f), lambda i, pos: (0, 0)),
                      pl.BlockSpec((d, f), lambda i, pos: (0, 0)),
                      pl.BlockSpec((f, d), lambda i, pos: (0, 0)),
                      pl.BlockSpec((1, d), lambda i, pos: (0, 0)),
                      pl.BlockSpec((1, d), lambda i, pos: (0, 0))],
            out_specs=pl.BlockSpec((tm, d), lambda i, pos: (i, 0)),
            scratch_shapes=[pltpu.VMEM((TOP_K, tm, d), ys.dtype), pltpu.SemaphoreType.DMA(())]),
        compiler_params=_params(("arbitrary",)),
        name="moe_combine",
    )(pos_flat, ys, w_tok, h2, x1, ada_blk, wsg16, wsu16, wsd16, ln_g, ln_b)


def _layer(x_all, c_pad, blk_batch, past_k, past_v, past_logf, st_re, st_im, lp, dims):
    bp, tp, bs, ts, d = dims
    n_p, n_s = bp * tp, bs * ts
    n = n_p + n_s
    ds = d // 2
    n_h = ds // FOX_HEAD_DIM
    n_hp = n_h // 2
    g_n, p_n = lp["ssm_lambda_re"].shape
    n_e = lp["w_router"].shape[1]
    depth_alpha = lp["alpha"]

    ada = _ada(c_pad, lp["w_ada"], lp["b_ada"])
    ada_blk = ada.reshape(ada.shape[0], 6, d)[blk_batch]

    w_in = lp["w_in"]
    w_main = jnp.concatenate([w_in[:, :4 * ds], w_in[:, 4 * ds + n_h:]], axis=1).astype(BF16)
    wf_t = w_in[:, 4 * ds:4 * ds + n_h].T.astype(BF16)
    z32, qkv16, logf_t = _inproj(x_all, ada_blk, w_main, wf_t, lp["b_f"].reshape(n_h, 1))

    lbr, lbi, bbr, bbi = _ssm_disc(lp["ssm_lambda_re"], lp["ssm_lambda_im"], lp["ssm_log_dt"],
                                   lp["ssm_b_re"], lp["ssm_b_im"])
    lam_lay = _to_state_layout(lbr[None], lbi[None])
    bd = jnp.concatenate([_block_diag_slabs(bbr.transpose(0, 2, 1)),
                          _block_diag_slabs(bbi.transpose(0, 2, 1))], axis=2).astype(BF16)
    cd = jnp.stack([_block_diag_slabs(lp["ssm_c_re"].transpose(0, 2, 1)),
                    _block_diag_slabs(lp["ssm_c_im"].transpose(0, 2, 1))], axis=1).astype(BF16)
    d_row = lp["ssm_d"].reshape(1, ds)
    x0_p = jnp.zeros((bp, 2 * g_n * p_n), F32)
    x0_s = _to_state_layout(st_re, st_im)
    g_p, xl_p = _ssm(z32, x0_p, lam_lay, bd, cd, d_row, row0=0, n_seq=bp, seq_len=tp,
                     s_blk=bp, tt=min(128, tp))
    g_s, xl_s = _ssm(z32, x0_s, lam_lay, bd, cd, d_row, row0=n_p, n_seq=bs, seq_len=ts,
                     s_blk=min(8, bs), tt=ts)
    g16 = jnp.concatenate(g_p + g_s, axis=0)
    br_ssm = _glu(g16, lp["w_glu"].astype(BF16))

    lf_p = logf_t[:, :n_p].reshape(n_h, bp, tp).transpose(1, 0, 2)
    lf_s = logf_t[:, n_p:].reshape(n_h, bs, ts).transpose(1, 0, 2)
    ck_p = _cumsum_lanes(lf_p.reshape(bp * n_h, tp)).reshape(bp, n_hp, 2, tp)
    past = past_logf.shape[1]
    cat = jnp.concatenate([past_logf.astype(F32).transpose(0, 2, 1), lf_s], axis=2)
    width = -(-(past + ts) // 256) * 256
    cat = jnp.pad(cat, ((0, 0), (0, 0), (0, width - past - ts)))
    nh_s = 4
    ck_s = _cumsum_lanes(cat.reshape(bs * n_h, width)).reshape(bs, n_h // nh_s, nh_s, width)
    attn_p = _attn_prompt(qkv16, ck_p, n_batch=bp, seq_len=tp, n_hp=n_hp, tq=min(512, tp))
    attn_s = _attn_sample(qkv16, past_k.reshape(bs, past, ds).astype(BF16),
                          past_v.reshape(bs, past, ds).astype(BF16),
                          ck_s[..., :past], ck_s[..., past:past + ts],
                          row0=n_p, n_batch=bs, seq_len=ts, n_h=n_h, nh=nh_s)
    attn16 = jnp.concatenate([attn_p, attn_s], axis=0)

    x1 = _post(attn16, br_ssm, z32, x_all, ada_blk, lp["w_fox_o"].astype(BF16),
               lp["w_out"].astype(BF16), lp["ln1_g"].reshape(1, d), lp["ln1_b"].reshape(1, d),
               alpha=depth_alpha)

    gsz = n_e // N_EXPERT_GROUPS
    perm = lambda a: a.reshape(N_EXPERT_GROUPS, gsz, -1).transpose(1, 0, 2).reshape(n_e, -1)
    h2, eidx_t, w_t = _router(x1, ada_blk, perm(lp["w_router"].T), perm(lp["router_bias"].reshape(n_e, 1)))
    pos_t, counts = _plan(eidx_t, n_e)
    pos_flat = pos_t.T.reshape(n * TOP_K)
    tm_e = 256
    tile, expert, valid, slot, nxt, bounds = _visit_tables(counts[:, 0], tm_e, n * TOP_K)
    xs = _dispatch(pos_flat, h2)
    ys = _gmm(tile, expert, valid, slot, nxt, bounds, xs, lp["w_exp_gate"], lp["w_exp_up"],
              lp["w_exp_down"], tm=tm_e)
    x2 = _combine(pos_flat, ys, w_t.T, h2, x1, ada_blk, lp["w_sh_gate"].astype(BF16),
                  lp["w_sh_up"].astype(BF16), lp["w_sh_down"].astype(BF16),
                  lp["ln2_g"].reshape(1, d), lp["ln2_b"].reshape(1, d), alpha=depth_alpha)

    k_new = z32[:, 2 * ds:3 * ds]
    v_new = z32[:, 3 * ds:4 * ds]
    logf = logf_t.T
    ssm_p = _from_state_layout(xl_p, g_n, p_n)
    ssm_s = _from_state_layout(xl_s, g_n, p_n)
    return x2, k_new, v_new, logf, ssm_p, ssm_s


def kernel(x_prompt, x_sample, cache_k, cache_v, cache_logf, state_ssm_re, state_ssm_im, c_prompt, c_sample, w_ada, b_ada, w_in, b_f, ssm_lambda_re, ssm_lambda_im, ssm_log_dt, ssm_b_re, ssm_b_im, ssm_c_re, ssm_c_im, ssm_d, w_glu, w_fox_o, w_out, ln1_g, ln1_b, w_router, router_bias, w_exp_gate, w_exp_up, w_exp_down, w_sh_gate, w_sh_up, w_sh_down, ln2_g, ln2_b):
    bp, tp, d = x_prompt.shape
    bs, ts, _ = x_sample.shape
    depth = w_ada.shape[0]
    n_p, n_s = bp * tp, bs * ts
    n_h = d // 2 // FOX_HEAD_DIM
    assert tp % ADA_BLOCK == 0 and ts == ADA_BLOCK
    alpha = (2.0 * depth) ** 0.25

    x_all = jnp.concatenate([x_prompt.reshape(n_p, d), x_sample.reshape(n_s, d)], axis=0)
    c_all = jnp.concatenate([c_prompt, c_sample], axis=0)
    c_pad = jnp.pad(c_all, ((0, -(bp + bs) % 16), (0, 0)))
    blk_batch = np.concatenate([np.repeat(np.arange(bp), tp // ADA_BLOCK),
                                bp + np.repeat(np.arange(bs), ts // ADA_BLOCK)])
    dims = (bp, tp, bs, ts, d)
    outs_p, outs_s = [], []
    for l in range(depth):
        lp = dict(w_ada=w_ada[l], b_ada=b_ada[l], w_in=w_in[l], b_f=b_f[l],
                  ssm_lambda_re=ssm_lambda_re[l], ssm_lambda_im=ssm_lambda_im[l],
                  ssm_log_dt=ssm_log_dt[l], ssm_b_re=ssm_b_re[l], ssm_b_im=ssm_b_im[l],
                  ssm_c_re=ssm_c_re[l], ssm_c_im=ssm_c_im[l], ssm_d=ssm_d[l], w_glu=w_glu[l],
                  w_fox_o=w_fox_o[l], w_out=w_out[l], ln1_g=ln1_g[l], ln1_b=ln1_b[l],
                  w_router=w_router[l], router_bias=router_bias[l], w_exp_gate=w_exp_gate[l],
                  w_exp_up=w_exp_up[l], w_exp_down=w_exp_down[l], w_sh_gate=w_sh_gate[l],
                  w_sh_up=w_sh_up[l], w_sh_down=w_sh_down[l], ln2_g=ln2_g[l], ln2_b=ln2_b[l],
                  alpha=alpha)
        x_all, k_new, v_new, logf, ssm_p, ssm_s = _layer(
            x_all, c_pad, blk_batch, cache_k[l], cache_v[l], cache_logf[l],
            state_ssm_re[l].astype(F32), state_ssm_im[l].astype(F32), lp, dims)
        hd = FOX_HEAD_DIM
        outs_p.append((k_new[:n_p].reshape(bp, tp, n_h, hd), v_new[:n_p].reshape(bp, tp, n_h, hd),
                       logf[:n_p].reshape(bp, tp, n_h), ssm_p[0], ssm_p[1]))
        outs_s.append((k_new[n_p:].reshape(bs, ts, n_h, hd), v_new[n_p:].reshape(bs, ts, n_h, hd),
                       logf[n_p:].reshape(bs, ts, n_h), ssm_s[0], ssm_s[1]))
    stack = lambda outs, i: jnp.stack([o[i] for o in outs])
    return (x_all[:n_p].reshape(bp, tp, d), x_all[n_p:].reshape(bs, ts, d),
            stack(outs_p, 0), stack(outs_p, 1), stack(outs_p, 2), stack(outs_p, 3), stack(outs_p, 4),
            stack(outs_s, 0), stack(outs_s, 1), stack(outs_s, 2), stack(outs_s, 3), stack(outs_s, 4))
```

```python
import functools

import jax
import jax.numpy as jnp
import numpy as np
from jax import lax
from jax.experimental import pallas as pl
from jax.experimental.pallas import tpu as pltpu

F32 = jnp.float32
BF16 = jnp.bfloat16
I32 = jnp.int32

V7X_VMEM_BYTES = 64 * 1024 * 1024
VMEM_LIMIT_BYTES = V7X_VMEM_BYTES - 8 * 1024 * 1024
LANES = 128
SUBLANES = 8

SSM_GROUP_WIDTH = 16
SSM_STATE = 64
SSM_GROUPS_PER_SLAB = 8
FOX_HEAD_DIM = 64
N_EXPERT_GROUPS = 8
TOPK_EXPERT_GROUPS = 4
TOP_K = 8
ROUTED_SCALE = 2.5
LN_EPS = 1e-5
ADA_BLOCK = 64

NT_DIMS = (((1,), (1,)), ((), ()))


def _params(sem, vmem=VMEM_LIMIT_BYTES):
    return pltpu.CompilerParams(dimension_semantics=sem, vmem_limit_bytes=vmem)


def _dot(a, b):
    return jnp.dot(a, b, preferred_element_type=F32)


def _dot_nt(a, b):
    return lax.dot_general(a, b, NT_DIMS, preferred_element_type=F32)


def _split_bf16(x):
    hi = x.astype(BF16)
    lo = (x - hi.astype(F32)).astype(BF16)
    return hi, lo


def _log_sigmoid(x):
    return jnp.minimum(x, 0.0) - jnp.log1p(jnp.exp(-jnp.abs(x)))


def _gelu_tanh(x):
    c = np.float32(np.sqrt(2.0 / np.pi))
    return x * (0.5 * (1.0 + jnp.tanh(c * (x + 0.044715 * (x * x * x)))))


def _layer_norm(y, g, b):
    mu = jnp.mean(y, axis=-1, keepdims=True)
    yc = y - mu
    var = jnp.mean(yc * yc, axis=-1, keepdims=True)
    return yc * lax.rsqrt(var + LN_EPS) * g + b


def _ada_kernel(c_ref, w_ref, b_ref, o_ref):
    c = c_ref[...]
    a_hi, a_lo = _split_bf16(c * jax.nn.sigmoid(c))
    w_hi, w_lo = _split_bf16(w_ref[...])
    acc = _dot(a_hi, w_lo) + _dot(a_lo, w_hi)
    o_ref[...] = acc + _dot(a_hi, w_hi) + b_ref[...]


def _ada(c_pad, w_ada, b_ada):
    m, d = c_pad.shape
    n = w_ada.shape[1]
    tn = 1024
    return pl.pallas_call(
        _ada_kernel,
        out_shape=jax.ShapeDtypeStruct((m, n), F32),
        grid=(n // tn,),
        in_specs=[pl.BlockSpec((m, d), lambda j: (0, 0)),
                  pl.BlockSpec((d, tn), lambda j: (0, j)),
                  pl.BlockSpec((1, tn), lambda j: (0, j))],
        out_specs=pl.BlockSpec((m, tn), lambda j: (0, j)),
        compiler_params=_params(("arbitrary",)),
        name="ada",
    )(c_pad, w_ada, b_ada.reshape(1, n))


def _inproj_kernel(xp_ref, xs_ref, ada_ref, w_ref, wf_ref, bf_ref, z_ref, qkv_ref, lf_ref, h_scr,
                   *, nsub, q_scale, np_tiles):
    i = pl.program_id(0)
    j = pl.program_id(1)

    def modulate(x_ref):
        for s in range(nsub):
            rows = slice(s * ADA_BLOCK, (s + 1) * ADA_BLOCK)
            sh = ada_ref[s, 0:1, :]
            sc = ada_ref[s, 1:2, :]
            h_scr[rows, :] = (x_ref[rows, :] * (1.0 + sc) + sh).astype(BF16)

    @pl.when((j == 0) & (i < np_tiles))
    def _():
        modulate(xp_ref)

    @pl.when((j == 0) & (i >= np_tiles))
    def _():
        modulate(xs_ref)

    @pl.when(j == 0)
    def _():
        f = _dot_nt(wf_ref[...], h_scr[...])
        lf_ref[...] = _log_sigmoid(f + bf_ref[...])

    zt = _dot(h_scr[...], w_ref[...])
    z_ref[...] = zt

    @pl.when(j == 1)
    def _():
        qkv_ref[...] = (zt * q_scale).astype(BF16)

    @pl.when((j == 2) | (j == 3))
    def _():
        qkv_ref[...] = zt.astype(BF16)


def _two_part_specs(tm, d, np_tiles, n_grid_args):
    if n_grid_args == 1:
        return [pl.BlockSpec((tm, d), lambda i: (jnp.minimum(i, np_tiles - 1), 0)),
                pl.BlockSpec((tm, d), lambda i: (jnp.maximum(i - np_tiles, 0), 0))]
    return [pl.BlockSpec((tm, d), lambda i, j: (jnp.minimum(i, np_tiles - 1), 0)),
            pl.BlockSpec((tm, d), lambda i, j: (jnp.maximum(i - np_tiles, 0), 0))]


def _inproj(x_p, x_s, ada_blk, w_main, wf_t, bf_col, tm=512):
    n_p, d = x_p.shape
    n = n_p + x_s.shape[0]
    ds = d // 2
    h = wf_t.shape[0]
    nsub = tm // ADA_BLOCK
    ncol = w_main.shape[1] // ds
    np_tiles = n_p // tm
    kern = functools.partial(_inproj_kernel, nsub=nsub, q_scale=FOX_HEAD_DIM ** -0.5,
                             np_tiles=np_tiles)
    return pl.pallas_call(
        kern,
        out_shape=(jax.ShapeDtypeStruct((n, w_main.shape[1]), F32),
                   jax.ShapeDtypeStruct((n, 3 * ds), BF16),
                   jax.ShapeDtypeStruct((h, n), F32)),
        grid=(n // tm, ncol),
        in_specs=_two_part_specs(tm, d, np_tiles, 2) + [
                  pl.BlockSpec((nsub, 6, d), lambda i, j: (i, 0, 0)),
                  pl.BlockSpec((d, ds), lambda i, j: (0, j)),
                  pl.BlockSpec((h, d), lambda i, j: (0, 0)),
                  pl.BlockSpec((h, 1), lambda i, j: (0, 0))],
        out_specs=(pl.BlockSpec((tm, ds), lambda i, j: (i, j)),
                   pl.BlockSpec((tm, ds), lambda i, j: (i, jnp.clip(j - 1, 0, 2))),
                   pl.BlockSpec((h, tm), lambda i, j: (0, i))),
        scratch_shapes=[pltpu.VMEM((tm, d), BF16)],
        compiler_params=_params(("arbitrary", "arbitrary")),
        name="inproj",
    )(x_p, x_s, ada_blk, w_main, wf_t, bf_col)


def _ssm_disc_kernel(lr_ref, li_ref, ldt_ref, br_ref, bi_ref,
                     lbr_ref, lbi_ref, bbr_ref, bbi_ref):
    lr = jnp.minimum(lr_ref[...], -1e-4)
    li = li_ref[...]
    dt = jnp.exp(ldt_ref[...])
    er = jnp.exp(lr * dt)
    lbr = er * jnp.cos(li * dt)
    lbi = er * jnp.sin(li * dt)
    lbr_ref[...] = lbr
    lbi_ref[...] = lbi
    nr = lbr - 1.0
    den = lr * lr + li * li
    qr = (nr * lr + lbi * li) / den
    qi = (lbi * lr - nr * li) / den
    b_r = br_ref[...]
    b_i = bi_ref[...]
    bbr_ref[...] = qr * b_r - qi * b_i
    bbi_ref[...] = qr * b_i + qi * b_r


def _ssm_disc(lam_re, lam_im, log_dt, b_re, b_im):
    g, p, w = b_re.shape
    rep = lambda a: jnp.repeat(a, w, axis=1)
    shp = jax.ShapeDtypeStruct((g, p * w), F32)
    lbr, lbi, bbr, bbi = pl.pallas_call(
        _ssm_disc_kernel, out_shape=(shp, shp, shp, shp), name="ssm_disc",
    )(rep(lam_re), rep(lam_im), log_dt.reshape(g, 1),
      b_re.reshape(g, p * w), b_im.reshape(g, p * w))
    return (lbr[:, ::w], lbi[:, ::w], bbr.reshape(g, p, w), bbi.reshape(g, p, w))


def _to_state_layout(re, im):
    s, g, p = re.shape
    ns = g // SSM_GROUPS_PER_SLAB
    r = re.reshape(s, ns, 1, SSM_GROUPS_PER_SLAB * p)
    i = im.reshape(s, ns, 1, SSM_GROUPS_PER_SLAB * p)
    return jnp.concatenate([r, i], axis=2).reshape(s, 2 * g * p)


def _from_state_layout(x, g, p):
    s = x.shape[0]
    y = x.reshape(s, g // SSM_GROUPS_PER_SLAB, 2, SSM_GROUPS_PER_SLAB, p)
    return y[:, :, 0].reshape(s, g, p), y[:, :, 1].reshape(s, g, p)


def _block_diag_slabs(a):
    g, m, n = a.shape
    k = SSM_GROUPS_PER_SLAB
    a4 = a.reshape(g // k, k, m, n)
    eye = jnp.eye(k, dtype=bool)
    out = jnp.where(eye[None, :, None, :, None], a4[:, :, :, None, :], 0.0)
    return out.reshape(g // k, k * m, k * n)


def _ssm_kernel(*refs, n_refs, rpr, s_blk, tt, n_slab, sw):
    u_refs = refs[:n_refs]
    x0_ref, lam_ref, bd_ref, cd_ref, d_ref = refs[n_refs:n_refs + 5]
    o_refs = refs[n_refs + 5:2 * n_refs + 5]
    xl_ref = refs[2 * n_refs + 5]
    u16, bu, st = refs[2 * n_refs + 6:]
    tb = pl.program_id(1)
    uw = SSM_GROUPS_PER_SLAB * SSM_GROUP_WIDTH

    @pl.when(tb == 0)
    def _():
        st[...] = x0_ref[...]

    for s in range(n_refs):
        u16[s * rpr:(s + 1) * rpr, :] = u_refs[s][...].astype(BF16)
    npl = sw // LANES
    nph = npl // 2
    for j in range(n_slab):
        res = _dot(u16[:, j * uw:(j + 1) * uw], bd_ref[j])
        for q in range(npl):
            bu[j * npl + q] = res[:, q * LANES:(q + 1) * LANES]

    for j in range(n_slab):
        c0 = j * sw
        ar = [jnp.broadcast_to(lam_ref[:, c0 + q * LANES:c0 + (q + 1) * LANES], (s_blk, LANES))
              for q in range(npl)]

        def body(t, carry, j=j, ar=ar):
            rows = pl.ds(t, s_blk, stride=tt)
            new = [None] * npl
            for q in range(nph):
                xr, xi = carry[q], carry[nph + q]
                a_r, a_i = ar[q], ar[nph + q]
                nr = a_r * xr - a_i * xi + bu[j * npl + q, rows, :]
                ni = a_r * xi + a_i * xr + bu[j * npl + nph + q, rows, :]
                bu[j * npl + q, rows, :] = nr
                bu[j * npl + nph + q, rows, :] = ni
                new[q], new[nph + q] = nr, ni
            return tuple(new)

        init = tuple(st[:, c0 + q * LANES:c0 + (q + 1) * LANES] for q in range(npl))
        fin = lax.fori_loop(0, tt, body, init, unroll=8)
        for q in range(npl):
            st[:, c0 + q * LANES:c0 + (q + 1) * LANES] = fin[q]

    for j in range(n_slab):
        xr16 = jnp.concatenate([bu[j * npl + q] for q in range(nph)], axis=1).astype(BF16)
        xi16 = jnp.concatenate([bu[j * npl + nph + q] for q in range(nph)], axis=1).astype(BF16)
        y = _dot(xr16, cd_ref[j, 0]) - _dot(xi16, cd_ref[j, 1])
        cols = slice(j * uw, (j + 1) * uw)
        for s in range(n_refs):
            ys = y[s * rpr:(s + 1) * rpr, :] + d_ref[:, cols] * u_refs[s][:, cols]
            o_refs[s][:, cols] = _gelu_tanh(ys).astype(BF16)

    @pl.when(tb == pl.num_programs(1) - 1)
    def _():
        xl_ref[...] = st[...]


def _ssm(z32, x0_lay, lam_lay, bd, cd, d_row, *, row0, n_seq, seq_len, s_blk, tt):
    n_slab = bd.shape[0]
    uw = bd.shape[1]
    sw = bd.shape[2]
    ds = n_slab * uw
    state_w = n_slab * sw
    n_tb = seq_len // tt
    n_sg = n_seq // s_blk
    if n_tb == 1:
        n_refs, rpr = 1, s_blk * tt
        assert row0 % rpr == 0
        in_maps = [lambda sg, tb: (row0 // rpr + sg, 0)]
        out_shape = [jax.ShapeDtypeStruct((n_seq * seq_len, ds), BF16)]
        out_maps = [lambda sg, tb: (sg, 0)]
    else:
        assert n_sg == 1 and row0 == 0
        n_refs, rpr = s_blk, tt
        in_maps = [functools.partial(lambda sg, tb, s: (s * n_tb + tb, 0), s=s) for s in range(s_blk)]
        out_shape = [jax.ShapeDtypeStruct((seq_len, ds), BF16)] * s_blk
        out_maps = [lambda sg, tb: (tb, 0)] * s_blk
    rows = n_refs * rpr
    kern = functools.partial(_ssm_kernel, n_refs=n_refs, rpr=rpr, s_blk=s_blk, tt=tt,
                             n_slab=n_slab, sw=sw)
    outs = pl.pallas_call(
        kern,
        out_shape=tuple(out_shape) + (jax.ShapeDtypeStruct((n_seq, state_w), F32),),
        grid=(n_sg, n_tb),
        in_specs=[pl.BlockSpec((rpr, ds), m) for m in in_maps] + [
            pl.BlockSpec((s_blk, state_w), lambda sg, tb: (sg, 0)),
            pl.BlockSpec((1, state_w), lambda sg, tb: (0, 0)),
            pl.BlockSpec(bd.shape, lambda sg, tb: (0, 0, 0)),
            pl.BlockSpec(cd.shape, lambda sg, tb: (0, 0, 0, 0)),
            pl.BlockSpec((1, ds), lambda sg, tb: (0, 0))],
        out_specs=tuple(pl.BlockSpec((rpr, ds), m) for m in out_maps) + (
            pl.BlockSpec((s_blk, state_w), lambda sg, tb: (sg, 0)),),
        scratch_shapes=[pltpu.VMEM((rows, ds), BF16),
                        pltpu.VMEM((state_w // LANES, rows, LANES), F32),
                        pltpu.VMEM((s_blk, state_w), F32)],
        compiler_params=_params(("arbitrary", "arbitrary")),
        name="ssm",
    )(*([z32] * n_refs), x0_lay, lam_lay, bd, cd, d_row)
    return list(outs[:-1]), outs[-1]


def _cumsum_kernel(x_ref, o_ref, *, blk):
    r, t = x_ref.shape
    row = lax.broadcasted_iota(I32, (blk, blk), 0)
    col = lax.broadcasted_iota(I32, (blk, blk), 1)
    upper = jnp.where(row <= col, 1.0, 0.0).astype(BF16)
    carry = jnp.zeros((r, 1), F32)
    for c in range(t // blk):
        x = x_ref[:, c * blk:(c + 1) * blk]
        h1 = x.astype(BF16)
        r1 = x - h1.astype(F32)
        h2 = r1.astype(BF16)
        h3 = (r1 - h2.astype(F32)).astype(BF16)
        s = (_dot(h3, upper) + _dot(h2, upper)) + _dot(h1, upper) + carry
        o_ref[:, c * blk:(c + 1) * blk] = s
        carry = s[:, blk - 1:blk]


def _cumsum_lanes(x, blk=256, tr=64):
    r, t = x.shape
    tr = min(tr, r)
    return pl.pallas_call(
        functools.partial(_cumsum_kernel, blk=blk),
        out_shape=jax.ShapeDtypeStruct((r, t), F32),
        grid=(r // tr,),
        in_specs=[pl.BlockSpec((tr, t), lambda i: (i, 0))],
        out_specs=pl.BlockSpec((tr, t), lambda i: (i, 0)),
        compiler_params=_params(("arbitrary",)),
        name="cumsum",
    )(x)


def _attn_step(q, k, v, ck, m_scr, l_scr, acc_scr, mask):
    lane = lax.broadcasted_iota(I32, (1, 2 * FOX_HEAD_DIM), 1)
    lo = lane < FOX_HEAD_DIM
    pvs, alphas = [], []
    for h in range(2):
        sel = lo if h == 0 else jnp.logical_not(lo)
        qh = jnp.where(sel, q, jnp.zeros_like(q))
        s = _dot_nt(qh, k) - ck[h:h + 1, :]
        if mask is not None:
            s = jnp.where(mask, s, -jnp.inf)
        m_prev = m_scr[h]
        m_new = jnp.maximum(m_prev, jnp.max(s, axis=-1, keepdims=True))
        alpha = jnp.exp(m_prev - m_new)
        p = jnp.exp(s - m_new[:, :1])
        l_scr[h] = alpha * l_scr[h] + jnp.sum(p, axis=-1, keepdims=True)
        m_scr[h] = m_new
        pvs.append(_dot(p.astype(BF16), v))
        alphas.append(alpha)
    acc_scr[...] = (jnp.where(lo, alphas[0], alphas[1]) * acc_scr[...]
                    + jnp.where(lo, pvs[0], pvs[1]))


def _attn_init(m_scr, l_scr, acc_scr):
    m_scr[...] = jnp.full(m_scr.shape, -jnp.inf, F32)
    l_scr[...] = jnp.zeros(l_scr.shape, F32)
    acc_scr[...] = jnp.zeros(acc_scr.shape, F32)


def _attn_finish(o_ref, l_scr, acc_scr):
    lane = lax.broadcasted_iota(I32, (1, 2 * FOX_HEAD_DIM), 1)
    l = jnp.where(lane < FOX_HEAD_DIM, l_scr[0], l_scr[1])
    o_ref[...] = (acc_scr[...] / l).astype(o_ref.dtype)


def _causal_mask(tq, tk):
    return (lax.broadcasted_iota(I32, (tq, tk), 1) <= lax.broadcasted_iota(I32, (tq, tk), 0))


def _attn_prompt_kernel(q_ref, k_ref, v_ref, ck_ref, o_ref, m_scr, l_scr, acc_scr, *, tq):
    qi, ki = pl.program_id(2), pl.program_id(3)

    @pl.when(ki == 0)
    def _():
        _attn_init(m_scr, l_scr, acc_scr)

    @pl.when(ki < qi)
    def _():
        _attn_step(q_ref[...], k_ref[...], v_ref[...], ck_ref[0, 0], m_scr, l_scr, acc_scr, None)

    @pl.when(ki == qi)
    def _():
        _attn_step(q_ref[...], k_ref[...], v_ref[...], ck_ref[0, 0], m_scr, l_scr, acc_scr,
                   _causal_mask(tq, tq))
        _attn_finish(o_ref, l_scr, acc_scr)


def _attn_prompt(qkv16, ck, *, n_batch, seq_len, n_hp, tq=512):
    nq = seq_len // tq
    lw = 2 * FOX_HEAD_DIM
    kv_map = lambda c: (lambda b, hp, qi, ki: (b * nq + jnp.minimum(ki, qi), c * n_hp + hp))
    return pl.pallas_call(
        functools.partial(_attn_prompt_kernel, tq=tq),
        out_shape=jax.ShapeDtypeStruct((n_batch * seq_len, n_hp * lw), BF16),
        grid=(n_batch, n_hp, nq, nq),
        in_specs=[pl.BlockSpec((tq, lw), lambda b, hp, qi, ki: (b * nq + qi, hp)),
                  pl.BlockSpec((tq, lw), kv_map(1)),
                  pl.BlockSpec((tq, lw), kv_map(2)),
                  pl.BlockSpec((1, 1, 2, tq), lambda b, hp, qi, ki: (b, hp, 0, jnp.minimum(ki, qi)))],
        out_specs=pl.BlockSpec((tq, lw), lambda b, hp, qi, ki: (b * nq + qi, hp)),
        scratch_shapes=[pltpu.VMEM((2, tq, lw), F32), pltpu.VMEM((2, tq, lw), F32),
                        pltpu.VMEM((tq, lw), F32)],
        compiler_params=_params(("arbitrary",) * 4),
        name="attn_prompt",
    )(qkv16, qkv16, qkv16, ck)


def _attn_sample_kernel(q_ref, kp_ref, vp_ref, kn_ref, vn_ref, ckp_ref, ckn_ref, o_ref, *, ts, nh):
    hd = FOX_HEAD_DIM
    lw = nh * hd
    lane = lax.broadcasted_iota(I32, (1, lw), 1)
    sels = [(lane >= h * hd) & (lane < (h + 1) * hd) for h in range(nh)]
    q = q_ref[...]
    qbd = jnp.concatenate([jnp.where(sels[h], q, jnp.zeros_like(q)) for h in range(nh)], axis=0)

    def update(state, k, v, ck, mask):
        m_prev, l_prev, acc = state
        s = _dot_nt(qbd, k)
        rows = []
        for h in range(nh):
            sh = s[h * ts:(h + 1) * ts, :] - ck[h:h + 1, :]
            rows.append(sh if mask is None else jnp.where(mask, sh, -jnp.inf))
        s = jnp.concatenate(rows, axis=0)
        m_new = jnp.maximum(m_prev, jnp.max(s, axis=-1, keepdims=True))
        alpha = jnp.exp(m_prev - m_new)
        p = jnp.exp(s - m_new)
        l_new = alpha * l_prev + jnp.sum(p, axis=-1, keepdims=True)
        return m_new, l_new, alpha * acc + _dot(p.astype(BF16), v)

    state = (jnp.full((nh * ts, 1), -jnp.inf, F32), jnp.zeros((nh * ts, 1), F32),
             jnp.zeros((nh * ts, lw), F32))
    state = update(state, kp_ref[0], vp_ref[0], ckp_ref[0, 0], None)
    _, l, acc = update(state, kn_ref[...], vn_ref[...], ckn_ref[0, 0], _causal_mask(ts, ts))
    res = acc / l
    out = res[0:ts, :]
    for h in range(1, nh):
        out = jnp.where(sels[h], res[h * ts:(h + 1) * ts, :], out)
    o_ref[...] = out.astype(o_ref.dtype)


def _attn_sample(qkv16, k_past, v_past, ck_past, ck_new, *, row0, n_batch, seq_len, n_h, nh=4):
    past = k_past.shape[1]
    lw = nh * FOX_HEAD_DIM
    ng = n_h // nh
    rb0 = row0 // seq_len
    new_map = lambda c: (lambda b, g: (rb0 + b, c * ng + g))
    return pl.pallas_call(
        functools.partial(_attn_sample_kernel, ts=seq_len, nh=nh),
        out_shape=jax.ShapeDtypeStruct((n_batch * seq_len, ng * lw), BF16),
        grid=(n_batch, ng),
        in_specs=[pl.BlockSpec((seq_len, lw), new_map(0)),
                  pl.BlockSpec((1, past, lw), lambda b, g: (b, 0, g)),
                  pl.BlockSpec((1, past, lw), lambda b, g: (b, 0, g)),
                  pl.BlockSpec((seq_len, lw), new_map(1)),
                  pl.BlockSpec((seq_len, lw), new_map(2)),
                  pl.BlockSpec((1, 1, nh, past), lambda b, g: (b, g, 0, 0)),
                  pl.BlockSpec((1, 1, nh, seq_len), lambda b, g: (b, g, 0, 0))],
        out_specs=pl.BlockSpec((seq_len, lw), lambda b, g: (b, g)),
        compiler_params=_params(("arbitrary",) * 2),
        name="attn_sample",
    )(qkv16, k_past, v_past, qkv16, qkv16, ck_past, ck_new)


def _glu_kernel(g_ref, wa_ref, wb_ref, o_ref):
    g = g_ref[...]
    o_ref[...] = _dot(g, wa_ref[...]) * jax.nn.sigmoid(_dot(g, wb_ref[...]))


def _glu(g16, w_glu16, tm=512, tn=1024):
    n, ds = g16.shape
    d = w_glu16.shape[1] // 2
    tn = min(tn, d)
    nb = d // tn
    return pl.pallas_call(
        _glu_kernel,
        out_shape=jax.ShapeDtypeStruct((n, d), F32),
        grid=(n // tm, nb),
        in_specs=[pl.BlockSpec((tm, ds), lambda i, j: (i, 0)),
                  pl.BlockSpec((ds, tn), lambda i, j: (0, j)),
                  pl.BlockSpec((ds, tn), lambda i, j: (0, nb + j))],
        out_specs=pl.BlockSpec((tm, tn), lambda i, j: (i, j)),
        compiler_params=_params(("arbitrary", "arbitrary")),
        name="glu",
    )(g16, w_glu16, w_glu16)


def _post_kernel(att_ref, brs_ref, gs_ref, gf_ref, xp_ref, xs_ref, ada_ref, wfo_ref, wo_ref,
                 lg_ref, lb_ref, o_ref, *, nsub, alpha, np_tiles):
    i = pl.program_id(0)
    br_fox = _dot(att_ref[...], wfo_ref[...])
    merged = jax.nn.sigmoid(gs_ref[...]) * brs_ref[...] + jax.nn.sigmoid(gf_ref[...]) * br_fox
    mix = _dot(merged.astype(BF16), wo_ref[...])

    def finish(x_ref):
        for s in range(nsub):
            rows = slice(s * ADA_BLOCK, (s + 1) * ADA_BLOCK)
            g1 = ada_ref[s, 2:3, :]
            y = alpha * x_ref[rows, :] + (1.0 + g1) * mix[rows, :]
            o_ref[rows, :] = _layer_norm(y, lg_ref[...], lb_ref[...])

    @pl.when(i < np_tiles)
    def _():
        finish(xp_ref)

    @pl.when(i >= np_tiles)
    def _():
        finish(xs_ref)


def _post(attn16, br_ssm, z32, x_p, x_s, ada_blk, w_fox16, w_out16, ln_g, ln_b, *, alpha, tm=256):
    n_p, d = x_p.shape
    n = n_p + x_s.shape[0]
    ds = d // 2
    nsub = tm // ADA_BLOCK
    np_tiles = n_p // tm
    return pl.pallas_call(
        functools.partial(_post_kernel, nsub=nsub, alpha=alpha, np_tiles=np_tiles),
        out_shape=jax.ShapeDtypeStruct((n, d), F32),
        grid=(n // tm,),
        in_specs=[pl.BlockSpec((tm, ds), lambda i: (i, 0)),
                  pl.BlockSpec((tm, d), lambda i: (i, 0)),
                  pl.BlockSpec((tm, d), lambda i: (i, 2)),
                  pl.BlockSpec((tm, d), lambda i: (i, 3)),
                  ] + _two_part_specs(tm, d, np_tiles, 1) + [
                  pl.BlockSpec((nsub, 6, d), lambda i: (i, 0, 0)),
                  pl.BlockSpec((ds, d), lambda i: (0, 0)),
                  pl.BlockSpec((d, d), lambda i: (0, 0)),
                  pl.BlockSpec((1, d), lambda i: (0, 0)),
                  pl.BlockSpec((1, d), lambda i: (0, 0))],
        out_specs=pl.BlockSpec((tm, d), lambda i: (i, 0)),
        compiler_params=_params(("arbitrary",)),
        name="post_mix",
    )(attn16, br_ssm, z32, z32, x_p, x_s, ada_blk, w_fox16, w_out16, ln_g, ln_b)


def _pack_halves(x):
    c = x.shape[1] // 2
    return pltpu.pack_elementwise([x[:, :c], x[:, c:]], packed_dtype=BF16)


def _unpack_halves(w):
    return tuple(pltpu.unpack_elementwise(w, index=i, packed_dtype=BF16, unpacked_dtype=F32)
                 for i in range(2))


def _router_kernel(x_ref, ada_ref, wr_ref, rb_ref, hp_ref, e_ref, w_ref, h_scr, *, nsub):
    ng = N_EXPERT_GROUPS
    for s in range(nsub):
        rows = slice(s * ADA_BLOCK, (s + 1) * ADA_BLOCK)
        h_scr[rows, :] = x_ref[rows, :] * (1.0 + ada_ref[s, 4:5, :]) + ada_ref[s, 3:4, :]
    hp_ref[...] = _pack_halves(h_scr[...])
    h_hi, h_lo = _split_bf16(h_scr[...])
    w_hi, w_lo = _split_bf16(wr_ref[...])
    logits = (_dot_nt(w_hi, h_lo) + _dot_nt(w_lo, h_hi)) + _dot_nt(w_hi, h_hi)
    scores = jax.nn.sigmoid(logits)
    sel = scores + rb_ref[...]
    gsz = sel.shape[0] // ng
    tm = sel.shape[1]
    xs = [sel[j * ng:(j + 1) * ng, :] for j in range(gsz)]
    sc = [scores[j * ng:(j + 1) * ng, :] for j in range(gsz)]
    neg = -jnp.inf

    def lmax(v):
        out = v[0]
        for a in v[1:]:
            out = jnp.maximum(out, a)
        return out

    def lmin(v):
        out = v[0]
        for a in v[1:]:
            out = jnp.minimum(out, a)
        return out

    m1 = lmax(xs)
    i1 = lmin([jnp.where(xs[j] == m1, j, gsz) for j in range(gsz)])
    m2 = lmax([jnp.where(i1 == j, neg, xs[j]) for j in range(gsz)])
    cur = m1 + m2
    giota = lax.broadcasted_iota(I32, (ng, tm), 0)
    gsel = jnp.zeros((ng, tm), F32)
    for _ in range(TOPK_EXPERT_GROUPS):
        m = jnp.max(cur, axis=0, keepdims=True)
        gi = jnp.min(jnp.where(cur == m, giota, ng), axis=0, keepdims=True)
        hit = giota == gi
        gsel = jnp.where(hit, 1.0, gsel)
        cur = jnp.where(hit, neg, cur)
    gmask = gsel > 0.0
    xs = [jnp.where(gmask, x, neg) for x in xs]
    eid = [giota * gsz + j for j in range(gsz)]
    n_e = ng * gsz
    vals = []
    for r in range(TOP_K):
        m = jnp.max(lmax(xs), axis=0, keepdims=True)
        ci = jnp.min(lmin([jnp.where(xs[j] == m, eid[j], n_e) for j in range(gsz)]),
                     axis=0, keepdims=True)
        hits = [eid[j] == ci for j in range(gsz)]
        v = sum(jnp.where(hits[j], sc[j], 0.0) for j in range(gsz))
        vals.append(jnp.sum(v, axis=0, keepdims=True))
        xs = [jnp.where(hits[j], neg, xs[j]) for j in range(gsz)]
        e_ref[r:r + 1, :] = ci
    tot = sum(vals)
    for r in range(TOP_K):
        w_ref[r:r + 1, :] = vals[r] / tot * ROUTED_SCALE


def _router(x1, ada_blk, wr_perm, rb_perm, tm=512):
    n, d = x1.shape
    e = wr_perm.shape[0]
    nsub = tm // ADA_BLOCK
    return pl.pallas_call(
        functools.partial(_router_kernel, nsub=nsub),
        out_shape=(jax.ShapeDtypeStruct((n, d // 2), jnp.uint32),
                   jax.ShapeDtypeStruct((TOP_K, n), I32),
                   jax.ShapeDtypeStruct((TOP_K, n), F32)),
        grid=(n // tm,),
        in_specs=[pl.BlockSpec((tm, d), lambda i: (i, 0)),
                  pl.BlockSpec((nsub, 6, d), lambda i: (i, 0, 0)),
                  pl.BlockSpec((e, d), lambda i: (0, 0)),
                  pl.BlockSpec((e, 1), lambda i: (0, 0))],
        out_specs=(pl.BlockSpec((tm, d // 2), lambda i: (i, 0)),
                   pl.BlockSpec((TOP_K, tm), lambda i: (0, i)),
                   pl.BlockSpec((TOP_K, tm), lambda i: (0, i))),
        scratch_shapes=[pltpu.VMEM((tm, d), F32)],
        compiler_params=_params(("arbitrary",)),
        name="router",
    )(x1, ada_blk, wr_perm, rb_perm)


def _plan_kernel(e_ref, pos_ref, st_ref, vis_ref, rank_scr, *, n_e, blk, tm):
    n = e_ref.shape[1]
    nblk = n // blk
    row = lax.broadcasted_iota(I32, (blk, blk), 0)
    col = lax.broadcasted_iota(I32, (blk, blk), 1)
    upper = jnp.where(row <= col, 1.0, 0.0).astype(BF16)
    eid = lax.broadcasted_iota(I32, (n_e, blk), 0)

    def count_body(cb, carry):
        cols = pl.ds(pl.multiple_of(cb * blk, blk), blk)
        e_blk = e_ref[:, cols]
        hit = jnp.zeros((n_e, blk), F32)
        for k in range(TOP_K):
            hit = hit + jnp.where(e_blk[k:k + 1, :] == eid, 1.0, 0.0)
        cs = _dot(hit.astype(BF16), upper) + carry
        rank_scr[:, cols] = cs - hit
        return cs[:, blk - 1:blk]

    counts = lax.fori_loop(0, nblk, count_body, jnp.zeros((n_e, 1), F32))

    hi = jnp.floor(counts * (1.0 / 128.0))
    lo = counts - hi * 128.0
    er = lax.broadcasted_iota(I32, (n_e, n_e), 0)
    ec = lax.broadcasted_iota(I32, (n_e, n_e), 1)
    lower = jnp.where(ec < er, 1.0, 0.0).astype(BF16)
    lower_incl = jnp.where(ec <= er, 1.0, 0.0).astype(BF16)
    wide = lambda v: jnp.broadcast_to(v, (n_e, LANES)).astype(BF16)
    starts = (_dot(lower, wide(hi)) * 128.0 + _dot(lower, wide(lo)))[:, :1]
    st_ref[...] = jnp.broadcast_to(starts, st_ref.shape).astype(I32)

    inv_tm = 1.0 / tm
    nonempty = counts > 0.0
    first_t = jnp.floor(starts * inv_tm)
    nvis = jnp.where(nonempty, jnp.floor((starts + counts - 1.0) * inv_tm) - first_t + 1.0, 0.0)
    vend = _dot(lower_incl, wide(nvis))[:, :1]
    gidx = _dot(lower_incl, wide(jnp.where(nonempty, 1.0, 0.0)))[:, :1] - 1.0
    total = jnp.max(vend, axis=0, keepdims=True)
    nv = vis_ref.shape[1]
    viota = lax.broadcasted_iota(I32, (1, nv), 1).astype(F32)
    vc = jnp.minimum(viota, total - 1.0)
    e_v = jnp.sum(jnp.where(vend <= vc, 1.0, 0.0), axis=0, keepdims=True)
    eio = lax.broadcasted_iota(I32, (n_e, nv), 0).astype(F32)
    mine = eio == e_v
    pick = lambda colv: jnp.sum(jnp.where(mine, colv, 0.0), axis=0, keepdims=True)
    tile_v = pick(first_t) + (vc - pick(vend - nvis))
    g_v = pick(gidx)
    slot_v = g_v - 2.0 * jnp.floor(g_v * 0.5)
    is_next = nonempty & (gidx == g_v + 1.0)
    nxt_v = (jnp.sum(jnp.where(is_next, eio + 1.0, 0.0), axis=0, keepdims=True) - 1.0)
    rows = [tile_v, e_v, jnp.where(viota < total, 1.0, 0.0), slot_v, nxt_v]
    rows += [jnp.zeros((1, nv), F32)] * (vis_ref.shape[0] - len(rows))
    vis_ref[...] = jnp.concatenate(rows, axis=0).astype(I32)

    def pos_body(cb, c):
        cols = pl.ds(pl.multiple_of(cb * blk, blk), blk)
        e_blk = e_ref[:, cols]
        val = rank_scr[:, cols] + starts
        for k in range(TOP_K):
            p = jnp.sum(jnp.where(e_blk[k:k + 1, :] == eid, val, 0.0), axis=0, keepdims=True)
            pos_ref[k:k + 1, cols] = p.astype(I32)
        return c

    lax.fori_loop(0, nblk, pos_body, 0)


def _plan(eidx_t, n_e, tm, blk=256):
    k, n = eidx_t.shape
    n_visits = (n * k) // tm + n_e - 1
    nv = -(-n_visits // LANES) * LANES
    pos, starts, vis = pl.pallas_call(
        functools.partial(_plan_kernel, n_e=n_e, blk=blk, tm=tm),
        out_shape=(jax.ShapeDtypeStruct((k, n), I32), jax.ShapeDtypeStruct((n_e, LANES), I32),
                   jax.ShapeDtypeStruct((SUBLANES, nv), I32)),
        scratch_shapes=[pltpu.VMEM((n_e, n), F32)],
        compiler_params=pltpu.CompilerParams(vmem_limit_bytes=VMEM_LIMIT_BYTES),
        name="moe_plan",
    )(eidx_t)
    return pos, starts, vis[:, :n_visits]


def _dispatch_kernel(pos_ref, h_ref, xs_ref, sem, *, tm):
    i = pl.program_id(0)

    def issue(r, c):
        base = (i * tm + r) * TOP_K
        for k in range(TOP_K):
            p = pos_ref[base + k]
            pltpu.make_async_copy(h_ref.at[pl.ds(r, 1), :], xs_ref.at[pl.ds(p, 1), :],
                                  sem).start(priority=k % 2)
        return c

    lax.fori_loop(0, tm, issue, 0)
    for k in range(TOP_K):
        pltpu.make_async_copy(h_ref, xs_ref.at[pl.ds(0, tm), :], sem).wait()


def _dispatch(pos_flat, h2, tm=256):
    n, d = h2.shape
    return pl.pallas_call(
        functools.partial(_dispatch_kernel, tm=tm),
        out_shape=jax.ShapeDtypeStruct((n * TOP_K, d), h2.dtype),
        grid_spec=pltpu.PrefetchScalarGridSpec(
            num_scalar_prefetch=1,
            grid=(n // tm,),
            in_specs=[pl.BlockSpec((tm, d), lambda i, pos: (i, 0))],
            out_specs=pl.BlockSpec(memory_space=pl.ANY),
            scratch_shapes=[pltpu.SemaphoreType.DMA(())]),
        compiler_params=_params(("arbitrary",)),
        name="moe_dispatch",
    )(pos_flat, h2)


def _gmm_kernel(vt_ref, ve_ref, vv_ref, vs_ref, nx_ref, bd_ref,
                xs_ref, wg_hbm, wu_hbm, wd_hbm, ys_ref,
                wg32, wu32, wd32, wg16, wu16, wd16, acc, sem, *, tm):
    v = pl.program_id(0)
    t = vt_ref[v]
    e = ve_ref[v]
    slot = vs_ref[v]
    pv = jnp.maximum(v - 1, 0)
    first = v == 0
    valid = vv_ref[v] == 1
    r0 = t * tm
    lo = bd_ref[e]
    hi = bd_ref[e + 1]
    whole = (r0 >= lo) & (r0 + tm <= hi)

    def weight_copies(expert, s):
        return (pltpu.make_async_copy(wg_hbm.at[expert], wg32.at[s], sem.at[s, 0]),
                pltpu.make_async_copy(wu_hbm.at[expert], wu32.at[s], sem.at[s, 1]),
                pltpu.make_async_copy(wd_hbm.at[expert], wd32.at[s], sem.at[s, 2]))

    @pl.when(first)
    def _():
        for c in weight_copies(e, slot):
            c.start()

    @pl.when(first | (e != ve_ref[pv]))
    def _():
        for c in weight_copies(e, slot):
            c.wait()
        nxt = nx_ref[v]

        @pl.when(nxt >= 0)
        def _():
            for c in weight_copies(nxt, 1 - slot):
                c.start()

        wg16[...] = wg32[slot].astype(BF16)
        wu16[...] = wu32[slot].astype(BF16)
        wd16[...] = wd32[slot].astype(BF16)

    @pl.when(valid & jnp.logical_not(whole) & (first | (t != vt_ref[pv])))
    def _():
        acc[...] = jnp.zeros(acc.shape, F32)

    @pl.when(valid)
    def _():
        x = jnp.concatenate(_unpack_halves(xs_ref[...]), axis=1).astype(BF16)
        g = _dot(x, wg16[...])
        u = _dot(x, wu16[...])
        act = (g * jax.nn.sigmoid(g) * u).astype(BF16)
        y = _dot(act, wd16[...])

        @pl.when(whole)
        def _():
            ys_ref[...] = _pack_halves(y)

        @pl.when(jnp.logical_not(whole))
        def _():
            row = r0 + lax.broadcasted_iota(I32, (tm, 1), 0)
            acc[...] += jnp.where((row >= lo) & (row < hi), y, 0.0)
            ys_ref[...] = _pack_halves(acc[...])


def _gmm(tile, expert, valid, slot, nxt, bounds, xs, w_gate, w_up, w_down, tm=256):
    nk, dh = xs.shape
    n_e, d, f = w_gate.shape
    n_visits = tile.shape[0]
    row_map = lambda v, vt, ve, vv, vs, nx, bd: (vt[v], 0)
    return pl.pallas_call(
        functools.partial(_gmm_kernel, tm=tm),
        out_shape=jax.ShapeDtypeStruct((nk, dh), xs.dtype),
        grid_spec=pltpu.PrefetchScalarGridSpec(
            num_scalar_prefetch=6,
            grid=(n_visits,),
            in_specs=[pl.BlockSpec((tm, dh), row_map),
                      pl.BlockSpec(memory_space=pl.ANY),
                      pl.BlockSpec(memory_space=pl.ANY),
                      pl.BlockSpec(memory_space=pl.ANY)],
            out_specs=pl.BlockSpec((tm, dh), row_map),
            scratch_shapes=[pltpu.VMEM((2, d, f), F32), pltpu.VMEM((2, d, f), F32),
                            pltpu.VMEM((2, f, d), F32),
                            pltpu.VMEM((d, f), BF16), pltpu.VMEM((d, f), BF16),
                            pltpu.VMEM((f, d), BF16),
                            pltpu.VMEM((tm, d), F32),
                            pltpu.SemaphoreType.DMA((2, 3))]),
        compiler_params=_params(("arbitrary",)),
        name="moe_experts",
    )(tile, expert, valid, slot, nxt, bounds, xs, w_gate, w_up, w_down)


def _combine_kernel(pos_ref, ys_ref, w_ref, hp_ref, x_ref, ada_ref, wsg_ref, wsu_ref, wsd_ref,
                    lg_ref, lb_ref, op_ref, os_ref, buf, sem, *, tm, nsub, alpha, np_tiles):
    i = pl.program_id(0)

    def issue(r, c):
        base = (i * tm + r) * TOP_K
        for k in range(TOP_K):
            p = pos_ref[base + k]
            pltpu.make_async_copy(ys_ref.at[pl.ds(p, 1), :], buf.at[k, pl.ds(r, 1), :],
                                  sem).start(priority=k % 2)
        return c

    lax.fori_loop(0, tm, issue, 0)

    h16 = jnp.concatenate(_unpack_halves(hp_ref[...]), axis=1).astype(BF16)
    g = _dot(h16, wsg_ref[...])
    u = _dot(h16, wsu_ref[...])
    shared = _dot((g * jax.nn.sigmoid(g) * u).astype(BF16), wsd_ref[...])

    for k in range(TOP_K):
        pltpu.make_async_copy(ys_ref.at[pl.ds(0, tm), :], buf.at[k], sem).wait()
    lo = hi = None
    for k in range(TOP_K):
        a, b = _unpack_halves(buf[k])
        wk = w_ref[:, k:k + 1]
        lo = wk * a if lo is None else lo + wk * a
        hi = wk * b if hi is None else hi + wk * b
    ffn = shared + jnp.concatenate([lo, hi], axis=1)

    def finish(o_ref):
        for s in range(nsub):
            rows = slice(s * ADA_BLOCK, (s + 1) * ADA_BLOCK)
            g2 = ada_ref[s, 5:6, :]
            y = alpha * x_ref[rows, :] + (1.0 + g2) * ffn[rows, :]
            o_ref[rows, :] = _layer_norm(y, lg_ref[...], lb_ref[...])

    @pl.when(i < np_tiles)
    def _():
        finish(op_ref)

    @pl.when(i >= np_tiles)
    def _():
        finish(os_ref)


def _combine(pos_flat, ys, w_tok, h2p, x1, ada_blk, wsg16, wsu16, wsd16, ln_g, ln_b, *,
             alpha, n_p, tm=256):
    n, d = x1.shape
    dh = ys.shape[1]
    f = wsg16.shape[1]
    nsub = tm // ADA_BLOCK
    np_tiles = n_p // tm
    return pl.pallas_call(
        functools.partial(_combine_kernel, tm=tm, nsub=nsub, alpha=alpha, np_tiles=np_tiles),
        out_shape=(jax.ShapeDtypeStruct((n_p, d), F32), jax.ShapeDtypeStruct((n - n_p, d), F32)),
        grid_spec=pltpu.PrefetchScalarGridSpec(
            num_scalar_prefetch=1,
            grid=(n // tm,),
            in_specs=[pl.BlockSpec(memory_space=pl.ANY),
                      pl.BlockSpec((tm, TOP_K), lambda i, pos: (i, 0)),
                      pl.BlockSpec((tm, dh), lambda i, pos: (i, 0)),
                      pl.BlockSpec((tm, d), lambda i, pos: (i, 0)),
                      pl.BlockSpec((nsub, 6, d), lambda i, pos: (i, 0, 0)),
                      pl.BlockSpec((d, f), lambda i, pos: (0, 0)),
                      pl.BlockSpec((d, f), lambda i, pos: (0, 0)),
                      pl.BlockSpec((f, d), lambda i, pos: (0, 0)),
                      pl.BlockSpec((1, d), lambda i, pos: (0, 0)),
                      pl.BlockSpec((1, d), lambda i, pos: (0, 0))],
            out_specs=(pl.BlockSpec((tm, d), lambda i, pos: (jnp.minimum(i, np_tiles - 1), 0)),
                       pl.BlockSpec((tm, d), lambda i, pos: (jnp.maximum(i - np_tiles, 0), 0))),
            scratch_shapes=[pltpu.VMEM((TOP_K, tm, dh), ys.dtype), pltpu.SemaphoreType.DMA(())]),
        compiler_params=_params(("arbitrary",)),
        name="moe_combine",
    )(pos_flat, ys, w_tok, h2p, x1, ada_blk, wsg16, wsu16, wsd16, ln_g, ln_b)


def _layer(x_p, x_s, c_pad, blk_batch, past_k, past_v, past_logf, st_re, st_im, lp, dims):
    bp, tp, bs, ts, d = dims
    n_p, n_s = bp * tp, bs * ts
    n = n_p + n_s
    ds = d // 2
    n_h = ds // FOX_HEAD_DIM
    n_hp = n_h // 2
    g_n, p_n = lp["ssm_lambda_re"].shape
    n_e = lp["w_router"].shape[1]
    depth_alpha = lp["alpha"]

    ada = _ada(c_pad, lp["w_ada"], lp["b_ada"])
    ada_blk = ada.reshape(ada.shape[0], 6, d)[blk_batch]

    w_in = lp["w_in"]
    w_main = jnp.concatenate([w_in[:, :4 * ds], w_in[:, 4 * ds + n_h:]], axis=1).astype(BF16)
    wf_t = w_in[:, 4 * ds:4 * ds + n_h].T.astype(BF16)
    z32, qkv16, logf_t = _inproj(x_p, x_s, ada_blk, w_main, wf_t, lp["b_f"].reshape(n_h, 1))

    lbr, lbi, bbr, bbi = _ssm_disc(lp["ssm_lambda_re"], lp["ssm_lambda_im"], lp["ssm_log_dt"],
                                   lp["ssm_b_re"], lp["ssm_b_im"])
    lam_lay = _to_state_layout(lbr[None], lbi[None])
    bd = jnp.concatenate([_block_diag_slabs(bbr.transpose(0, 2, 1)),
                          _block_diag_slabs(bbi.transpose(0, 2, 1))], axis=2).astype(BF16)
    cd = jnp.stack([_block_diag_slabs(lp["ssm_c_re"].transpose(0, 2, 1)),
                    _block_diag_slabs(lp["ssm_c_im"].transpose(0, 2, 1))], axis=1).astype(BF16)
    d_row = lp["ssm_d"].reshape(1, ds)
    x0_p = jnp.zeros((bp, 2 * g_n * p_n), F32)
    x0_s = _to_state_layout(st_re, st_im)
    g_p, xl_p = _ssm(z32, x0_p, lam_lay, bd, cd, d_row, row0=0, n_seq=bp, seq_len=tp,
                     s_blk=bp, tt=min(128, tp))
    g_s, xl_s = _ssm(z32, x0_s, lam_lay, bd, cd, d_row, row0=n_p, n_seq=bs, seq_len=ts,
                     s_blk=min(8, bs), tt=ts)
    g16 = jnp.concatenate(g_p + g_s, axis=0)
    br_ssm = _glu(g16, lp["w_glu"].astype(BF16))

    lf_p = logf_t[:, :n_p].reshape(n_h, bp, tp).transpose(1, 0, 2)
    lf_s = logf_t[:, n_p:].reshape(n_h, bs, ts).transpose(1, 0, 2)
    ck_p = _cumsum_lanes(lf_p.reshape(bp * n_h, tp)).reshape(bp, n_hp, 2, tp)
    past = past_logf.shape[1]
    cat = jnp.concatenate([past_logf.astype(F32).transpose(0, 2, 1), lf_s], axis=2)
    width = -(-(past + ts) // 256) * 256
    cat = jnp.pad(cat, ((0, 0), (0, 0), (0, width - past - ts)))
    nh_s = 4
    ck_s = _cumsum_lanes(cat.reshape(bs * n_h, width)).reshape(bs, n_h // nh_s, nh_s, width)
    attn_p = _attn_prompt(qkv16, ck_p, n_batch=bp, seq_len=tp, n_hp=n_hp, tq=min(512, tp))
    attn_s = _attn_sample(qkv16, past_k.reshape(bs, past, ds).astype(BF16),
                          past_v.reshape(bs, past, ds).astype(BF16),
                          ck_s[..., :past], ck_s[..., past:past + ts],
                          row0=n_p, n_batch=bs, seq_len=ts, n_h=n_h, nh=nh_s)
    attn16 = jnp.concatenate([attn_p, attn_s], axis=0)

    x1 = _post(attn16, br_ssm, z32, x_p, x_s, ada_blk, lp["w_fox_o"].astype(BF16),
               lp["w_out"].astype(BF16), lp["ln1_g"].reshape(1, d), lp["ln1_b"].reshape(1, d),
               alpha=depth_alpha)

    gsz = n_e // N_EXPERT_GROUPS
    perm = lambda a: a.reshape(N_EXPERT_GROUPS, gsz, -1).transpose(1, 0, 2).reshape(n_e, -1)
    h2p, eidx_t, w_t = _router(x1, ada_blk, perm(lp["w_router"].T), perm(lp["router_bias"].reshape(n_e, 1)))
    tm_e = 256
    pos_t, starts, vis = _plan(eidx_t, n_e, tm_e)
    pos_flat = pos_t.T.reshape(n * TOP_K)
    bounds = jnp.concatenate([starts[:, 0], jnp.full((1,), n * TOP_K, I32)])
    xs = _dispatch(pos_flat, h2p)
    ys = _gmm(vis[0], vis[1], vis[2], vis[3], vis[4], bounds, xs, lp["w_exp_gate"], lp["w_exp_up"],
              lp["w_exp_down"], tm=tm_e)
    x2_p, x2_s = _combine(pos_flat, ys, w_t.T, h2p, x1, ada_blk, lp["w_sh_gate"].astype(BF16),
                          lp["w_sh_up"].astype(BF16), lp["w_sh_down"].astype(BF16),
                          lp["ln2_g"].reshape(1, d), lp["ln2_b"].reshape(1, d),
                          alpha=depth_alpha, n_p=n_p)

    k_new = z32[:, 2 * ds:3 * ds]
    v_new = z32[:, 3 * ds:4 * ds]
    logf = logf_t.T
    ssm_p = _from_state_layout(xl_p, g_n, p_n)
    ssm_s = _from_state_layout(xl_s, g_n, p_n)
    return x2_p, x2_s, k_new, v_new, logf, ssm_p, ssm_s


def kernel(x_prompt, x_sample, cache_k, cache_v, cache_logf, state_ssm_re, state_ssm_im, c_prompt, c_sample, w_ada, b_ada, w_in, b_f, ssm_lambda_re, ssm_lambda_im, ssm_log_dt, ssm_b_re, ssm_b_im, ssm_c_re, ssm_c_im, ssm_d, w_glu, w_fox_o, w_out, ln1_g, ln1_b, w_router, router_bias, w_exp_gate, w_exp_up, w_exp_down, w_sh_gate, w_sh_up, w_sh_down, ln2_g, ln2_b):
    bp, tp, d = x_prompt.shape
    bs, ts, _ = x_sample.shape
    depth = w_ada.shape[0]
    n_p, n_s = bp * tp, bs * ts
    n_h = d // 2 // FOX_HEAD_DIM
    assert tp % ADA_BLOCK == 0 and ts == ADA_BLOCK
    alpha = (2.0 * depth) ** 0.25

    x_p, x_s = x_prompt.reshape(n_p, d), x_sample.reshape(n_s, d)
    c_all = jnp.concatenate([c_prompt, c_sample], axis=0)
    c_pad = jnp.pad(c_all, ((0, -(bp + bs) % 16), (0, 0)))
    blk_batch = np.concatenate([np.repeat(np.arange(bp), tp // ADA_BLOCK),
                                bp + np.repeat(np.arange(bs), ts // ADA_BLOCK)])
    dims = (bp, tp, bs, ts, d)
    outs_p, outs_s = [], []
    for l in range(depth):
        lp = dict(w_ada=w_ada[l], b_ada=b_ada[l], w_in=w_in[l], b_f=b_f[l],
                  ssm_lambda_re=ssm_lambda_re[l], ssm_lambda_im=ssm_lambda_im[l],
                  ssm_log_dt=ssm_log_dt[l], ssm_b_re=ssm_b_re[l], ssm_b_im=ssm_b_im[l],
                  ssm_c_re=ssm_c_re[l], ssm_c_im=ssm_c_im[l], ssm_d=ssm_d[l], w_glu=w_glu[l],
                  w_fox_o=w_fox_o[l], w_out=w_out[l], ln1_g=ln1_g[l], ln1_b=ln1_b[l],
                  w_router=w_router[l], router_bias=router_bias[l], w_exp_gate=w_exp_gate[l],
                  w_exp_up=w_exp_up[l], w_exp_down=w_exp_down[l], w_sh_gate=w_sh_gate[l],
                  w_sh_up=w_sh_up[l], w_sh_down=w_sh_down[l], ln2_g=ln2_g[l], ln2_b=ln2_b[l],
                  alpha=alpha)
        x_p, x_s, k_new, v_new, logf, ssm_p, ssm_s = _layer(
            x_p, x_s, c_pad, blk_batch, cache_k[l], cache_v[l], cache_logf[l],
            state_ssm_re[l].astype(F32), state_ssm_im[l].astype(F32), lp, dims)
        hd = FOX_HEAD_DIM
        outs_p.append((k_new[:n_p].reshape(bp, tp, n_h, hd), v_new[:n_p].reshape(bp, tp, n_h, hd),
                       logf[:n_p].reshape(bp, tp, n_h), ssm_p[0], ssm_p[1]))
        outs_s.append((k_new[n_p:].reshape(bs, ts, n_h, hd), v_new[n_p:].reshape(bs, ts, n_h, hd),
                       logf[n_p:].reshape(bs, ts, n_h), ssm_s[0], ssm_s[1]))
    stack = lambda outs, i: jnp.stack([o[i] for o in outs])
    return (x_p.reshape(bp, tp, d), x_s.reshape(bs, ts, d),
            stack(outs_p, 0), stack(outs_p, 1), stack(outs_p, 2), stack(outs_p, 3), stack(outs_p, 4),
            stack(outs_s, 0), stack(outs_s, 1), stack(outs_s, 2), stack(outs_s, 3), stack(outs_s, 4))
```

```python
import functools

import jax
import jax.numpy as jnp
import numpy as np
from jax import lax
from jax.experimental import pallas as pl
from jax.experimental.pallas import tpu as pltpu

F32 = jnp.float32
BF16 = jnp.bfloat16
I32 = jnp.int32

V7X_VMEM_BYTES = 64 * 1024 * 1024
VMEM_LIMIT_BYTES = V7X_VMEM_BYTES - 8 * 1024 * 1024
LANES = 128
SUBLANES = 8

SSM_GROUP_WIDTH = 16
SSM_STATE = 64
SSM_GROUPS_PER_SLAB = 8
FOX_HEAD_DIM = 64
N_EXPERT_GROUPS = 8
TOPK_EXPERT_GROUPS = 4
TOP_K = 8
ROUTED_SCALE = 2.5
LN_EPS = 1e-5
ADA_BLOCK = 64

NT_DIMS = (((1,), (1,)), ((), ()))


def _params(sem, vmem=VMEM_LIMIT_BYTES):
    return pltpu.CompilerParams(dimension_semantics=sem, vmem_limit_bytes=vmem)


def _dot(a, b):
    return jnp.dot(a, b, preferred_element_type=F32)


def _dot_nt(a, b):
    return lax.dot_general(a, b, NT_DIMS, preferred_element_type=F32)


def _split_bf16(x):
    hi = x.astype(BF16)
    lo = (x - hi.astype(F32)).astype(BF16)
    return hi, lo


def _log_sigmoid(x):
    return jnp.minimum(x, 0.0) - jnp.log1p(jnp.exp(-jnp.abs(x)))


def _gelu_tanh(x):
    c = np.float32(np.sqrt(2.0 / np.pi))
    return x * (0.5 * (1.0 + jnp.tanh(c * (x + 0.044715 * (x * x * x)))))


def _layer_norm(y, g, b):
    mu = jnp.mean(y, axis=-1, keepdims=True)
    yc = y - mu
    var = jnp.mean(yc * yc, axis=-1, keepdims=True)
    return yc * lax.rsqrt(var + LN_EPS) * g + b


def _ada_kernel(c_ref, w_ref, b_ref, o_ref):
    c = c_ref[...]
    a_hi, a_lo = _split_bf16(c * jax.nn.sigmoid(c))
    w_hi, w_lo = _split_bf16(w_ref[...])
    acc = _dot(a_hi, w_lo) + _dot(a_lo, w_hi)
    o_ref[...] = acc + _dot(a_hi, w_hi) + b_ref[...]


def _ada(c_pad, w_ada, b_ada):
    m, d = c_pad.shape
    n = w_ada.shape[1]
    tn = 1024
    return pl.pallas_call(
        _ada_kernel,
        out_shape=jax.ShapeDtypeStruct((m, n), F32),
        grid=(n // tn,),
        in_specs=[pl.BlockSpec((m, d), lambda j: (0, 0)),
                  pl.BlockSpec((d, tn), lambda j: (0, j)),
                  pl.BlockSpec((1, tn), lambda j: (0, j))],
        out_specs=pl.BlockSpec((m, tn), lambda j: (0, j)),
        compiler_params=_params(("arbitrary",)),
        name="ada",
    )(c_pad, w_ada, b_ada.reshape(1, n))


def _inproj_kernel(xp_ref, xs_ref, ada_ref, w_ref, wf_ref, bf_ref,
                   u_ref, kv_ref, g_ref, qkv_ref, lf_ref, h_scr, *, nsub, q_scale, np_tiles):
    i = pl.program_id(0)
    j = pl.program_id(1)

    def modulate(x_ref):
        for s in range(nsub):
            rows = slice(s * ADA_BLOCK, (s + 1) * ADA_BLOCK)
            sh = ada_ref[s, 0:1, :]
            sc = ada_ref[s, 1:2, :]
            h_scr[rows, :] = (x_ref[rows, :] * (1.0 + sc) + sh).astype(BF16)

    @pl.when((j == 0) & (i < np_tiles))
    def _():
        modulate(xp_ref)

    @pl.when((j == 0) & (i >= np_tiles))
    def _():
        modulate(xs_ref)

    @pl.when(j == 0)
    def _():
        f = _dot_nt(wf_ref[...], h_scr[...])
        lf_ref[...] = _log_sigmoid(f + bf_ref[...])

    zt = _dot(h_scr[...], w_ref[...])

    @pl.when(j == 0)
    def _():
        u_ref[...] = zt

    @pl.when(j == 1)
    def _():
        qkv_ref[...] = (zt * q_scale).astype(BF16)

    @pl.when((j == 2) | (j == 3))
    def _():
        kv_ref[0] = zt
        qkv_ref[...] = zt.astype(BF16)

    @pl.when(j >= 4)
    def _():
        g_ref[...] = zt


def _two_part_specs(tm, d, np_tiles, n_grid_args):
    if n_grid_args == 1:
        return [pl.BlockSpec((tm, d), lambda i: (jnp.minimum(i, np_tiles - 1), 0)),
                pl.BlockSpec((tm, d), lambda i: (jnp.maximum(i - np_tiles, 0), 0))]
    return [pl.BlockSpec((tm, d), lambda i, j: (jnp.minimum(i, np_tiles - 1), 0)),
            pl.BlockSpec((tm, d), lambda i, j: (jnp.maximum(i - np_tiles, 0), 0))]


def _inproj(x_p, x_s, ada_blk, w_main, wf_t, bf_col, tm=512):
    n_p, d = x_p.shape
    n = n_p + x_s.shape[0]
    ds = d // 2
    h = wf_t.shape[0]
    nsub = tm // ADA_BLOCK
    ncol = w_main.shape[1] // ds
    np_tiles = n_p // tm
    kern = functools.partial(_inproj_kernel, nsub=nsub, q_scale=FOX_HEAD_DIM ** -0.5,
                             np_tiles=np_tiles)
    return pl.pallas_call(
        kern,
        out_shape=(jax.ShapeDtypeStruct((n, ds), F32),
                   jax.ShapeDtypeStruct((2, n, ds), F32),
                   jax.ShapeDtypeStruct((n, 2 * d), F32),
                   jax.ShapeDtypeStruct((n, 3 * ds), BF16),
                   jax.ShapeDtypeStruct((h, n), F32)),
        grid=(n // tm, ncol),
        in_specs=_two_part_specs(tm, d, np_tiles, 2) + [
                  pl.BlockSpec((nsub, 6, d), lambda i, j: (i, 0, 0)),
                  pl.BlockSpec((d, ds), lambda i, j: (0, j)),
                  pl.BlockSpec((h, d), lambda i, j: (0, 0)),
                  pl.BlockSpec((h, 1), lambda i, j: (0, 0))],
        out_specs=(pl.BlockSpec((tm, ds), lambda i, j: (i, 0)),
                   pl.BlockSpec((1, tm, ds), lambda i, j: (jnp.clip(j - 2, 0, 1), i, 0)),
                   pl.BlockSpec((tm, ds), lambda i, j: (i, jnp.clip(j - 4, 0, ncol - 5))),
                   pl.BlockSpec((tm, ds), lambda i, j: (i, jnp.clip(j - 1, 0, 2))),
                   pl.BlockSpec((h, tm), lambda i, j: (0, i))),
        scratch_shapes=[pltpu.VMEM((tm, d), BF16)],
        compiler_params=_params(("arbitrary", "arbitrary")),
        name="inproj",
    )(x_p, x_s, ada_blk, w_main, wf_t, bf_col)


def _ssm_disc_kernel(lr_ref, li_ref, ldt_ref, br_ref, bi_ref,
                     lbr_ref, lbi_ref, bbr_ref, bbi_ref):
    lr = jnp.minimum(lr_ref[...], -1e-4)
    li = li_ref[...]
    dt = jnp.exp(ldt_ref[...])
    er = jnp.exp(lr * dt)
    lbr = er * jnp.cos(li * dt)
    lbi = er * jnp.sin(li * dt)
    lbr_ref[...] = lbr
    lbi_ref[...] = lbi
    nr = lbr - 1.0
    den = lr * lr + li * li
    qr = (nr * lr + lbi * li) / den
    qi = (lbi * lr - nr * li) / den
    b_r = br_ref[...]
    b_i = bi_ref[...]
    bbr_ref[...] = qr * b_r - qi * b_i
    bbi_ref[...] = qr * b_i + qi * b_r


def _ssm_disc(lam_re, lam_im, log_dt, b_re, b_im):
    g, p, w = b_re.shape
    rep = lambda a: jnp.repeat(a, w, axis=1)
    shp = jax.ShapeDtypeStruct((g, p * w), F32)
    lbr, lbi, bbr, bbi = pl.pallas_call(
        _ssm_disc_kernel, out_shape=(shp, shp, shp, shp), name="ssm_disc",
    )(rep(lam_re), rep(lam_im), log_dt.reshape(g, 1),
      b_re.reshape(g, p * w), b_im.reshape(g, p * w))
    return (lbr[:, ::w], lbi[:, ::w], bbr.reshape(g, p, w), bbi.reshape(g, p, w))


def _to_state_layout(re, im):
    s, g, p = re.shape
    ns = g // SSM_GROUPS_PER_SLAB
    r = re.reshape(s, ns, 1, SSM_GROUPS_PER_SLAB * p)
    i = im.reshape(s, ns, 1, SSM_GROUPS_PER_SLAB * p)
    return jnp.concatenate([r, i], axis=2).reshape(s, 2 * g * p)


def _from_state_layout(x, g, p):
    s = x.shape[0]
    y = x.reshape(s, g // SSM_GROUPS_PER_SLAB, 2, SSM_GROUPS_PER_SLAB, p)
    return y[:, :, 0].reshape(s, g, p), y[:, :, 1].reshape(s, g, p)


def _block_diag_slabs(a):
    g, m, n = a.shape
    k = SSM_GROUPS_PER_SLAB
    a4 = a.reshape(g // k, k, m, n)
    eye = jnp.eye(k, dtype=bool)
    out = jnp.where(eye[None, :, None, :, None], a4[:, :, :, None, :], 0.0)
    return out.reshape(g // k, k * m, k * n)


def _ssm_kernel(*refs, n_refs, rpr, s_blk, tt, n_slab, sw):
    u_refs = refs[:n_refs]
    x0_ref, lam_ref, bd_ref, cd_ref, d_ref = refs[n_refs:n_refs + 5]
    o_refs = refs[n_refs + 5:2 * n_refs + 5]
    xl_ref = refs[2 * n_refs + 5]
    u_tm, bu, y_tm, st = refs[2 * n_refs + 6:]
    tb = pl.program_id(1)
    uw = SSM_GROUPS_PER_SLAB * SSM_GROUP_WIDTH
    npl = sw // LANES
    nph = npl // 2
    k_sub = SUBLANES // s_blk
    n_rows = s_blk * tt

    def seq_view(refs_, s):
        return (refs_[s], slice(None)) if n_refs == s_blk else (refs_[0], slice(s * tt, (s + 1) * tt))

    @pl.when(tb == 0)
    def _():
        st[...] = jnp.zeros(st.shape, F32)
        st[0:s_blk, :] = x0_ref[...]

    for s in range(s_blk):
        ref, rows = seq_view(u_refs, s)
        for j in range(n_slab):
            u_tm[j, pl.ds(s, tt, stride=s_blk), :] = ref[rows, j * uw:(j + 1) * uw]
    for j in range(n_slab):
        res = _dot(u_tm[j].astype(BF16), bd_ref[j])
        for q in range(npl):
            bu[j * npl + q] = res[:, q * LANES:(q + 1) * LANES]

    sub = lax.broadcasted_iota(I32, (SUBLANES, LANES), 0)
    for j in range(n_slab):
        c0 = j * sw
        a = [jnp.broadcast_to(lam_ref[:, c0 + q * LANES:c0 + (q + 1) * LANES], (SUBLANES, LANES))
             for q in range(npl)]

        def body(i, carry, j=j, a=a):
            rows = pl.ds(pl.multiple_of(i * SUBLANES, SUBLANES), SUBLANES)
            x = list(carry)
            b = [bu[j * npl + q, rows, :] for q in range(npl)]
            out = [None] * npl
            for step in range(k_sub):
                win = (sub >= step * s_blk) & (sub < (step + 1) * s_blk)
                for q in range(nph):
                    xr, xi = x[q], x[nph + q]
                    nr = a[q] * xr - a[nph + q] * xi + b[q]
                    ni = a[q] * xi + a[nph + q] * xr + b[nph + q]
                    out[q] = nr if step == 0 else jnp.where(win, nr, out[q])
                    out[nph + q] = ni if step == 0 else jnp.where(win, ni, out[nph + q])
                    if k_sub > 1:
                        nr = pltpu.roll(nr, s_blk, 0)
                        ni = pltpu.roll(ni, s_blk, 0)
                    x[q], x[nph + q] = nr, ni
            for q in range(npl):
                bu[j * npl + q, rows, :] = out[q]
            return tuple(x)

        init = tuple(st[:, c0 + q * LANES:c0 + (q + 1) * LANES] for q in range(npl))
        fin = lax.fori_loop(0, n_rows // SUBLANES, body, init, unroll=4)
        for q in range(npl):
            st[:, c0 + q * LANES:c0 + (q + 1) * LANES] = fin[q]

    for j in range(n_slab):
        xr16 = jnp.concatenate([bu[j * npl + q] for q in range(nph)], axis=1).astype(BF16)
        xi16 = jnp.concatenate([bu[j * npl + nph + q] for q in range(nph)], axis=1).astype(BF16)
        y_tm[j] = _dot(xr16, cd_ref[j, 0]) - _dot(xi16, cd_ref[j, 1])
    for s in range(s_blk):
        ref, rows = seq_view(u_refs, s)
        o_ref, o_rows = seq_view(o_refs, s)
        for j in range(n_slab):
            cols = slice(j * uw, (j + 1) * uw)
            y = y_tm[j, pl.ds(s, tt, stride=s_blk), :] + d_ref[:, cols] * ref[rows, cols]
            o_ref[o_rows, cols] = _gelu_tanh(y).astype(BF16)

    @pl.when(tb == pl.num_programs(1) - 1)
    def _():
        xl_ref[...] = st[0:s_blk, :]


def _ssm(z32, x0_lay, lam_lay, bd, cd, d_row, *, row0, n_seq, seq_len, s_blk, tt):
    n_slab = bd.shape[0]
    uw = bd.shape[1]
    sw = bd.shape[2]
    ds = n_slab * uw
    state_w = n_slab * sw
    n_tb = seq_len // tt
    n_sg = n_seq // s_blk
    if n_tb == 1:
        n_refs, rpr = 1, s_blk * tt
        assert row0 % rpr == 0
        in_maps = [lambda sg, tb: (row0 // rpr + sg, 0)]
        out_shape = [jax.ShapeDtypeStruct((n_seq * seq_len, ds), BF16)]
        out_maps = [lambda sg, tb: (sg, 0)]
    else:
        assert n_sg == 1 and row0 == 0
        n_refs, rpr = s_blk, tt
        in_maps = [functools.partial(lambda sg, tb, s: (s * n_tb + tb, 0), s=s) for s in range(s_blk)]
        out_shape = [jax.ShapeDtypeStruct((seq_len, ds), BF16)] * s_blk
        out_maps = [lambda sg, tb: (tb, 0)] * s_blk
    rows = n_refs * rpr
    kern = functools.partial(_ssm_kernel, n_refs=n_refs, rpr=rpr, s_blk=s_blk, tt=tt,
                             n_slab=n_slab, sw=sw)
    outs = pl.pallas_call(
        kern,
        out_shape=tuple(out_shape) + (jax.ShapeDtypeStruct((n_seq, state_w), F32),),
        grid=(n_sg, n_tb),
        in_specs=[pl.BlockSpec((rpr, ds), m) for m in in_maps] + [
            pl.BlockSpec((s_blk, state_w), lambda sg, tb: (sg, 0)),
            pl.BlockSpec((1, state_w), lambda sg, tb: (0, 0)),
            pl.BlockSpec(bd.shape, lambda sg, tb: (0, 0, 0)),
            pl.BlockSpec(cd.shape, lambda sg, tb: (0, 0, 0, 0)),
            pl.BlockSpec((1, ds), lambda sg, tb: (0, 0))],
        out_specs=tuple(pl.BlockSpec((rpr, ds), m) for m in out_maps) + (
            pl.BlockSpec((s_blk, state_w), lambda sg, tb: (sg, 0)),),
        scratch_shapes=[pltpu.VMEM((n_slab, rows, uw), F32),
                        pltpu.VMEM((state_w // LANES, rows, LANES), F32),
                        pltpu.VMEM((n_slab, rows, uw), F32),
                        pltpu.VMEM((SUBLANES, state_w), F32)],
        compiler_params=_params(("arbitrary", "arbitrary")),
        name="ssm",
    )(*([z32] * n_refs), x0_lay, lam_lay, bd, cd, d_row)
    return list(outs[:-1]), outs[-1]


def _cumsum_kernel(x_ref, o_ref, *, blk):
    r, t = x_ref.shape
    row = lax.broadcasted_iota(I32, (blk, blk), 0)
    col = lax.broadcasted_iota(I32, (blk, blk), 1)
    upper = jnp.where(row <= col, 1.0, 0.0).astype(BF16)
    carry = jnp.zeros((r, 1), F32)
    for c in range(t // blk):
        x = x_ref[:, c * blk:(c + 1) * blk]
        h1 = x.astype(BF16)
        r1 = x - h1.astype(F32)
        h2 = r1.astype(BF16)
        h3 = (r1 - h2.astype(F32)).astype(BF16)
        s = (_dot(h3, upper) + _dot(h2, upper)) + _dot(h1, upper) + carry
        o_ref[:, c * blk:(c + 1) * blk] = s
        carry = s[:, blk - 1:blk]


def _cumsum_lanes(x, blk=256, tr=64):
    r, t = x.shape
    tr = min(tr, r)
    return pl.pallas_call(
        functools.partial(_cumsum_kernel, blk=blk),
        out_shape=jax.ShapeDtypeStruct((r, t), F32),
        grid=(r // tr,),
        in_specs=[pl.BlockSpec((tr, t), lambda i: (i, 0))],
        out_specs=pl.BlockSpec((tr, t), lambda i: (i, 0)),
        compiler_params=_params(("arbitrary",)),
        name="cumsum",
    )(x)


def _attn_step(q, k, v, ck, m_scr, l_scr, acc_scr, mask):
    lane = lax.broadcasted_iota(I32, (1, 2 * FOX_HEAD_DIM), 1)
    lo = lane < FOX_HEAD_DIM
    pvs, alphas = [], []
    for h in range(2):
        sel = lo if h == 0 else jnp.logical_not(lo)
        qh = jnp.where(sel, q, jnp.zeros_like(q))
        s = _dot_nt(qh, k) - ck[h:h + 1, :]
        if mask is not None:
            s = jnp.where(mask, s, -jnp.inf)
        m_prev = m_scr[h]
        m_new = jnp.maximum(m_prev, jnp.max(s, axis=-1, keepdims=True))
        alpha = jnp.exp(m_prev - m_new)
        p = jnp.exp(s - m_new[:, :1])
        l_scr[h] = alpha * l_scr[h] + jnp.sum(p, axis=-1, keepdims=True)
        m_scr[h] = m_new
        pvs.append(_dot(p.astype(BF16), v))
        alphas.append(alpha)
    acc_scr[...] = (jnp.where(lo, alphas[0], alphas[1]) * acc_scr[...]
                    + jnp.where(lo, pvs[0], pvs[1]))


def _attn_init(m_scr, l_scr, acc_scr):
    m_scr[...] = jnp.full(m_scr.shape, -jnp.inf, F32)
    l_scr[...] = jnp.zeros(l_scr.shape, F32)
    acc_scr[...] = jnp.zeros(acc_scr.shape, F32)


def _attn_finish(o_ref, l_scr, acc_scr):
    lane = lax.broadcasted_iota(I32, (1, 2 * FOX_HEAD_DIM), 1)
    l = jnp.where(lane < FOX_HEAD_DIM, l_scr[0], l_scr[1])
    o_ref[...] = (acc_scr[...] / l).astype(o_ref.dtype)


def _causal_mask(tq, tk):
    return (lax.broadcasted_iota(I32, (tq, tk), 1) <= lax.broadcasted_iota(I32, (tq, tk), 0))


def _attn_prompt_kernel(qt_ref, kt_ref, q_ref, k_ref, v_ref, ck_ref, o_ref, m_scr, l_scr, acc_scr,
                        *, tq):
    step = pl.program_id(2)
    qi, ki = qt_ref[step], kt_ref[step]

    @pl.when(ki == 0)
    def _():
        _attn_init(m_scr, l_scr, acc_scr)

    @pl.when(ki < qi)
    def _():
        _attn_step(q_ref[...], k_ref[...], v_ref[...], ck_ref[0, 0], m_scr, l_scr, acc_scr, None)

    @pl.when(ki == qi)
    def _():
        _attn_step(q_ref[...], k_ref[...], v_ref[...], ck_ref[0, 0], m_scr, l_scr, acc_scr,
                   _causal_mask(tq, tq))
        _attn_finish(o_ref, l_scr, acc_scr)


def _attn_prompt(qkv16, ck, *, n_batch, seq_len, n_hp, tq=512):
    nq = seq_len // tq
    lw = 2 * FOX_HEAD_DIM
    pairs = [(qi, ki) for qi in range(nq) for ki in range(qi + 1)]
    qt = jnp.asarray([p[0] for p in pairs], I32)
    kt = jnp.asarray([p[1] for p in pairs], I32)
    q_map = lambda b, hp, s, qt, kt: (b * nq + qt[s], hp)
    kv_map = lambda c: (lambda b, hp, s, qt, kt: (b * nq + kt[s], c * n_hp + hp))
    return pl.pallas_call(
        functools.partial(_attn_prompt_kernel, tq=tq),
        out_shape=jax.ShapeDtypeStruct((n_batch * seq_len, n_hp * lw), BF16),
        grid_spec=pltpu.PrefetchScalarGridSpec(
            num_scalar_prefetch=2,
            grid=(n_batch, n_hp, len(pairs)),
            in_specs=[pl.BlockSpec((tq, lw), q_map),
                      pl.BlockSpec((tq, lw), kv_map(1)),
                      pl.BlockSpec((tq, lw), kv_map(2)),
                      pl.BlockSpec((1, 1, 2, tq), lambda b, hp, s, qt, kt: (b, hp, 0, kt[s]))],
            out_specs=pl.BlockSpec((tq, lw), q_map),
            scratch_shapes=[pltpu.VMEM((2, tq, lw), F32), pltpu.VMEM((2, tq, lw), F32),
                            pltpu.VMEM((tq, lw), F32)]),
        compiler_params=_params(("arbitrary",) * 3),
        name="attn_prompt",
    )(qt, kt, qkv16, qkv16, qkv16, ck)


def _attn_sample_kernel(q_ref, kn_ref, vn_ref, ckp_ref, ckn_ref, kc_hbm, vc_hbm, o_ref,
                        kbuf, vbuf, sem, *, ts, n_h, layer):
    b = pl.program_id(0)
    n_b = pl.num_programs(0)
    hd = FOX_HEAD_DIM

    def head_copies(bb, h, slot):
        return (pltpu.make_async_copy(kc_hbm.at[layer, bb, :, h, :], kbuf.at[slot], sem.at[slot, 0]),
                pltpu.make_async_copy(vc_hbm.at[layer, bb, :, h, :], vbuf.at[slot], sem.at[slot, 1]))

    @pl.when(b == 0)
    def _():
        for c in head_copies(0, 0, 0):
            c.start()

    mask = _causal_mask(ts, ts)
    for h in range(n_h):
        slot = h % 2
        for c in head_copies(b, h, slot):
            c.wait()
        if h + 1 < n_h:
            for c in head_copies(b, h + 1, 1 - slot):
                c.start()
        else:
            @pl.when(b + 1 < n_b)
            def _():
                for c in head_copies(b + 1, 0, 1 - slot):
                    c.start()
        cols = slice(h * hd, (h + 1) * hd)
        q = q_ref[:, cols]
        s_old = _dot_nt(q, kbuf[slot].astype(BF16)) - ckp_ref[0, h:h + 1, :]
        s_new = jnp.where(mask, _dot_nt(q, kn_ref[:, cols]) - ckn_ref[0, h:h + 1, :], -jnp.inf)
        m = jnp.maximum(jnp.max(s_old, axis=-1, keepdims=True),
                        jnp.max(s_new, axis=-1, keepdims=True))
        p_old = jnp.exp(s_old - m)
        p_new = jnp.exp(s_new - m)
        l = jnp.sum(p_old, axis=-1, keepdims=True) + jnp.sum(p_new, axis=-1, keepdims=True)
        acc = _dot(p_old.astype(BF16), vbuf[slot].astype(BF16)) + _dot(p_new.astype(BF16), vn_ref[:, cols])
        o_ref[:, cols] = (acc / l).astype(o_ref.dtype)


def _attn_sample(qkv16, cache_k, cache_v, ck_past, ck_new, *, layer, row0, n_batch, seq_len, n_h):
    past = cache_k.shape[2]
    hd = FOX_HEAD_DIM
    ds = n_h * hd
    assert n_h % 2 == 0
    rb0 = row0 // seq_len
    new_map = lambda c: (lambda b: (rb0 + b, c))
    return pl.pallas_call(
        functools.partial(_attn_sample_kernel, ts=seq_len, n_h=n_h, layer=layer),
        out_shape=jax.ShapeDtypeStruct((n_batch * seq_len, ds), BF16),
        grid=(n_batch,),
        in_specs=[pl.BlockSpec((seq_len, ds), new_map(0)),
                  pl.BlockSpec((seq_len, ds), new_map(1)),
                  pl.BlockSpec((seq_len, ds), new_map(2)),
                  pl.BlockSpec((1, n_h, past), lambda b: (b, 0, 0)),
                  pl.BlockSpec((1, n_h, seq_len), lambda b: (b, 0, 0)),
                  pl.BlockSpec(memory_space=pl.ANY),
                  pl.BlockSpec(memory_space=pl.ANY)],
        out_specs=pl.BlockSpec((seq_len, ds), lambda b: (b, 0)),
        scratch_shapes=[pltpu.VMEM((2, past, hd), cache_k.dtype), pltpu.VMEM((2, past, hd), cache_v.dtype),
                        pltpu.SemaphoreType.DMA((2, 2))],
        compiler_params=_params(("arbitrary",)),
        name="attn_sample",
    )(qkv16, qkv16, qkv16, ck_past, ck_new, cache_k, cache_v)


def _glu_kernel(g_ref, wa_ref, wb_ref, o_ref):
    g = g_ref[...]
    o_ref[...] = _dot(g, wa_ref[...]) * jax.nn.sigmoid(_dot(g, wb_ref[...]))


def _glu(g16, w_glu16, tm=512, tn=1024):
    n, ds = g16.shape
    d = w_glu16.shape[1] // 2
    tn = min(tn, d)
    nb = d // tn
    return pl.pallas_call(
        _glu_kernel,
        out_shape=jax.ShapeDtypeStruct((n, d), F32),
        grid=(n // tm, nb),
        in_specs=[pl.BlockSpec((tm, ds), lambda i, j: (i, 0)),
                  pl.BlockSpec((ds, tn), lambda i, j: (0, j)),
                  pl.BlockSpec((ds, tn), lambda i, j: (0, nb + j))],
        out_specs=pl.BlockSpec((tm, tn), lambda i, j: (i, j)),
        compiler_params=_params(("arbitrary", "arbitrary")),
        name="glu",
    )(g16, w_glu16, w_glu16)


def _post_kernel(attp_ref, atts_ref, brs_ref, gs_ref, gf_ref, xp_ref, xs_ref, ada_ref, wfo_ref,
                 wo_ref, lg_ref, lb_ref, o_ref, *, nsub, alpha, np_tiles):
    i = pl.program_id(0)
    att = jnp.where(i < np_tiles, attp_ref[...], atts_ref[...])
    br_fox = _dot(att, wfo_ref[...])
    merged = jax.nn.sigmoid(gs_ref[...]) * brs_ref[...] + jax.nn.sigmoid(gf_ref[...]) * br_fox
    mix = _dot(merged.astype(BF16), wo_ref[...])

    def finish(x_ref):
        for s in range(nsub):
            rows = slice(s * ADA_BLOCK, (s + 1) * ADA_BLOCK)
            g1 = ada_ref[s, 2:3, :]
            y = alpha * x_ref[rows, :] + (1.0 + g1) * mix[rows, :]
            o_ref[rows, :] = _layer_norm(y, lg_ref[...], lb_ref[...])

    @pl.when(i < np_tiles)
    def _():
        finish(xp_ref)

    @pl.when(i >= np_tiles)
    def _():
        finish(xs_ref)


def _post(attn_p, attn_s, br_ssm, gates, x_p, x_s, ada_blk, w_fox16, w_out16, ln_g, ln_b, *,
          alpha, tm=256):
    n_p, d = x_p.shape
    n = n_p + x_s.shape[0]
    ds = d // 2
    nsub = tm // ADA_BLOCK
    np_tiles = n_p // tm
    return pl.pallas_call(
        functools.partial(_post_kernel, nsub=nsub, alpha=alpha, np_tiles=np_tiles),
        out_shape=jax.ShapeDtypeStruct((n, d), F32),
        grid=(n // tm,),
        in_specs=_two_part_specs(tm, ds, np_tiles, 1) + [
                  pl.BlockSpec((tm, d), lambda i: (i, 0)),
                  pl.BlockSpec((tm, d), lambda i: (i, 0)),
                  pl.BlockSpec((tm, d), lambda i: (i, 1)),
                  ] + _two_part_specs(tm, d, np_tiles, 1) + [
                  pl.BlockSpec((nsub, 6, d), lambda i: (i, 0, 0)),
                  pl.BlockSpec((ds, d), lambda i: (0, 0)),
                  pl.BlockSpec((d, d), lambda i: (0, 0)),
                  pl.BlockSpec((1, d), lambda i: (0, 0)),
                  pl.BlockSpec((1, d), lambda i: (0, 0))],
        out_specs=pl.BlockSpec((tm, d), lambda i: (i, 0)),
        compiler_params=_params(("arbitrary",)),
        name="post_mix",
    )(attn_p, attn_s, br_ssm, gates, gates, x_p, x_s, ada_blk, w_fox16, w_out16, ln_g, ln_b)


def _pack_halves(x):
    c = x.shape[1] // 2
    return pltpu.pack_elementwise([x[:, :c], x[:, c:]], packed_dtype=BF16)


def _unpack_halves(w):
    return tuple(pltpu.unpack_elementwise(w, index=i, packed_dtype=BF16, unpacked_dtype=F32)
                 for i in range(2))


def _store_token_tiles(ref, words):
    m, c = words.shape
    nsl = c // LANES
    for s in range(nsl):
        ref[pl.ds(s, m, stride=nsl), :] = words[:, s * LANES:(s + 1) * LANES]


def _load_token_tiles(ref, m):
    nsl = ref.shape[0] // m
    return jnp.concatenate([ref[pl.ds(s, m, stride=nsl), :] for s in range(nsl)], axis=1)


def _token_rows(t, nsl):
    start = t * nsl
    if nsl % SUBLANES == 0:
        start = pl.multiple_of(start, SUBLANES)
    return pl.ds(start, nsl)


def _router_kernel(x_ref, ada_ref, wr_ref, rb_ref, hp_ref, e_ref, w_ref, h_scr, *, nsub):
    ng = N_EXPERT_GROUPS
    for s in range(nsub):
        rows = slice(s * ADA_BLOCK, (s + 1) * ADA_BLOCK)
        h_scr[rows, :] = x_ref[rows, :] * (1.0 + ada_ref[s, 4:5, :]) + ada_ref[s, 3:4, :]
    _store_token_tiles(hp_ref, _pack_halves(h_scr[...]))
    h_hi, h_lo = _split_bf16(h_scr[...])
    w_hi, w_lo = _split_bf16(wr_ref[...])
    logits = (_dot_nt(w_hi, h_lo) + _dot_nt(w_lo, h_hi)) + _dot_nt(w_hi, h_hi)
    scores = jax.nn.sigmoid(logits)
    sel = scores + rb_ref[...]
    gsz = sel.shape[0] // ng
    tm = sel.shape[1]
    xs = [sel[j * ng:(j + 1) * ng, :] for j in range(gsz)]
    sc = [scores[j * ng:(j + 1) * ng, :] for j in range(gsz)]
    neg = -jnp.inf

    def lmax(v):
        out = v[0]
        for a in v[1:]:
            out = jnp.maximum(out, a)
        return out

    def lmin(v):
        out = v[0]
        for a in v[1:]:
            out = jnp.minimum(out, a)
        return out

    m1 = lmax(xs)
    i1 = lmin([jnp.where(xs[j] == m1, j, gsz) for j in range(gsz)])
    m2 = lmax([jnp.where(i1 == j, neg, xs[j]) for j in range(gsz)])
    cur = m1 + m2
    giota = lax.broadcasted_iota(I32, (ng, tm), 0)
    gsel = jnp.zeros((ng, tm), F32)
    for _ in range(TOPK_EXPERT_GROUPS):
        m = jnp.max(cur, axis=0, keepdims=True)
        gi = jnp.min(jnp.where(cur == m, giota, ng), axis=0, keepdims=True)
        hit = giota == gi
        gsel = jnp.where(hit, 1.0, gsel)
        cur = jnp.where(hit, neg, cur)
    gmask = gsel > 0.0
    xs = [jnp.where(gmask, x, neg) for x in xs]
    eid = [giota * gsz + j for j in range(gsz)]
    n_e = ng * gsz
    vals = []
    for r in range(TOP_K):
        m = jnp.max(lmax(xs), axis=0, keepdims=True)
        ci = jnp.min(lmin([jnp.where(xs[j] == m, eid[j], n_e) for j in range(gsz)]),
                     axis=0, keepdims=True)
        hits = [eid[j] == ci for j in range(gsz)]
        v = sum(jnp.where(hits[j], sc[j], 0.0) for j in range(gsz))
        vals.append(jnp.sum(v, axis=0, keepdims=True))
        xs = [jnp.where(hits[j], neg, xs[j]) for j in range(gsz)]
        e_ref[r:r + 1, :] = ci
    tot = sum(vals)
    for r in range(TOP_K):
        w_ref[r:r + 1, :] = vals[r] / tot * ROUTED_SCALE


def _router(x1, ada_blk, wr_perm, rb_perm, tm=512):
    n, d = x1.shape
    e = wr_perm.shape[0]
    nsub = tm // ADA_BLOCK
    return pl.pallas_call(
        functools.partial(_router_kernel, nsub=nsub),
        out_shape=(jax.ShapeDtypeStruct((n * (d // 2 // LANES), LANES), jnp.uint32),
                   jax.ShapeDtypeStruct((TOP_K, n), I32),
                   jax.ShapeDtypeStruct((TOP_K, n), F32)),
        grid=(n // tm,),
        in_specs=[pl.BlockSpec((tm, d), lambda i: (i, 0)),
                  pl.BlockSpec((nsub, 6, d), lambda i: (i, 0, 0)),
                  pl.BlockSpec((e, d), lambda i: (0, 0)),
                  pl.BlockSpec((e, 1), lambda i: (0, 0))],
        out_specs=(pl.BlockSpec((tm * (d // 2 // LANES), LANES), lambda i: (i, 0)),
                   pl.BlockSpec((TOP_K, tm), lambda i: (0, i)),
                   pl.BlockSpec((TOP_K, tm), lambda i: (0, i))),
        scratch_shapes=[pltpu.VMEM((tm, d), F32)],
        compiler_params=_params(("arbitrary",)),
        name="router",
    )(x1, ada_blk, wr_perm, rb_perm)


def _plan_kernel(e_ref, pos_ref, st_ref, vis_ref, rank_scr, *, n_e, blk, tm):
    n = e_ref.shape[1]
    nblk = n // blk
    row = lax.broadcasted_iota(I32, (blk, blk), 0)
    col = lax.broadcasted_iota(I32, (blk, blk), 1)
    upper = jnp.where(row <= col, 1.0, 0.0).astype(BF16)
    eid = lax.broadcasted_iota(I32, (n_e, blk), 0)

    def count_body(cb, carry):
        cols = pl.ds(pl.multiple_of(cb * blk, blk), blk)
        e_blk = e_ref[:, cols]
        hit = jnp.zeros((n_e, blk), F32)
        for k in range(TOP_K):
            hit = hit + jnp.where(e_blk[k:k + 1, :] == eid, 1.0, 0.0)
        cs = _dot(hit.astype(BF16), upper) + carry
        rank_scr[:, cols] = cs - hit
        return cs[:, blk - 1:blk]

    counts = lax.fori_loop(0, nblk, count_body, jnp.zeros((n_e, 1), F32))

    hi = jnp.floor(counts * (1.0 / 128.0))
    lo = counts - hi * 128.0
    er = lax.broadcasted_iota(I32, (n_e, n_e), 0)
    ec = lax.broadcasted_iota(I32, (n_e, n_e), 1)
    lower = jnp.where(ec < er, 1.0, 0.0).astype(BF16)
    lower_incl = jnp.where(ec <= er, 1.0, 0.0).astype(BF16)
    wide = lambda v: jnp.broadcast_to(v, (n_e, LANES)).astype(BF16)
    starts = (_dot(lower, wide(hi)) * 128.0 + _dot(lower, wide(lo)))[:, :1]
    st_ref[...] = jnp.broadcast_to(starts, st_ref.shape).astype(I32)

    inv_tm = 1.0 / tm
    nonempty = counts > 0.0
    first_t = jnp.floor(starts * inv_tm)
    nvis = jnp.where(nonempty, jnp.floor((starts + counts - 1.0) * inv_tm) - first_t + 1.0, 0.0)
    vend = _dot(lower_incl, wide(nvis))[:, :1]
    gidx = _dot(lower_incl, wide(jnp.where(nonempty, 1.0, 0.0)))[:, :1] - 1.0
    total = jnp.max(vend, axis=0, keepdims=True)
    nv = vis_ref.shape[1]
    viota = lax.broadcasted_iota(I32, (1, nv), 1).astype(F32)
    vc = jnp.minimum(viota, total - 1.0)
    e_v = jnp.sum(jnp.where(vend <= vc, 1.0, 0.0), axis=0, keepdims=True)
    eio = lax.broadcasted_iota(I32, (n_e, nv), 0).astype(F32)
    mine = eio == e_v
    pick = lambda colv: jnp.sum(jnp.where(mine, colv, 0.0), axis=0, keepdims=True)
    tile_v = pick(first_t) + (vc - pick(vend - nvis))
    g_v = pick(gidx)
    slot_v = g_v - 2.0 * jnp.floor(g_v * 0.5)
    is_next = nonempty & (gidx == g_v + 1.0)
    nxt_v = (jnp.sum(jnp.where(is_next, eio + 1.0, 0.0), axis=0, keepdims=True) - 1.0)
    rows = [tile_v, e_v, jnp.where(viota < total, 1.0, 0.0), slot_v, nxt_v]
    rows += [jnp.zeros((1, nv), F32)] * (vis_ref.shape[0] - len(rows))
    vis_ref[...] = jnp.concatenate(rows, axis=0).astype(I32)

    def pos_body(cb, c):
        cols = pl.ds(pl.multiple_of(cb * blk, blk), blk)
        e_blk = e_ref[:, cols]
        val = rank_scr[:, cols] + starts
        for k in range(TOP_K):
            p = jnp.sum(jnp.where(e_blk[k:k + 1, :] == eid, val, 0.0), axis=0, keepdims=True)
            pos_ref[k:k + 1, cols] = p.astype(I32)
        return c

    lax.fori_loop(0, nblk, pos_body, 0)


def _plan(eidx_t, n_e, tm, blk=256):
    k, n = eidx_t.shape
    n_visits = (n * k) // tm + n_e - 1
    nv = -(-n_visits // LANES) * LANES
    pos, starts, vis = pl.pallas_call(
        functools.partial(_plan_kernel, n_e=n_e, blk=blk, tm=tm),
        out_shape=(jax.ShapeDtypeStruct((k, n), I32), jax.ShapeDtypeStruct((n_e, LANES), I32),
                   jax.ShapeDtypeStruct((SUBLANES, nv), I32)),
        scratch_shapes=[pltpu.VMEM((n_e, n), F32)],
        compiler_params=pltpu.CompilerParams(vmem_limit_bytes=VMEM_LIMIT_BYTES),
        name="moe_plan",
    )(eidx_t)
    return pos, starts, vis[:, :n_visits]


def _dispatch_kernel(pos_ref, h_ref, xs_ref, sem, *, tm):
    i = pl.program_id(0)
    nsl = h_ref.shape[0] // tm

    def issue(r, c):
        base = (i * tm + r) * TOP_K
        src = h_ref.at[_token_rows(r, nsl), :]
        for k in range(TOP_K):
            p = pos_ref[base + k]
            pltpu.make_async_copy(src, xs_ref.at[_token_rows(p, nsl), :],
                                  sem).start(priority=k % 2)
        return c

    lax.fori_loop(0, tm, issue, 0)
    for k in range(TOP_K):
        pltpu.make_async_copy(h_ref, xs_ref.at[pl.ds(0, tm * nsl), :], sem).wait()


def _dispatch(pos_flat, h2t, n, tm=256):
    rows, lanes = h2t.shape
    nsl = rows // n
    return pl.pallas_call(
        functools.partial(_dispatch_kernel, tm=tm),
        out_shape=jax.ShapeDtypeStruct((rows * TOP_K, lanes), h2t.dtype),
        grid_spec=pltpu.PrefetchScalarGridSpec(
            num_scalar_prefetch=1,
            grid=(n // tm,),
            in_specs=[pl.BlockSpec((tm * nsl, lanes), lambda i, pos: (i, 0))],
            out_specs=pl.BlockSpec(memory_space=pl.ANY),
            scratch_shapes=[pltpu.SemaphoreType.DMA(())]),
        compiler_params=_params(("arbitrary",)),
        name="moe_dispatch",
    )(pos_flat, h2t)


def _gmm_kernel(vt_ref, ve_ref, vv_ref, vs_ref, nx_ref, bd_ref,
                xs_ref, wg_hbm, wu_hbm, wd_hbm, ys_ref,
                wg32, wu32, wd32, wg16, wu16, wd16, acc, sem, *, tm):
    v = pl.program_id(0)
    t = vt_ref[v]
    e = ve_ref[v]
    slot = vs_ref[v]
    pv = jnp.maximum(v - 1, 0)
    first = v == 0
    valid = vv_ref[v] == 1
    r0 = t * tm
    lo = bd_ref[e]
    hi = bd_ref[e + 1]
    whole = (r0 >= lo) & (r0 + tm <= hi)

    def weight_copies(expert, s):
        return (pltpu.make_async_copy(wg_hbm.at[expert], wg32.at[s], sem.at[s, 0]),
                pltpu.make_async_copy(wu_hbm.at[expert], wu32.at[s], sem.at[s, 1]),
                pltpu.make_async_copy(wd_hbm.at[expert], wd32.at[s], sem.at[s, 2]))

    @pl.when(first)
    def _():
        for c in weight_copies(e, slot):
            c.start()

    @pl.when(first | (e != ve_ref[pv]))
    def _():
        for c in weight_copies(e, slot):
            c.wait()
        nxt = nx_ref[v]

        @pl.when(nxt >= 0)
        def _():
            for c in weight_copies(nxt, 1 - slot):
                c.start()

        wg16[...] = wg32[slot].astype(BF16)
        wu16[...] = wu32[slot].astype(BF16)
        wd16[...] = wd32[slot].astype(BF16)

    @pl.when(valid & jnp.logical_not(whole) & (first | (t != vt_ref[pv])))
    def _():
        acc[...] = jnp.zeros(acc.shape, F32)

    @pl.when(valid)
    def _():
        x = jnp.concatenate(_unpack_halves(_load_token_tiles(xs_ref, tm)), axis=1).astype(BF16)
        g = _dot(x, wg16[...])
        u = _dot(x, wu16[...])
        act = (g * jax.nn.sigmoid(g) * u).astype(BF16)
        y = _dot(act, wd16[...])

        @pl.when(whole)
        def _():
            _store_token_tiles(ys_ref, _pack_halves(y))

        @pl.when(jnp.logical_not(whole))
        def _():
            row = r0 + lax.broadcasted_iota(I32, (tm, 1), 0)
            acc[...] += jnp.where((row >= lo) & (row < hi), y, 0.0)
            _store_token_tiles(ys_ref, _pack_halves(acc[...]))


def _gmm(tile, expert, valid, slot, nxt, bounds, xs, w_gate, w_up, w_down, tm=256):
    n_e, d, f = w_gate.shape
    nsl = d // 2 // LANES
    n_visits = tile.shape[0]
    row_map = lambda v, vt, ve, vv, vs, nx, bd: (vt[v], 0)
    return pl.pallas_call(
        functools.partial(_gmm_kernel, tm=tm),
        out_shape=jax.ShapeDtypeStruct(xs.shape, xs.dtype),
        grid_spec=pltpu.PrefetchScalarGridSpec(
            num_scalar_prefetch=6,
            grid=(n_visits,),
            in_specs=[pl.BlockSpec((tm * nsl, LANES), row_map),
                      pl.BlockSpec(memory_space=pl.ANY),
                      pl.BlockSpec(memory_space=pl.ANY),
                      pl.BlockSpec(memory_space=pl.ANY)],
            out_specs=pl.BlockSpec((tm * nsl, LANES), row_map),
            scratch_shapes=[pltpu.VMEM((2, d, f), F32), pltpu.VMEM((2, d, f), F32),
                            pltpu.VMEM((2, f, d), F32),
                            pltpu.VMEM((d, f), BF16), pltpu.VMEM((d, f), BF16),
                            pltpu.VMEM((f, d), BF16),
                            pltpu.VMEM((tm, d), F32),
                            pltpu.SemaphoreType.DMA((2, 3))]),
        compiler_params=_params(("arbitrary",)),
        name="moe_experts",
    )(tile, expert, valid, slot, nxt, bounds, xs, w_gate, w_up, w_down)


def _combine_kernel(pos_ref, ys_ref, w_ref, hp_ref, x_ref, ada_ref, wsg_ref, wsu_ref, wsd_ref,
                    lg_ref, lb_ref, op_ref, os_ref, buf, sem, *, tm, nsub, alpha, np_tiles):
    i = pl.program_id(0)
    nsl = hp_ref.shape[0] // tm

    def issue(r, c):
        base = (i * tm + r) * TOP_K
        dst_rows = _token_rows(r, nsl)
        for k in range(TOP_K):
            p = pos_ref[base + k]
            pltpu.make_async_copy(ys_ref.at[_token_rows(p, nsl), :], buf.at[k, dst_rows, :],
                                  sem).start(priority=k % 2)
        return c

    lax.fori_loop(0, tm, issue, 0)

    h16 = jnp.concatenate(_unpack_halves(_load_token_tiles(hp_ref, tm)), axis=1).astype(BF16)
    g = _dot(h16, wsg_ref[...])
    u = _dot(h16, wsu_ref[...])
    shared = _dot((g * jax.nn.sigmoid(g) * u).astype(BF16), wsd_ref[...])

    for k in range(TOP_K):
        pltpu.make_async_copy(ys_ref.at[pl.ds(0, tm * nsl), :], buf.at[k], sem).wait()
    lo = hi = None
    for k in range(TOP_K):
        a, b = _unpack_halves(_load_token_tiles(buf.at[k], tm))
        wk = w_ref[:, k:k + 1]
        lo = wk * a if lo is None else lo + wk * a
        hi = wk * b if hi is None else hi + wk * b
    ffn = shared + jnp.concatenate([lo, hi], axis=1)

    def finish(o_ref):
        for s in range(nsub):
            rows = slice(s * ADA_BLOCK, (s + 1) * ADA_BLOCK)
            g2 = ada_ref[s, 5:6, :]
            y = alpha * x_ref[rows, :] + (1.0 + g2) * ffn[rows, :]
            o_ref[rows, :] = _layer_norm(y, lg_ref[...], lb_ref[...])

    @pl.when(i < np_tiles)
    def _():
        finish(op_ref)

    @pl.when(i >= np_tiles)
    def _():
        finish(os_ref)


def _combine(pos_flat, ys, w_tok, h2p, x1, ada_blk, wsg16, wsu16, wsd16, ln_g, ln_b, *,
             alpha, n_p, tm=256):
    n, d = x1.shape
    nsl = d // 2 // LANES
    f = wsg16.shape[1]
    nsub = tm // ADA_BLOCK
    np_tiles = n_p // tm
    return pl.pallas_call(
        functools.partial(_combine_kernel, tm=tm, nsub=nsub, alpha=alpha, np_tiles=np_tiles),
        out_shape=(jax.ShapeDtypeStruct((n_p, d), F32), jax.ShapeDtypeStruct((n - n_p, d), F32)),
        grid_spec=pltpu.PrefetchScalarGridSpec(
            num_scalar_prefetch=1,
            grid=(n // tm,),
            in_specs=[pl.BlockSpec(memory_space=pl.ANY),
                      pl.BlockSpec((tm, TOP_K), lambda i, pos: (i, 0)),
                      pl.BlockSpec((tm * nsl, LANES), lambda i, pos: (i, 0)),
                      pl.BlockSpec((tm, d), lambda i, pos: (i, 0)),
                      pl.BlockSpec((nsub, 6, d), lambda i, pos: (i, 0, 0)),
                      pl.BlockSpec((d, f), lambda i, pos: (0, 0)),
                      pl.BlockSpec((d, f), lambda i, pos: (0, 0)),
                      pl.BlockSpec((f, d), lambda i, pos: (0, 0)),
                      pl.BlockSpec((1, d), lambda i, pos: (0, 0)),
                      pl.BlockSpec((1, d), lambda i, pos: (0, 0))],
            out_specs=(pl.BlockSpec((tm, d), lambda i, pos: (jnp.minimum(i, np_tiles - 1), 0)),
                       pl.BlockSpec((tm, d), lambda i, pos: (jnp.maximum(i - np_tiles, 0), 0))),
            scratch_shapes=[pltpu.VMEM((TOP_K, tm * nsl, LANES), ys.dtype),
                            pltpu.SemaphoreType.DMA(())]),
        compiler_params=_params(("arbitrary",)),
        name="moe_combine",
    )(pos_flat, ys, w_tok, h2p, x1, ada_blk, wsg16, wsu16, wsd16, ln_g, ln_b)


def _layer(x_p, x_s, c_pad, blk_batch, cache_k, cache_v, layer, past_logf, st_re, st_im, lp, dims):
    bp, tp, bs, ts, d = dims
    n_p, n_s = bp * tp, bs * ts
    n = n_p + n_s
    ds = d // 2
    n_h = ds // FOX_HEAD_DIM
    n_hp = n_h // 2
    g_n, p_n = lp["ssm_lambda_re"].shape
    n_e = lp["w_router"].shape[1]
    depth_alpha = lp["alpha"]

    ada = _ada(c_pad, lp["w_ada"], lp["b_ada"])
    ada_blk = ada.reshape(ada.shape[0], 6, d)[blk_batch]

    w_in = lp["w_in"]
    w_main = jnp.concatenate([w_in[:, :4 * ds], w_in[:, 4 * ds + n_h:]], axis=1).astype(BF16)
    wf_t = w_in[:, 4 * ds:4 * ds + n_h].T.astype(BF16)
    zu, kv32, gates, qkv16, logf_t = _inproj(x_p, x_s, ada_blk, w_main, wf_t,
                                             lp["b_f"].reshape(n_h, 1))

    lbr, lbi, bbr, bbi = _ssm_disc(lp["ssm_lambda_re"], lp["ssm_lambda_im"], lp["ssm_log_dt"],
                                   lp["ssm_b_re"], lp["ssm_b_im"])
    lam_lay = _to_state_layout(lbr[None], lbi[None])
    bd = jnp.concatenate([_block_diag_slabs(bbr.transpose(0, 2, 1)),
                          _block_diag_slabs(bbi.transpose(0, 2, 1))], axis=2).astype(BF16)
    cd = jnp.stack([_block_diag_slabs(lp["ssm_c_re"].transpose(0, 2, 1)),
                    _block_diag_slabs(lp["ssm_c_im"].transpose(0, 2, 1))], axis=1).astype(BF16)
    d_row = lp["ssm_d"].reshape(1, ds)
    x0_p = jnp.zeros((bp, 2 * g_n * p_n), F32)
    x0_s = _to_state_layout(st_re, st_im)
    g_p, xl_p = _ssm(zu, x0_p, lam_lay, bd, cd, d_row, row0=0, n_seq=bp, seq_len=tp,
                     s_blk=bp, tt=min(128, tp))
    g_s, xl_s = _ssm(zu, x0_s, lam_lay, bd, cd, d_row, row0=n_p, n_seq=bs, seq_len=ts,
                     s_blk=min(8, bs), tt=ts)
    g16 = jnp.concatenate(g_p + g_s, axis=0)
    br_ssm = _glu(g16, lp["w_glu"].astype(BF16))

    lf_p = logf_t[:, :n_p].reshape(n_h, bp, tp).transpose(1, 0, 2)
    lf_s = logf_t[:, n_p:].reshape(n_h, bs, ts).transpose(1, 0, 2)
    ck_p = _cumsum_lanes(lf_p.reshape(bp * n_h, tp)).reshape(bp, n_hp, 2, tp)
    past = past_logf.shape[1]
    cat = jnp.concatenate([past_logf.astype(F32).transpose(0, 2, 1), lf_s], axis=2)
    width = -(-(past + ts) // 256) * 256
    cat = jnp.pad(cat, ((0, 0), (0, 0), (0, width - past - ts)))
    ck_s = _cumsum_lanes(cat.reshape(bs * n_h, width)).reshape(bs, n_h, width)
    attn_p = _attn_prompt(qkv16, ck_p, n_batch=bp, seq_len=tp, n_hp=n_hp, tq=min(512, tp))
    attn_s = _attn_sample(qkv16, cache_k, cache_v, ck_s[..., :past], ck_s[..., past:past + ts],
                          layer=layer, row0=n_p, n_batch=bs, seq_len=ts, n_h=n_h)

    x1 = _post(attn_p, attn_s, br_ssm, gates, x_p, x_s, ada_blk, lp["w_fox_o"].astype(BF16),
               lp["w_out"].astype(BF16), lp["ln1_g"].reshape(1, d), lp["ln1_b"].reshape(1, d),
               alpha=depth_alpha)

    gsz = n_e // N_EXPERT_GROUPS
    perm = lambda a: a.reshape(N_EXPERT_GROUPS, gsz, -1).transpose(1, 0, 2).reshape(n_e, -1)
    h2p, eidx_t, w_t = _router(x1, ada_blk, perm(lp["w_router"].T), perm(lp["router_bias"].reshape(n_e, 1)))
    tm_e = 256
    pos_t, starts, vis = _plan(eidx_t, n_e, tm_e)
    pos_flat = pos_t.T.reshape(n * TOP_K)
    bounds = jnp.concatenate([starts[:, 0], jnp.full((1,), n * TOP_K, I32)])
    xs = _dispatch(pos_flat, h2p, n)
    ys = _gmm(vis[0], vis[1], vis[2], vis[3], vis[4], bounds, xs, lp["w_exp_gate"], lp["w_exp_up"],
              lp["w_exp_down"], tm=tm_e)
    x2_p, x2_s = _combine(pos_flat, ys, w_t.T, h2p, x1, ada_blk, lp["w_sh_gate"].astype(BF16),
                          lp["w_sh_up"].astype(BF16), lp["w_sh_down"].astype(BF16),
                          lp["ln2_g"].reshape(1, d), lp["ln2_b"].reshape(1, d),
                          alpha=depth_alpha, n_p=n_p)

    k_new, v_new = kv32[0], kv32[1]
    logf = logf_t.T
    ssm_p = _from_state_layout(xl_p, g_n, p_n)
    ssm_s = _from_state_layout(xl_s, g_n, p_n)
    return x2_p, x2_s, k_new, v_new, logf, ssm_p, ssm_s


def kernel(x_prompt, x_sample, cache_k, cache_v, cache_logf, state_ssm_re, state_ssm_im, c_prompt, c_sample, w_ada, b_ada, w_in, b_f, ssm_lambda_re, ssm_lambda_im, ssm_log_dt, ssm_b_re, ssm_b_im, ssm_c_re, ssm_c_im, ssm_d, w_glu, w_fox_o, w_out, ln1_g, ln1_b, w_router, router_bias, w_exp_gate, w_exp_up, w_exp_down, w_sh_gate, w_sh_up, w_sh_down, ln2_g, ln2_b):
    bp, tp, d = x_prompt.shape
    bs, ts, _ = x_sample.shape
    depth = w_ada.shape[0]
    n_p, n_s = bp * tp, bs * ts
    n_h = d // 2 // FOX_HEAD_DIM
    assert tp % ADA_BLOCK == 0 and ts == ADA_BLOCK
    alpha = (2.0 * depth) ** 0.25

    x_p, x_s = x_prompt.reshape(n_p, d), x_sample.reshape(n_s, d)
    c_all = jnp.concatenate([c_prompt, c_sample], axis=0)
    c_pad = jnp.pad(c_all, ((0, -(bp + bs) % 16), (0, 0)))
    blk_batch = np.concatenate([np.repeat(np.arange(bp), tp // ADA_BLOCK),
                                bp + np.repeat(np.arange(bs), ts // ADA_BLOCK)])
    dims = (bp, tp, bs, ts, d)
    outs_p, outs_s = [], []
    for l in range(depth):
        lp = dict(w_ada=w_ada[l], b_ada=b_ada[l], w_in=w_in[l], b_f=b_f[l],
                  ssm_lambda_re=ssm_lambda_re[l], ssm_lambda_im=ssm_lambda_im[l],
                  ssm_log_dt=ssm_log_dt[l], ssm_b_re=ssm_b_re[l], ssm_b_im=ssm_b_im[l],
                  ssm_c_re=ssm_c_re[l], ssm_c_im=ssm_c_im[l], ssm_d=ssm_d[l], w_glu=w_glu[l],
                  w_fox_o=w_fox_o[l], w_out=w_out[l], ln1_g=ln1_g[l], ln1_b=ln1_b[l],
                  w_router=w_router[l], router_bias=router_bias[l], w_exp_gate=w_exp_gate[l],
                  w_exp_up=w_exp_up[l], w_exp_down=w_exp_down[l], w_sh_gate=w_sh_gate[l],
                  w_sh_up=w_sh_up[l], w_sh_down=w_sh_down[l], ln2_g=ln2_g[l], ln2_b=ln2_b[l],
                  alpha=alpha)
        x_p, x_s, k_new, v_new, logf, ssm_p, ssm_s = _layer(
            x_p, x_s, c_pad, blk_batch, cache_k, cache_v, l, cache_logf[l],
            state_ssm_re[l].astype(F32), state_ssm_im[l].astype(F32), lp, dims)
        hd = FOX_HEAD_DIM
        outs_p.append((k_new[:n_p].reshape(bp, tp, n_h, hd), v_new[:n_p].reshape(bp, tp, n_h, hd),
                       logf[:n_p].reshape(bp, tp, n_h), ssm_p[0], ssm_p[1]))
        outs_s.append((k_new[n_p:].reshape(bs, ts, n_h, hd), v_new[n_p:].reshape(bs, ts, n_h, hd),
                       logf[n_p:].reshape(bs, ts, n_h), ssm_s[0], ssm_s[1]))
    stack = lambda outs, i: jnp.stack([o[i] for o in outs])
    return (x_p.reshape(bp, tp, d), x_s.reshape(bs, ts, d),
            stack(outs_p, 0), stack(outs_p, 1), stack(outs_p, 2), stack(outs_p, 3), stack(outs_p, 4),
            stack(outs_s, 0), stack(outs_s, 1), stack(outs_s, 2), stack(outs_s, 3), stack(outs_s, 4))
```

```python
import functools

import jax
import jax.numpy as jnp
import numpy as np
from jax import lax
from jax.experimental import pallas as pl
from jax.experimental.pallas import tpu as pltpu

F32 = jnp.float32
BF16 = jnp.bfloat16
I32 = jnp.int32

V7X_VMEM_BYTES = 64 * 1024 * 1024
VMEM_LIMIT_BYTES = V7X_VMEM_BYTES - 8 * 1024 * 1024
LANES = 128
SUBLANES = 8

SSM_GROUP_WIDTH = 16
SSM_STATE = 64
SSM_GROUPS_PER_SLAB = 8
FOX_HEAD_DIM = 64
N_EXPERT_GROUPS = 8
TOPK_EXPERT_GROUPS = 4
TOP_K = 8
ROUTED_SCALE = 2.5
LN_EPS = 1e-5
ADA_BLOCK = 64

NT_DIMS = (((1,), (1,)), ((), ()))


def _params(sem, vmem=VMEM_LIMIT_BYTES):
    return pltpu.CompilerParams(dimension_semantics=sem, vmem_limit_bytes=vmem)


def _dot(a, b):
    return jnp.dot(a, b, preferred_element_type=F32)


def _dot_nt(a, b):
    return lax.dot_general(a, b, NT_DIMS, preferred_element_type=F32)


def _split_bf16(x):
    hi = x.astype(BF16)
    lo = (x - hi.astype(F32)).astype(BF16)
    return hi, lo


def _log_sigmoid(x):
    return jnp.minimum(x, 0.0) - jnp.log1p(jnp.exp(-jnp.abs(x)))


def _gelu_tanh(x):
    c = np.float32(np.sqrt(2.0 / np.pi))
    return x * (0.5 * (1.0 + jnp.tanh(c * (x + 0.044715 * (x * x * x)))))


def _layer_norm(y, g, b):
    mu = jnp.mean(y, axis=-1, keepdims=True)
    yc = y - mu
    var = jnp.mean(yc * yc, axis=-1, keepdims=True)
    return yc * lax.rsqrt(var + LN_EPS) * g + b


def _ada_kernel(c_ref, w_ref, b_ref, o_ref):
    c = c_ref[...]
    a_hi, a_lo = _split_bf16(c * jax.nn.sigmoid(c))
    w_hi, w_lo = _split_bf16(w_ref[...])
    acc = _dot(a_hi, w_lo) + _dot(a_lo, w_hi)
    o_ref[...] = acc + _dot(a_hi, w_hi) + b_ref[...]


def _ada(c_pad, w_ada, b_ada):
    m, d = c_pad.shape
    n = w_ada.shape[1]
    tn = 1024
    return pl.pallas_call(
        _ada_kernel,
        out_shape=jax.ShapeDtypeStruct((m, n), F32),
        grid=(n // tn,),
        in_specs=[pl.BlockSpec((m, d), lambda j: (0, 0)),
                  pl.BlockSpec((d, tn), lambda j: (0, j)),
                  pl.BlockSpec((1, tn), lambda j: (0, j))],
        out_specs=pl.BlockSpec((m, tn), lambda j: (0, j)),
        compiler_params=_params(("arbitrary",)),
        name="ada",
    )(c_pad, w_ada, b_ada.reshape(1, n))


def _inproj_kernel(xp_ref, xs_ref, ada_ref, w_ref, wf_ref, bf_ref,
                   u_ref, kvp_ref, kvs_ref, g_ref, qkv_ref, lf_ref, h_scr,
                   *, nsub, q_scale, np_tiles):
    i = pl.program_id(0)
    j = pl.program_id(1)

    def modulate(x_ref):
        for s in range(nsub):
            rows = slice(s * ADA_BLOCK, (s + 1) * ADA_BLOCK)
            sh = ada_ref[s, 0:1, :]
            sc = ada_ref[s, 1:2, :]
            h_scr[rows, :] = (x_ref[rows, :] * (1.0 + sc) + sh).astype(BF16)

    @pl.when((j == 0) & (i < np_tiles))
    def _():
        modulate(xp_ref)

    @pl.when((j == 0) & (i >= np_tiles))
    def _():
        modulate(xs_ref)

    @pl.when(j == 0)
    def _():
        f = _dot_nt(wf_ref[...], h_scr[...])
        lf_ref[...] = _log_sigmoid(f + bf_ref[...])

    zt = _dot(h_scr[...], w_ref[...])

    @pl.when(j == 0)
    def _():
        u_ref[...] = zt

    @pl.when(j == 1)
    def _():
        qkv_ref[...] = (zt * q_scale).astype(BF16)

    @pl.when((j == 2) | (j == 3))
    def _():
        qkv_ref[...] = zt.astype(BF16)

    @pl.when(((j == 2) | (j == 3)) & (i < np_tiles))
    def _():
        kvp_ref[0] = zt

    @pl.when(((j == 2) | (j == 3)) & (i >= np_tiles))
    def _():
        kvs_ref[0] = zt

    @pl.when(j >= 4)
    def _():
        g_ref[...] = zt


def _two_part_specs(tm, d, np_tiles, n_grid_args):
    if n_grid_args == 1:
        return [pl.BlockSpec((tm, d), lambda i: (jnp.minimum(i, np_tiles - 1), 0)),
                pl.BlockSpec((tm, d), lambda i: (jnp.maximum(i - np_tiles, 0), 0))]
    return [pl.BlockSpec((tm, d), lambda i, j: (jnp.minimum(i, np_tiles - 1), 0)),
            pl.BlockSpec((tm, d), lambda i, j: (jnp.maximum(i - np_tiles, 0), 0))]


def _inproj(x_p, x_s, ada_blk, w_main, wf_t, bf_col, tm=512):
    n_p, d = x_p.shape
    n = n_p + x_s.shape[0]
    ds = d // 2
    h = wf_t.shape[0]
    nsub = tm // ADA_BLOCK
    ncol = w_main.shape[1] // ds
    np_tiles = n_p // tm
    kern = functools.partial(_inproj_kernel, nsub=nsub, q_scale=FOX_HEAD_DIM ** -0.5,
                             np_tiles=np_tiles)
    return pl.pallas_call(
        kern,
        out_shape=(jax.ShapeDtypeStruct((n, ds), F32),
                   jax.ShapeDtypeStruct((2, n_p, ds), F32),
                   jax.ShapeDtypeStruct((2, n - n_p, ds), F32),
                   jax.ShapeDtypeStruct((n, 2 * d), F32),
                   jax.ShapeDtypeStruct((n, 3 * ds), BF16),
                   jax.ShapeDtypeStruct((h, n), F32)),
        grid=(n // tm, ncol),
        in_specs=_two_part_specs(tm, d, np_tiles, 2) + [
                  pl.BlockSpec((nsub, 6, d), lambda i, j: (i, 0, 0)),
                  pl.BlockSpec((d, ds), lambda i, j: (0, j)),
                  pl.BlockSpec((h, d), lambda i, j: (0, 0)),
                  pl.BlockSpec((h, 1), lambda i, j: (0, 0))],
        out_specs=(pl.BlockSpec((tm, ds), lambda i, j: (i, 0)),
                   pl.BlockSpec((1, tm, ds), lambda i, j: (
                       jnp.where(i < np_tiles, jnp.clip(j - 2, 0, 1), 1),
                       jnp.minimum(i, np_tiles - 1), 0)),
                   pl.BlockSpec((1, tm, ds), lambda i, j: (
                       jnp.where(i >= np_tiles, jnp.clip(j - 2, 0, 1), 0),
                       jnp.maximum(i - np_tiles, 0), 0)),
                   pl.BlockSpec((tm, ds), lambda i, j: (i, jnp.clip(j - 4, 0, ncol - 5))),
                   pl.BlockSpec((tm, ds), lambda i, j: (i, jnp.clip(j - 1, 0, 2))),
                   pl.BlockSpec((h, tm), lambda i, j: (0, i))),
        scratch_shapes=[pltpu.VMEM((tm, d), BF16)],
        compiler_params=_params(("arbitrary", "arbitrary")),
        name="inproj",
    )(x_p, x_s, ada_blk, w_main, wf_t, bf_col)


def _ssm_disc_kernel(lr_ref, li_ref, ldt_ref, br_ref, bi_ref,
                     lbr_ref, lbi_ref, bbr_ref, bbi_ref):
    lr = jnp.minimum(lr_ref[...], -1e-4)
    li = li_ref[...]
    dt = jnp.exp(ldt_ref[...])
    er = jnp.exp(lr * dt)
    lbr = er * jnp.cos(li * dt)
    lbi = er * jnp.sin(li * dt)
    lbr_ref[...] = lbr
    lbi_ref[...] = lbi
    nr = lbr - 1.0
    den = lr * lr + li * li
    qr = (nr * lr + lbi * li) / den
    qi = (lbi * lr - nr * li) / den
    b_r = br_ref[...]
    b_i = bi_ref[...]
    bbr_ref[...] = qr * b_r - qi * b_i
    bbi_ref[...] = qr * b_i + qi * b_r


def _ssm_disc(lam_re, lam_im, log_dt, b_re, b_im):
    g, p, w = b_re.shape
    rep = lambda a: jnp.repeat(a, w, axis=1)
    shp = jax.ShapeDtypeStruct((g, p * w), F32)
    lbr, lbi, bbr, bbi = pl.pallas_call(
        _ssm_disc_kernel, out_shape=(shp, shp, shp, shp), name="ssm_disc",
    )(rep(lam_re), rep(lam_im), log_dt.reshape(g, 1),
      b_re.reshape(g, p * w), b_im.reshape(g, p * w))
    return (lbr[:, ::w], lbi[:, ::w], bbr.reshape(g, p, w), bbi.reshape(g, p, w))


def _to_state_layout(re, im):
    s, g, p = re.shape
    ns = g // SSM_GROUPS_PER_SLAB
    r = re.reshape(s, ns, 1, SSM_GROUPS_PER_SLAB * p)
    i = im.reshape(s, ns, 1, SSM_GROUPS_PER_SLAB * p)
    return jnp.concatenate([r, i], axis=2).reshape(s, 2 * g * p)


def _from_state_layout(x, g, p):
    s = x.shape[0]
    y = x.reshape(s, g // SSM_GROUPS_PER_SLAB, 2, SSM_GROUPS_PER_SLAB, p)
    return y[:, :, 0].reshape(s, g, p), y[:, :, 1].reshape(s, g, p)


def _block_diag_slabs(a):
    g, m, n = a.shape
    k = SSM_GROUPS_PER_SLAB
    a4 = a.reshape(g // k, k, m, n)
    eye = jnp.eye(k, dtype=bool)
    out = jnp.where(eye[None, :, None, :, None], a4[:, :, :, None, :], 0.0)
    return out.reshape(g // k, k * m, k * n)


def _ssm_kernel(*refs, n_refs, rpr, s_blk, tt, n_slab, sw):
    u_refs = refs[:n_refs]
    x0_ref, lam_ref, bd_ref, cd_ref, d_ref = refs[n_refs:n_refs + 5]
    o_refs = refs[n_refs + 5:2 * n_refs + 5]
    xl_ref = refs[2 * n_refs + 5]
    u_tm, bu, y_tm, st = refs[2 * n_refs + 6:]
    tb = pl.program_id(1)
    uw = SSM_GROUPS_PER_SLAB * SSM_GROUP_WIDTH
    npl = sw // LANES
    nph = npl // 2
    k_sub = SUBLANES // s_blk
    n_rows = s_blk * tt

    def seq_view(refs_, s):
        return (refs_[s], slice(None)) if n_refs == s_blk else (refs_[0], slice(s * tt, (s + 1) * tt))

    @pl.when(tb == 0)
    def _():
        st[...] = jnp.zeros(st.shape, F32)
        st[0:s_blk, :] = x0_ref[...]

    for s in range(s_blk):
        ref, rows = seq_view(u_refs, s)
        for j in range(n_slab):
            u_tm[j, pl.ds(s, tt, stride=s_blk), :] = ref[rows, j * uw:(j + 1) * uw]
    for j in range(n_slab):
        res = _dot(u_tm[j].astype(BF16), bd_ref[j])
        for q in range(npl):
            bu[j * npl + q] = res[:, q * LANES:(q + 1) * LANES]

    sub = lax.broadcasted_iota(I32, (SUBLANES, LANES), 0)
    for j in range(n_slab):
        c0 = j * sw
        a = [jnp.broadcast_to(lam_ref[:, c0 + q * LANES:c0 + (q + 1) * LANES], (SUBLANES, LANES))
             for q in range(npl)]

        def body(i, carry, j=j, a=a):
            rows = pl.ds(pl.multiple_of(i * SUBLANES, SUBLANES), SUBLANES)
            x = list(carry)
            b = [bu[j * npl + q, rows, :] for q in range(npl)]
            out = [None] * npl
            for step in range(k_sub):
                win = (sub >= step * s_blk) & (sub < (step + 1) * s_blk)
                for q in range(nph):
                    xr, xi = x[q], x[nph + q]
                    nr = a[q] * xr - a[nph + q] * xi + b[q]
                    ni = a[q] * xi + a[nph + q] * xr + b[nph + q]
                    out[q] = nr if step == 0 else jnp.where(win, nr, out[q])
                    out[nph + q] = ni if step == 0 else jnp.where(win, ni, out[nph + q])
                    if k_sub > 1:
                        nr = pltpu.roll(nr, s_blk, 0)
                        ni = pltpu.roll(ni, s_blk, 0)
                    x[q], x[nph + q] = nr, ni
            for q in range(npl):
                bu[j * npl + q, rows, :] = out[q]
            return tuple(x)

        init = tuple(st[:, c0 + q * LANES:c0 + (q + 1) * LANES] for q in range(npl))
        fin = lax.fori_loop(0, n_rows // SUBLANES, body, init, unroll=4)
        for q in range(npl):
            st[:, c0 + q * LANES:c0 + (q + 1) * LANES] = fin[q]

    for j in range(n_slab):
        xr16 = jnp.concatenate([bu[j * npl + q] for q in range(nph)], axis=1).astype(BF16)
        xi16 = jnp.concatenate([bu[j * npl + nph + q] for q in range(nph)], axis=1).astype(BF16)
        y_tm[j] = _dot(xr16, cd_ref[j, 0]) - _dot(xi16, cd_ref[j, 1])
    for s in range(s_blk):
        ref, rows = seq_view(u_refs, s)
        o_ref, o_rows = seq_view(o_refs, s)
        for j in range(n_slab):
            cols = slice(j * uw, (j + 1) * uw)
            y = y_tm[j, pl.ds(s, tt, stride=s_blk), :] + d_ref[:, cols] * ref[rows, cols]
            o_ref[o_rows, cols] = _gelu_tanh(y).astype(BF16)

    @pl.when(tb == pl.num_programs(1) - 1)
    def _():
        xl_ref[...] = st[0:s_blk, :]


def _ssm(z32, x0_lay, lam_lay, bd, cd, d_row, *, row0, n_seq, seq_len, s_blk, tt):
    n_slab = bd.shape[0]
    uw = bd.shape[1]
    sw = bd.shape[2]
    ds = n_slab * uw
    state_w = n_slab * sw
    n_tb = seq_len // tt
    n_sg = n_seq // s_blk
    if n_tb == 1:
        n_refs, rpr = 1, s_blk * tt
        assert row0 % rpr == 0
        in_maps = [lambda sg, tb: (row0 // rpr + sg, 0)]
        out_shape = [jax.ShapeDtypeStruct((n_seq * seq_len, ds), BF16)]
        out_maps = [lambda sg, tb: (sg, 0)]
    else:
        assert n_sg == 1 and row0 == 0
        n_refs, rpr = s_blk, tt
        in_maps = [functools.partial(lambda sg, tb, s: (s * n_tb + tb, 0), s=s) for s in range(s_blk)]
        out_shape = [jax.ShapeDtypeStruct((seq_len, ds), BF16)] * s_blk
        out_maps = [lambda sg, tb: (tb, 0)] * s_blk
    rows = n_refs * rpr
    kern = functools.partial(_ssm_kernel, n_refs=n_refs, rpr=rpr, s_blk=s_blk, tt=tt,
                             n_slab=n_slab, sw=sw)
    outs = pl.pallas_call(
        kern,
        out_shape=tuple(out_shape) + (jax.ShapeDtypeStruct((n_seq, state_w), F32),),
        grid=(n_sg, n_tb),
        in_specs=[pl.BlockSpec((rpr, ds), m) for m in in_maps] + [
            pl.BlockSpec((s_blk, state_w), lambda sg, tb: (sg, 0)),
            pl.BlockSpec((1, state_w), lambda sg, tb: (0, 0)),
            pl.BlockSpec(bd.shape, lambda sg, tb: (0, 0, 0)),
            pl.BlockSpec(cd.shape, lambda sg, tb: (0, 0, 0, 0)),
            pl.BlockSpec((1, ds), lambda sg, tb: (0, 0))],
        out_specs=tuple(pl.BlockSpec((rpr, ds), m) for m in out_maps) + (
            pl.BlockSpec((s_blk, state_w), lambda sg, tb: (sg, 0)),),
        scratch_shapes=[pltpu.VMEM((n_slab, rows, uw), F32),
                        pltpu.VMEM((state_w // LANES, rows, LANES), F32),
                        pltpu.VMEM((n_slab, rows, uw), F32),
                        pltpu.VMEM((SUBLANES, state_w), F32)],
        compiler_params=_params(("arbitrary", "arbitrary")),
        name="ssm",
    )(*([z32] * n_refs), x0_lay, lam_lay, bd, cd, d_row)
    return list(outs[:-1]), outs[-1]


def _cumsum_kernel(x_ref, o_ref, *, blk):
    r, t = x_ref.shape
    row = lax.broadcasted_iota(I32, (blk, blk), 0)
    col = lax.broadcasted_iota(I32, (blk, blk), 1)
    upper = jnp.where(row <= col, 1.0, 0.0).astype(BF16)
    carry = jnp.zeros((r, 1), F32)
    for c in range(t // blk):
        x = x_ref[:, c * blk:(c + 1) * blk]
        h1 = x.astype(BF16)
        r1 = x - h1.astype(F32)
        h2 = r1.astype(BF16)
        h3 = (r1 - h2.astype(F32)).astype(BF16)
        s = (_dot(h3, upper) + _dot(h2, upper)) + _dot(h1, upper) + carry
        o_ref[:, c * blk:(c + 1) * blk] = s
        carry = s[:, blk - 1:blk]


def _cumsum_lanes(x, blk=256, tr=64):
    r, t = x.shape
    tr = min(tr, r)
    return pl.pallas_call(
        functools.partial(_cumsum_kernel, blk=blk),
        out_shape=jax.ShapeDtypeStruct((r, t), F32),
        grid=(r // tr,),
        in_specs=[pl.BlockSpec((tr, t), lambda i: (i, 0))],
        out_specs=pl.BlockSpec((tr, t), lambda i: (i, 0)),
        compiler_params=_params(("arbitrary",)),
        name="cumsum",
    )(x)


def _attn_step(q, k, v, ck, m_scr, l_scr, acc_scr, mask):
    lane = lax.broadcasted_iota(I32, (1, 2 * FOX_HEAD_DIM), 1)
    lo = lane < FOX_HEAD_DIM
    pvs, alphas = [], []
    for h in range(2):
        sel = lo if h == 0 else jnp.logical_not(lo)
        qh = jnp.where(sel, q, jnp.zeros_like(q))
        s = _dot_nt(qh, k) - ck[h:h + 1, :]
        if mask is not None:
            s = jnp.where(mask, s, -jnp.inf)
        m_prev = m_scr[h]
        m_new = jnp.maximum(m_prev, jnp.max(s, axis=-1, keepdims=True))
        alpha = jnp.exp(m_prev - m_new)
        p = jnp.exp(s - m_new[:, :1])
        l_scr[h] = alpha * l_scr[h] + jnp.sum(p, axis=-1, keepdims=True)
        m_scr[h] = m_new
        pvs.append(_dot(p.astype(BF16), v))
        alphas.append(alpha)
    acc_scr[...] = (jnp.where(lo, alphas[0], alphas[1]) * acc_scr[...]
                    + jnp.where(lo, pvs[0], pvs[1]))


def _attn_init(m_scr, l_scr, acc_scr):
    m_scr[...] = jnp.full(m_scr.shape, -jnp.inf, F32)
    l_scr[...] = jnp.zeros(l_scr.shape, F32)
    acc_scr[...] = jnp.zeros(acc_scr.shape, F32)


def _attn_finish(o_ref, l_scr, acc_scr):
    lane = lax.broadcasted_iota(I32, (1, 2 * FOX_HEAD_DIM), 1)
    l = jnp.where(lane < FOX_HEAD_DIM, l_scr[0], l_scr[1])
    o_ref[...] = (acc_scr[...] / l).astype(o_ref.dtype)


def _causal_mask(tq, tk):
    return (lax.broadcasted_iota(I32, (tq, tk), 1) <= lax.broadcasted_iota(I32, (tq, tk), 0))


def _attn_prompt_kernel(qt_ref, kt_ref, q_ref, k_ref, v_ref, ck_ref, o_ref, m_scr, l_scr, acc_scr,
                        *, tq):
    step = pl.program_id(2)
    qi, ki = qt_ref[step], kt_ref[step]

    @pl.when(ki == 0)
    def _():
        _attn_init(m_scr, l_scr, acc_scr)

    @pl.when(ki < qi)
    def _():
        _attn_step(q_ref[...], k_ref[...], v_ref[...], ck_ref[0, 0], m_scr, l_scr, acc_scr, None)

    @pl.when(ki == qi)
    def _():
        _attn_step(q_ref[...], k_ref[...], v_ref[...], ck_ref[0, 0], m_scr, l_scr, acc_scr,
                   _causal_mask(tq, tq))
        _attn_finish(o_ref, l_scr, acc_scr)


def _attn_prompt(qkv16, ck, *, n_batch, seq_len, n_hp, tq=512):
    nq = seq_len // tq
    lw = 2 * FOX_HEAD_DIM
    pairs = [(qi, ki) for qi in range(nq) for ki in range(qi + 1)]
    qt = jnp.asarray([p[0] for p in pairs], I32)
    kt = jnp.asarray([p[1] for p in pairs], I32)
    q_map = lambda b, hp, s, qt, kt: (b * nq + qt[s], hp)
    kv_map = lambda c: (lambda b, hp, s, qt, kt: (b * nq + kt[s], c * n_hp + hp))
    return pl.pallas_call(
        functools.partial(_attn_prompt_kernel, tq=tq),
        out_shape=jax.ShapeDtypeStruct((n_batch * seq_len, n_hp * lw), BF16),
        grid_spec=pltpu.PrefetchScalarGridSpec(
            num_scalar_prefetch=2,
            grid=(n_batch, n_hp, len(pairs)),
            in_specs=[pl.BlockSpec((tq, lw), q_map),
                      pl.BlockSpec((tq, lw), kv_map(1)),
                      pl.BlockSpec((tq, lw), kv_map(2)),
                      pl.BlockSpec((1, 1, 2, tq), lambda b, hp, s, qt, kt: (b, hp, 0, kt[s]))],
            out_specs=pl.BlockSpec((tq, lw), q_map),
            scratch_shapes=[pltpu.VMEM((2, tq, lw), F32), pltpu.VMEM((2, tq, lw), F32),
                            pltpu.VMEM((tq, lw), F32)]),
        compiler_params=_params(("arbitrary",) * 3),
        name="attn_prompt",
    )(qt, kt, qkv16, qkv16, qkv16, ck)


def _attn_sample_kernel(q_ref, kp_ref, vp_ref, kn_ref, vn_ref, ckp_ref, ckn_ref, o_ref, *, ts, nh):
    hd = FOX_HEAD_DIM
    lw = nh * hd
    lane = lax.broadcasted_iota(I32, (1, lw), 1)
    sels = [(lane >= h * hd) & (lane < (h + 1) * hd) for h in range(nh)]
    q = q_ref[...]
    qbd = jnp.concatenate([jnp.where(sels[h], q, jnp.zeros_like(q)) for h in range(nh)], axis=0)

    def update(state, k, v, ck, mask):
        m_prev, l_prev, acc = state
        s = _dot_nt(qbd, k)
        rows = []
        for h in range(nh):
            sh = s[h * ts:(h + 1) * ts, :] - ck[h:h + 1, :]
            rows.append(sh if mask is None else jnp.where(mask, sh, -jnp.inf))
        s = jnp.concatenate(rows, axis=0)
        m_new = jnp.maximum(m_prev, jnp.max(s, axis=-1, keepdims=True))
        alpha = jnp.exp(m_prev - m_new)
        p = jnp.exp(s - m_new)
        l_new = alpha * l_prev + jnp.sum(p, axis=-1, keepdims=True)
        return m_new, l_new, alpha * acc + _dot(p.astype(BF16), v)

    state = (jnp.full((nh * ts, 1), -jnp.inf, F32), jnp.zeros((nh * ts, 1), F32),
             jnp.zeros((nh * ts, lw), F32))
    state = update(state, kp_ref[0], vp_ref[0], ckp_ref[0, 0], None)
    _, l, acc = update(state, kn_ref[...], vn_ref[...], ckn_ref[0, 0], _causal_mask(ts, ts))
    res = acc / l
    out = res[0:ts, :]
    for h in range(1, nh):
        out = jnp.where(sels[h], res[h * ts:(h + 1) * ts, :], out)
    o_ref[...] = out.astype(o_ref.dtype)


def _attn_sample(qkv16, k_past, v_past, ck_past, ck_new, *, row0, n_batch, seq_len, n_h, nh=4):
    past = k_past.shape[1]
    lw = nh * FOX_HEAD_DIM
    ng = n_h // nh
    rb0 = row0 // seq_len
    new_map = lambda c: (lambda b, g: (rb0 + b, c * ng + g))
    return pl.pallas_call(
        functools.partial(_attn_sample_kernel, ts=seq_len, nh=nh),
        out_shape=jax.ShapeDtypeStruct((n_batch * seq_len, ng * lw), BF16),
        grid=(n_batch, ng),
        in_specs=[pl.BlockSpec((seq_len, lw), new_map(0)),
                  pl.BlockSpec((1, past, lw), lambda b, g: (b, 0, g)),
                  pl.BlockSpec((1, past, lw), lambda b, g: (b, 0, g)),
                  pl.BlockSpec((seq_len, lw), new_map(1)),
                  pl.BlockSpec((seq_len, lw), new_map(2)),
                  pl.BlockSpec((1, 1, nh, past), lambda b, g: (b, g, 0, 0)),
                  pl.BlockSpec((1, 1, nh, seq_len), lambda b, g: (b, g, 0, 0))],
        out_specs=pl.BlockSpec((seq_len, lw), lambda b, g: (b, g)),
        compiler_params=_params(("arbitrary",) * 2),
        name="attn_sample",
    )(qkv16, k_past, v_past, qkv16, qkv16, ck_past, ck_new)


def _glu_kernel(g_ref, wa_ref, wb_ref, o_ref):
    g = g_ref[...]
    o_ref[...] = _dot(g, wa_ref[...]) * jax.nn.sigmoid(_dot(g, wb_ref[...]))


def _glu(g16, w_glu16, tm=512, tn=1024):
    n, ds = g16.shape
    d = w_glu16.shape[1] // 2
    tn = min(tn, d)
    nb = d // tn
    return pl.pallas_call(
        _glu_kernel,
        out_shape=jax.ShapeDtypeStruct((n, d), F32),
        grid=(n // tm, nb),
        in_specs=[pl.BlockSpec((tm, ds), lambda i, j: (i, 0)),
                  pl.BlockSpec((ds, tn), lambda i, j: (0, j)),
                  pl.BlockSpec((ds, tn), lambda i, j: (0, nb + j))],
        out_specs=pl.BlockSpec((tm, tn), lambda i, j: (i, j)),
        compiler_params=_params(("arbitrary", "arbitrary")),
        name="glu",
    )(g16, w_glu16, w_glu16)


def _post_kernel(attp_ref, atts_ref, brs_ref, gs_ref, gf_ref, xp_ref, xs_ref, ada_ref, wfo_ref,
                 wo_ref, lg_ref, lb_ref, o_ref, *, nsub, alpha, np_tiles):
    i = pl.program_id(0)
    att = jnp.where(i < np_tiles, attp_ref[...], atts_ref[...])
    br_fox = _dot(att, wfo_ref[...])
    merged = jax.nn.sigmoid(gs_ref[...]) * brs_ref[...] + jax.nn.sigmoid(gf_ref[...]) * br_fox
    mix = _dot(merged.astype(BF16), wo_ref[...])

    def finish(x_ref):
        for s in range(nsub):
            rows = slice(s * ADA_BLOCK, (s + 1) * ADA_BLOCK)
            g1 = ada_ref[s, 2:3, :]
            y = alpha * x_ref[rows, :] + (1.0 + g1) * mix[rows, :]
            o_ref[rows, :] = _layer_norm(y, lg_ref[...], lb_ref[...])

    @pl.when(i < np_tiles)
    def _():
        finish(xp_ref)

    @pl.when(i >= np_tiles)
    def _():
        finish(xs_ref)


def _post(attn_p, attn_s, br_ssm, gates, x_p, x_s, ada_blk, w_fox16, w_out16, ln_g, ln_b, *,
          alpha, tm=256):
    n_p, d = x_p.shape
    n = n_p + x_s.shape[0]
    ds = d // 2
    nsub = tm // ADA_BLOCK
    np_tiles = n_p // tm
    return pl.pallas_call(
        functools.partial(_post_kernel, nsub=nsub, alpha=alpha, np_tiles=np_tiles),
        out_shape=jax.ShapeDtypeStruct((n, d), F32),
        grid=(n // tm,),
        in_specs=_two_part_specs(tm, ds, np_tiles, 1) + [
                  pl.BlockSpec((tm, d), lambda i: (i, 0)),
                  pl.BlockSpec((tm, d), lambda i: (i, 0)),
                  pl.BlockSpec((tm, d), lambda i: (i, 1)),
                  ] + _two_part_specs(tm, d, np_tiles, 1) + [
                  pl.BlockSpec((nsub, 6, d), lambda i: (i, 0, 0)),
                  pl.BlockSpec((ds, d), lambda i: (0, 0)),
                  pl.BlockSpec((d, d), lambda i: (0, 0)),
                  pl.BlockSpec((1, d), lambda i: (0, 0)),
                  pl.BlockSpec((1, d), lambda i: (0, 0))],
        out_specs=pl.BlockSpec((tm, d), lambda i: (i, 0)),
        compiler_params=_params(("arbitrary",)),
        name="post_mix",
    )(attn_p, attn_s, br_ssm, gates, gates, x_p, x_s, ada_blk, w_fox16, w_out16, ln_g, ln_b)


def _pack_halves(x):
    c = x.shape[1] // 2
    return pltpu.pack_elementwise([x[:, :c], x[:, c:]], packed_dtype=BF16)


def _unpack_halves(w):
    return tuple(pltpu.unpack_elementwise(w, index=i, packed_dtype=BF16, unpacked_dtype=F32)
                 for i in range(2))


def _store_token_tiles(ref, words):
    m, c = words.shape
    nsl = c // LANES
    for s in range(nsl):
        ref[pl.ds(s, m, stride=nsl), :] = words[:, s * LANES:(s + 1) * LANES]


def _load_token_tiles(ref, m):
    nsl = ref.shape[0] // m
    return jnp.concatenate([ref[pl.ds(s, m, stride=nsl), :] for s in range(nsl)], axis=1)


def _token_rows(t, nsl):
    start = t * nsl
    if nsl % SUBLANES == 0:
        start = pl.multiple_of(start, SUBLANES)
    return pl.ds(start, nsl)


def _router_kernel(x_ref, ada_ref, wr_ref, rb_ref, hp_ref, e_ref, w_ref, h_scr, *, nsub):
    ng = N_EXPERT_GROUPS
    for s in range(nsub):
        rows = slice(s * ADA_BLOCK, (s + 1) * ADA_BLOCK)
        h_scr[rows, :] = x_ref[rows, :] * (1.0 + ada_ref[s, 4:5, :]) + ada_ref[s, 3:4, :]
    _store_token_tiles(hp_ref, _pack_halves(h_scr[...]))
    h_hi, h_lo = _split_bf16(h_scr[...])
    w_hi, w_lo = _split_bf16(wr_ref[...])
    logits = (_dot_nt(w_hi, h_lo) + _dot_nt(w_lo, h_hi)) + _dot_nt(w_hi, h_hi)
    scores = jax.nn.sigmoid(logits)
    sel = scores + rb_ref[...]
    gsz = sel.shape[0] // ng
    tm = sel.shape[1]
    xs = [sel[j * ng:(j + 1) * ng, :] for j in range(gsz)]
    sc = [scores[j * ng:(j + 1) * ng, :] for j in range(gsz)]
    neg = -jnp.inf

    def lmax(v):
        out = v[0]
        for a in v[1:]:
            out = jnp.maximum(out, a)
        return out

    def lmin(v):
        out = v[0]
        for a in v[1:]:
            out = jnp.minimum(out, a)
        return out

    m1 = lmax(xs)
    i1 = lmin([jnp.where(xs[j] == m1, j, gsz) for j in range(gsz)])
    m2 = lmax([jnp.where(i1 == j, neg, xs[j]) for j in range(gsz)])
    cur = m1 + m2
    giota = lax.broadcasted_iota(I32, (ng, tm), 0)
    gsel = jnp.zeros((ng, tm), F32)
    for _ in range(TOPK_EXPERT_GROUPS):
        m = jnp.max(cur, axis=0, keepdims=True)
        gi = jnp.min(jnp.where(cur == m, giota, ng), axis=0, keepdims=True)
        hit = giota == gi
        gsel = jnp.where(hit, 1.0, gsel)
        cur = jnp.where(hit, neg, cur)
    gmask = gsel > 0.0
    xs = [jnp.where(gmask, x, neg) for x in xs]
    eid = [giota * gsz + j for j in range(gsz)]
    n_e = ng * gsz
    vals = []
    for r in range(TOP_K):
        m = jnp.max(lmax(xs), axis=0, keepdims=True)
        ci = jnp.min(lmin([jnp.where(xs[j] == m, eid[j], n_e) for j in range(gsz)]),
                     axis=0, keepdims=True)
        hits = [eid[j] == ci for j in range(gsz)]
        v = sum(jnp.where(hits[j], sc[j], 0.0) for j in range(gsz))
        vals.append(jnp.sum(v, axis=0, keepdims=True))
        xs = [jnp.where(hits[j], neg, xs[j]) for j in range(gsz)]
        e_ref[r:r + 1, :] = ci
    tot = sum(vals)
    for r in range(TOP_K):
        w_ref[r:r + 1, :] = vals[r] / tot * ROUTED_SCALE


def _router(x1, ada_blk, wr_perm, rb_perm, tm=512):
    n, d = x1.shape
    e = wr_perm.shape[0]
    nsub = tm // ADA_BLOCK
    return pl.pallas_call(
        functools.partial(_router_kernel, nsub=nsub),
        out_shape=(jax.ShapeDtypeStruct((n * (d // 2 // LANES), LANES), jnp.uint32),
                   jax.ShapeDtypeStruct((TOP_K, n), I32),
                   jax.ShapeDtypeStruct((TOP_K, n), F32)),
        grid=(n // tm,),
        in_specs=[pl.BlockSpec((tm, d), lambda i: (i, 0)),
                  pl.BlockSpec((nsub, 6, d), lambda i: (i, 0, 0)),
                  pl.BlockSpec((e, d), lambda i: (0, 0)),
                  pl.BlockSpec((e, 1), lambda i: (0, 0))],
        out_specs=(pl.BlockSpec((tm * (d // 2 // LANES), LANES), lambda i: (i, 0)),
                   pl.BlockSpec((TOP_K, tm), lambda i: (0, i)),
                   pl.BlockSpec((TOP_K, tm), lambda i: (0, i))),
        scratch_shapes=[pltpu.VMEM((tm, d), F32)],
        compiler_params=_params(("arbitrary",)),
        name="router",
    )(x1, ada_blk, wr_perm, rb_perm)


def _plan_kernel(e_ref, pos_ref, st_ref, vis_ref, rank_scr, *, n_e, blk, tm):
    n = e_ref.shape[1]
    nblk = n // blk
    row = lax.broadcasted_iota(I32, (blk, blk), 0)
    col = lax.broadcasted_iota(I32, (blk, blk), 1)
    upper = jnp.where(row <= col, 1.0, 0.0).astype(BF16)
    eid = lax.broadcasted_iota(I32, (n_e, blk), 0)

    def count_body(cb, carry):
        cols = pl.ds(pl.multiple_of(cb * blk, blk), blk)
        e_blk = e_ref[:, cols]
        hit = jnp.zeros((n_e, blk), F32)
        for k in range(TOP_K):
            hit = hit + jnp.where(e_blk[k:k + 1, :] == eid, 1.0, 0.0)
        cs = _dot(hit.astype(BF16), upper) + carry
        rank_scr[:, cols] = cs - hit
        return cs[:, blk - 1:blk]

    counts = lax.fori_loop(0, nblk, count_body, jnp.zeros((n_e, 1), F32))

    hi = jnp.floor(counts * (1.0 / 128.0))
    lo = counts - hi * 128.0
    er = lax.broadcasted_iota(I32, (n_e, n_e), 0)
    ec = lax.broadcasted_iota(I32, (n_e, n_e), 1)
    lower = jnp.where(ec < er, 1.0, 0.0).astype(BF16)
    lower_incl = jnp.where(ec <= er, 1.0, 0.0).astype(BF16)
    wide = lambda v: jnp.broadcast_to(v, (n_e, LANES)).astype(BF16)
    starts = (_dot(lower, wide(hi)) * 128.0 + _dot(lower, wide(lo)))[:, :1]
    st_ref[...] = jnp.broadcast_to(starts, st_ref.shape).astype(I32)

    inv_tm = 1.0 / tm
    nonempty = counts > 0.0
    first_t = jnp.floor(starts * inv_tm)
    nvis = jnp.where(nonempty, jnp.floor((starts + counts - 1.0) * inv_tm) - first_t + 1.0, 0.0)
    vend = _dot(lower_incl, wide(nvis))[:, :1]
    gidx = _dot(lower_incl, wide(jnp.where(nonempty, 1.0, 0.0)))[:, :1] - 1.0
    total = jnp.max(vend, axis=0, keepdims=True)
    nv = vis_ref.shape[1]
    viota = lax.broadcasted_iota(I32, (1, nv), 1).astype(F32)
    vc = jnp.minimum(viota, total - 1.0)
    e_v = jnp.sum(jnp.where(vend <= vc, 1.0, 0.0), axis=0, keepdims=True)
    eio = lax.broadcasted_iota(I32, (n_e, nv), 0).astype(F32)
    mine = eio == e_v
    pick = lambda colv: jnp.sum(jnp.where(mine, colv, 0.0), axis=0, keepdims=True)
    tile_v = pick(first_t) + (vc - pick(vend - nvis))
    g_v = pick(gidx)
    slot_v = g_v - 2.0 * jnp.floor(g_v * 0.5)
    is_next = nonempty & (gidx == g_v + 1.0)
    nxt_v = (jnp.sum(jnp.where(is_next, eio + 1.0, 0.0), axis=0, keepdims=True) - 1.0)
    rows = [tile_v, e_v, jnp.where(viota < total, 1.0, 0.0), slot_v, nxt_v]
    rows += [jnp.zeros((1, nv), F32)] * (vis_ref.shape[0] - len(rows))
    vis_ref[...] = jnp.concatenate(rows, axis=0).astype(I32)

    def pos_body(cb, c):
        cols = pl.ds(pl.multiple_of(cb * blk, blk), blk)
        e_blk = e_ref[:, cols]
        val = rank_scr[:, cols] + starts
        for k in range(TOP_K):
            p = jnp.sum(jnp.where(e_blk[k:k + 1, :] == eid, val, 0.0), axis=0, keepdims=True)
            pos_ref[k:k + 1, cols] = p.astype(I32)
        return c

    lax.fori_loop(0, nblk, pos_body, 0)


def _plan(eidx_t, n_e, tm, blk=256):
    k, n = eidx_t.shape
    n_visits = (n * k) // tm + n_e - 1
    nv = -(-n_visits // LANES) * LANES
    pos, starts, vis = pl.pallas_call(
        functools.partial(_plan_kernel, n_e=n_e, blk=blk, tm=tm),
        out_shape=(jax.ShapeDtypeStruct((k, n), I32), jax.ShapeDtypeStruct((n_e, LANES), I32),
                   jax.ShapeDtypeStruct((SUBLANES, nv), I32)),
        scratch_shapes=[pltpu.VMEM((n_e, n), F32)],
        compiler_params=pltpu.CompilerParams(vmem_limit_bytes=VMEM_LIMIT_BYTES),
        name="moe_plan",
    )(eidx_t)
    return pos, starts, vis[:, :n_visits]


def _dispatch_kernel(pos_ref, h_ref, xs_ref, sem, *, tm):
    i = pl.program_id(0)
    nsl = h_ref.shape[0] // tm

    def issue(r, c):
        base = (i * tm + r) * TOP_K
        src = h_ref.at[_token_rows(r, nsl), :]
        for k in range(TOP_K):
            p = pos_ref[base + k]
            pltpu.make_async_copy(src, xs_ref.at[_token_rows(p, nsl), :],
                                  sem).start(priority=k % 2)
        return c

    lax.fori_loop(0, tm, issue, 0)
    for k in range(TOP_K):
        pltpu.make_async_copy(h_ref, xs_ref.at[pl.ds(0, tm * nsl), :], sem).wait()


def _dispatch(pos_flat, h2t, n, tm=256):
    rows, lanes = h2t.shape
    nsl = rows // n
    return pl.pallas_call(
        functools.partial(_dispatch_kernel, tm=tm),
        out_shape=jax.ShapeDtypeStruct((rows * TOP_K, lanes), h2t.dtype),
        grid_spec=pltpu.PrefetchScalarGridSpec(
            num_scalar_prefetch=1,
            grid=(n // tm,),
            in_specs=[pl.BlockSpec((tm * nsl, lanes), lambda i, pos: (i, 0))],
            out_specs=pl.BlockSpec(memory_space=pl.ANY),
            scratch_shapes=[pltpu.SemaphoreType.DMA(())]),
        compiler_params=_params(("arbitrary",)),
        name="moe_dispatch",
    )(pos_flat, h2t)


def _gmm_kernel(vt_ref, ve_ref, vv_ref, vs_ref, nx_ref, bd_ref,
                xs_ref, wg_hbm, wu_hbm, wd_hbm, ys_ref,
                wg32, wu32, wd32, wg16, wu16, wd16, acc, sem, *, tm):
    v = pl.program_id(0)
    t = vt_ref[v]
    e = ve_ref[v]
    slot = vs_ref[v]
    pv = jnp.maximum(v - 1, 0)
    first = v == 0
    valid = vv_ref[v] == 1
    r0 = t * tm
    lo = bd_ref[e]
    hi = bd_ref[e + 1]
    whole = (r0 >= lo) & (r0 + tm <= hi)

    def weight_copies(expert, s):
        return (pltpu.make_async_copy(wg_hbm.at[expert], wg32.at[s], sem.at[s, 0]),
                pltpu.make_async_copy(wu_hbm.at[expert], wu32.at[s], sem.at[s, 1]),
                pltpu.make_async_copy(wd_hbm.at[expert], wd32.at[s], sem.at[s, 2]))

    @pl.when(first)
    def _():
        for c in weight_copies(e, slot):
            c.start()

    @pl.when(first | (e != ve_ref[pv]))
    def _():
        for c in weight_copies(e, slot):
            c.wait()
        nxt = nx_ref[v]

        @pl.when(nxt >= 0)
        def _():
            for c in weight_copies(nxt, 1 - slot):
                c.start()

        wg16[...] = wg32[slot].astype(BF16)
        wu16[...] = wu32[slot].astype(BF16)
        wd16[...] = wd32[slot].astype(BF16)

    @pl.when(valid & jnp.logical_not(whole) & (first | (t != vt_ref[pv])))
    def _():
        acc[...] = jnp.zeros(acc.shape, F32)

    @pl.when(valid)
    def _():
        x = jnp.concatenate(_unpack_halves(_load_token_tiles(xs_ref, tm)), axis=1).astype(BF16)
        g = _dot(x, wg16[...])
        u = _dot(x, wu16[...])
        act = (g * jax.nn.sigmoid(g) * u).astype(BF16)
        y = _dot(act, wd16[...])

        @pl.when(whole)
        def _():
            _store_token_tiles(ys_ref, _pack_halves(y))

        @pl.when(jnp.logical_not(whole))
        def _():
            row = r0 + lax.broadcasted_iota(I32, (tm, 1), 0)
            acc[...] += jnp.where((row >= lo) & (row < hi), y, 0.0)
            _store_token_tiles(ys_ref, _pack_halves(acc[...]))


def _gmm(tile, expert, valid, slot, nxt, bounds, xs, w_gate, w_up, w_down, tm=256):
    n_e, d, f = w_gate.shape
    nsl = d // 2 // LANES
    n_visits = tile.shape[0]
    row_map = lambda v, vt, ve, vv, vs, nx, bd: (vt[v], 0)
    return pl.pallas_call(
        functools.partial(_gmm_kernel, tm=tm),
        out_shape=jax.ShapeDtypeStruct(xs.shape, xs.dtype),
        grid_spec=pltpu.PrefetchScalarGridSpec(
            num_scalar_prefetch=6,
            grid=(n_visits,),
            in_specs=[pl.BlockSpec((tm * nsl, LANES), row_map),
                      pl.BlockSpec(memory_space=pl.ANY),
                      pl.BlockSpec(memory_space=pl.ANY),
                      pl.BlockSpec(memory_space=pl.ANY)],
            out_specs=pl.BlockSpec((tm * nsl, LANES), row_map),
            scratch_shapes=[pltpu.VMEM((2, d, f), F32), pltpu.VMEM((2, d, f), F32),
                            pltpu.VMEM((2, f, d), F32),
                            pltpu.VMEM((d, f), BF16), pltpu.VMEM((d, f), BF16),
                            pltpu.VMEM((f, d), BF16),
                            pltpu.VMEM((tm, d), F32),
                            pltpu.SemaphoreType.DMA((2, 3))]),
        compiler_params=_params(("arbitrary",)),
        name="moe_experts",
    )(tile, expert, valid, slot, nxt, bounds, xs, w_gate, w_up, w_down)


def _combine_kernel(pos_ref, ys_ref, w_ref, hp_ref, x_ref, ada_ref, wsg_ref, wsu_ref, wsd_ref,
                    lg_ref, lb_ref, op_ref, os_ref, buf, ffn_scr, sem, *, tm, nsub, alpha, np_tiles):
    i = pl.program_id(0)
    nsl = hp_ref.shape[0] // tm

    def issue(r, c):
        base = (i * tm + r) * TOP_K
        dst_rows = _token_rows(r, nsl)
        for k in range(TOP_K):
            p = pos_ref[base + k]
            pltpu.make_async_copy(ys_ref.at[_token_rows(p, nsl), :], buf.at[k, dst_rows, :],
                                  sem).start(priority=k % 2)
        return c

    lax.fori_loop(0, tm, issue, 0)

    h16 = jnp.concatenate(_unpack_halves(_load_token_tiles(hp_ref, tm)), axis=1).astype(BF16)
    g = _dot(h16, wsg_ref[...])
    u = _dot(h16, wsu_ref[...])
    shared = _dot((g * jax.nn.sigmoid(g) * u).astype(BF16), wsd_ref[...])

    for k in range(TOP_K):
        pltpu.make_async_copy(ys_ref.at[pl.ds(0, tm * nsl), :], buf.at[k], sem).wait()
    ffn_scr[...] = shared
    half = ffn_scr.shape[1] // 2
    rblk = min(tm, 128)
    for r0 in range(0, tm, rblk):
        wb = [jnp.broadcast_to(w_ref[r0:r0 + rblk, k:k + 1], (rblk, LANES)) for k in range(TOP_K)]
        for s in range(nsl):
            lo = hi = None
            for k in range(TOP_K):
                a, b = _unpack_halves(buf[k, pl.ds(r0 * nsl + s, rblk, stride=nsl), :])
                lo = wb[k] * a if lo is None else lo + wb[k] * a
                hi = wb[k] * b if hi is None else hi + wb[k] * b
            ffn_scr[r0:r0 + rblk, s * LANES:(s + 1) * LANES] += lo
            ffn_scr[r0:r0 + rblk, half + s * LANES:half + (s + 1) * LANES] += hi

    def finish(o_ref):
        for s in range(nsub):
            rows = slice(s * ADA_BLOCK, (s + 1) * ADA_BLOCK)
            g2 = ada_ref[s, 5:6, :]
            y = alpha * x_ref[rows, :] + (1.0 + g2) * ffn_scr[rows, :]
            o_ref[rows, :] = _layer_norm(y, lg_ref[...], lb_ref[...])

    @pl.when(i < np_tiles)
    def _():
        finish(op_ref)

    @pl.when(i >= np_tiles)
    def _():
        finish(os_ref)


def _combine(pos_flat, ys, w_tok, h2p, x1, ada_blk, wsg16, wsu16, wsd16, ln_g, ln_b, *,
             alpha, n_p, tm=256):
    n, d = x1.shape
    nsl = d // 2 // LANES
    f = wsg16.shape[1]
    nsub = tm // ADA_BLOCK
    np_tiles = n_p // tm
    return pl.pallas_call(
        functools.partial(_combine_kernel, tm=tm, nsub=nsub, alpha=alpha, np_tiles=np_tiles),
        out_shape=(jax.ShapeDtypeStruct((n_p, d), F32), jax.ShapeDtypeStruct((n - n_p, d), F32)),
        grid_spec=pltpu.PrefetchScalarGridSpec(
            num_scalar_prefetch=1,
            grid=(n // tm,),
            in_specs=[pl.BlockSpec(memory_space=pl.ANY),
                      pl.BlockSpec((tm, TOP_K), lambda i, pos: (i, 0)),
                      pl.BlockSpec((tm * nsl, LANES), lambda i, pos: (i, 0)),
                      pl.BlockSpec((tm, d), lambda i, pos: (i, 0)),
                      pl.BlockSpec((nsub, 6, d), lambda i, pos: (i, 0, 0)),
                      pl.BlockSpec((d, f), lambda i, pos: (0, 0)),
                      pl.BlockSpec((d, f), lambda i, pos: (0, 0)),
                      pl.BlockSpec((f, d), lambda i, pos: (0, 0)),
                      pl.BlockSpec((1, d), lambda i, pos: (0, 0)),
                      pl.BlockSpec((1, d), lambda i, pos: (0, 0))],
            out_specs=(pl.BlockSpec((tm, d), lambda i, pos: (jnp.minimum(i, np_tiles - 1), 0)),
                       pl.BlockSpec((tm, d), lambda i, pos: (jnp.maximum(i - np_tiles, 0), 0))),
            scratch_shapes=[pltpu.VMEM((TOP_K, tm * nsl, LANES), ys.dtype),
                            pltpu.VMEM((tm, d), F32),
                            pltpu.SemaphoreType.DMA(())]),
        compiler_params=_params(("arbitrary",)),
        name="moe_combine",
    )(pos_flat, ys, w_tok, h2p, x1, ada_blk, wsg16, wsu16, wsd16, ln_g, ln_b)


def _layer(x_p, x_s, c_pad, blk_batch, cache_k, cache_v, layer, past_logf, st_re, st_im, lp, dims):
    bp, tp, bs, ts, d = dims
    n_p, n_s = bp * tp, bs * ts
    n = n_p + n_s
    ds = d // 2
    n_h = ds // FOX_HEAD_DIM
    n_hp = n_h // 2
    g_n, p_n = lp["ssm_lambda_re"].shape
    n_e = lp["w_router"].shape[1]
    depth_alpha = lp["alpha"]

    ada = _ada(c_pad, lp["w_ada"], lp["b_ada"])
    ada_blk = ada.reshape(ada.shape[0], 6, d)[blk_batch]

    w_in = lp["w_in"]
    w_main = jnp.concatenate([w_in[:, :4 * ds], w_in[:, 4 * ds + n_h:]], axis=1).astype(BF16)
    wf_t = w_in[:, 4 * ds:4 * ds + n_h].T.astype(BF16)
    zu, kv_p, kv_s, gates, qkv16, logf_t = _inproj(x_p, x_s, ada_blk, w_main, wf_t,
                                                   lp["b_f"].reshape(n_h, 1))

    lbr, lbi, bbr, bbi = _ssm_disc(lp["ssm_lambda_re"], lp["ssm_lambda_im"], lp["ssm_log_dt"],
                                   lp["ssm_b_re"], lp["ssm_b_im"])
    lam_lay = _to_state_layout(lbr[None], lbi[None])
    bd = jnp.concatenate([_block_diag_slabs(bbr.transpose(0, 2, 1)),
                          _block_diag_slabs(bbi.transpose(0, 2, 1))], axis=2).astype(BF16)
    cd = jnp.stack([_block_diag_slabs(lp["ssm_c_re"].transpose(0, 2, 1)),
                    _block_diag_slabs(lp["ssm_c_im"].transpose(0, 2, 1))], axis=1).astype(BF16)
    d_row = lp["ssm_d"].reshape(1, ds)
    x0_p = jnp.zeros((bp, 2 * g_n * p_n), F32)
    x0_s = _to_state_layout(st_re, st_im)
    g_p, xl_p = _ssm(zu, x0_p, lam_lay, bd, cd, d_row, row0=0, n_seq=bp, seq_len=tp,
                     s_blk=bp, tt=min(128, tp))
    g_s, xl_s = _ssm(zu, x0_s, lam_lay, bd, cd, d_row, row0=n_p, n_seq=bs, seq_len=ts,
                     s_blk=min(8, bs), tt=ts)
    g16 = jnp.concatenate(g_p + g_s, axis=0)
    br_ssm = _glu(g16, lp["w_glu"].astype(BF16))

    lf_p = logf_t[:, :n_p].reshape(n_h, bp, tp).transpose(1, 0, 2)
    lf_s = logf_t[:, n_p:].reshape(n_h, bs, ts).transpose(1, 0, 2)
    ck_p = _cumsum_lanes(lf_p.reshape(bp * n_h, tp)).reshape(bp, n_hp, 2, tp)
    past = past_logf.shape[1]
    cat = jnp.concatenate([past_logf.astype(F32).transpose(0, 2, 1), lf_s], axis=2)
    width = -(-(past + ts) // 256) * 256
    cat = jnp.pad(cat, ((0, 0), (0, 0), (0, width - past - ts)))
    nh_s = 4
    ck_s = _cumsum_lanes(cat.reshape(bs * n_h, width)).reshape(bs, n_h // nh_s, nh_s, width)
    attn_p = _attn_prompt(qkv16, ck_p, n_batch=bp, seq_len=tp, n_hp=n_hp, tq=min(512, tp))
    attn_s = _attn_sample(qkv16, cache_k[layer].reshape(bs, past, ds).astype(BF16),
                          cache_v[layer].reshape(bs, past, ds).astype(BF16),
                          ck_s[..., :past], ck_s[..., past:past + ts],
                          row0=n_p, n_batch=bs, seq_len=ts, n_h=n_h, nh=nh_s)

    x1 = _post(attn_p, attn_s, br_ssm, gates, x_p, x_s, ada_blk, lp["w_fox_o"].astype(BF16),
               lp["w_out"].astype(BF16), lp["ln1_g"].reshape(1, d), lp["ln1_b"].reshape(1, d),
               alpha=depth_alpha)

    gsz = n_e // N_EXPERT_GROUPS
    perm = lambda a: a.reshape(N_EXPERT_GROUPS, gsz, -1).transpose(1, 0, 2).reshape(n_e, -1)
    h2p, eidx_t, w_t = _router(x1, ada_blk, perm(lp["w_router"].T), perm(lp["router_bias"].reshape(n_e, 1)))
    tm_e = 256
    pos_t, starts, vis = _plan(eidx_t, n_e, tm_e)
    pos_flat = pos_t.T.reshape(n * TOP_K)
    bounds = jnp.concatenate([starts[:, 0], jnp.full((1,), n * TOP_K, I32)])
    xs = _dispatch(pos_flat, h2p, n)
    ys = _gmm(vis[0], vis[1], vis[2], vis[3], vis[4], bounds, xs, lp["w_exp_gate"], lp["w_exp_up"],
              lp["w_exp_down"], tm=tm_e)
    x2_p, x2_s = _combine(pos_flat, ys, w_t.T, h2p, x1, ada_blk, lp["w_sh_gate"].astype(BF16),
                          lp["w_sh_up"].astype(BF16), lp["w_sh_down"].astype(BF16),
                          lp["ln2_g"].reshape(1, d), lp["ln2_b"].reshape(1, d),
                          alpha=depth_alpha, n_p=n_p)

    k_new, v_new = (kv_p[0], kv_s[0]), (kv_p[1], kv_s[1])
    logf = logf_t.T
    ssm_p = _from_state_layout(xl_p, g_n, p_n)
    ssm_s = _from_state_layout(xl_s, g_n, p_n)
    return x2_p, x2_s, k_new, v_new, logf, ssm_p, ssm_s


def kernel(x_prompt, x_sample, cache_k, cache_v, cache_logf, state_ssm_re, state_ssm_im, c_prompt, c_sample, w_ada, b_ada, w_in, b_f, ssm_lambda_re, ssm_lambda_im, ssm_log_dt, ssm_b_re, ssm_b_im, ssm_c_re, ssm_c_im, ssm_d, w_glu, w_fox_o, w_out, ln1_g, ln1_b, w_router, router_bias, w_exp_gate, w_exp_up, w_exp_down, w_sh_gate, w_sh_up, w_sh_down, ln2_g, ln2_b):
    bp, tp, d = x_prompt.shape
    bs, ts, _ = x_sample.shape
    depth = w_ada.shape[0]
    n_p, n_s = bp * tp, bs * ts
    n_h = d // 2 // FOX_HEAD_DIM
    assert tp % ADA_BLOCK == 0 and ts == ADA_BLOCK
    alpha = (2.0 * depth) ** 0.25

    x_p, x_s = x_prompt.reshape(n_p, d), x_sample.reshape(n_s, d)
    c_all = jnp.concatenate([c_prompt, c_sample], axis=0)
    c_pad = jnp.pad(c_all, ((0, -(bp + bs) % 16), (0, 0)))
    blk_batch = np.concatenate([np.repeat(np.arange(bp), tp // ADA_BLOCK),
                                bp + np.repeat(np.arange(bs), ts // ADA_BLOCK)])
    dims = (bp, tp, bs, ts, d)
    outs_p, outs_s = [], []
    for l in range(depth):
        lp = dict(w_ada=w_ada[l], b_ada=b_ada[l], w_in=w_in[l], b_f=b_f[l],
                  ssm_lambda_re=ssm_lambda_re[l], ssm_lambda_im=ssm_lambda_im[l],
                  ssm_log_dt=ssm_log_dt[l], ssm_b_re=ssm_b_re[l], ssm_b_im=ssm_b_im[l],
                  ssm_c_re=ssm_c_re[l], ssm_c_im=ssm_c_im[l], ssm_d=ssm_d[l], w_glu=w_glu[l],
                  w_fox_o=w_fox_o[l], w_out=w_out[l], ln1_g=ln1_g[l], ln1_b=ln1_b[l],
                  w_router=w_router[l], router_bias=router_bias[l], w_exp_gate=w_exp_gate[l],
                  w_exp_up=w_exp_up[l], w_exp_down=w_exp_down[l], w_sh_gate=w_sh_gate[l],
                  w_sh_up=w_sh_up[l], w_sh_down=w_sh_down[l], ln2_g=ln2_g[l], ln2_b=ln2_b[l],
                  alpha=alpha)
        x_p, x_s, k_new, v_new, logf, ssm_p, ssm_s = _layer(
            x_p, x_s, c_pad, blk_batch, cache_k, cache_v, l, cache_logf[l],
            state_ssm_re[l].astype(F32), state_ssm_im[l].astype(F32), lp, dims)
        hd = FOX_HEAD_DIM
        outs_p.append((k_new[0].reshape(bp, tp, n_h, hd), v_new[0].reshape(bp, tp, n_h, hd),
                       logf[:n_p].reshape(bp, tp, n_h), ssm_p[0], ssm_p[1]))
        outs_s.append((k_new[1].reshape(bs, ts, n_h, hd), v_new[1].reshape(bs, ts, n_h, hd),
                       logf[n_p:].reshape(bs, ts, n_h), ssm_s[0], ssm_s[1]))
    stack = lambda outs, i: jnp.stack([o[i] for o in outs])
    return (x_p.reshape(bp, tp, d), x_s.reshape(bs, ts, d),
            stack(outs_p, 0), stack(outs_p, 1), stack(outs_p, 2), stack(outs_p, 3), stack(outs_p, 4),
            stack(outs_s, 0), stack(outs_s, 1), stack(outs_s, 2), stack(outs_s, 3), stack(outs_s, 4))
```

```python
import functools

import jax
import jax.numpy as jnp
import numpy as np
from jax import lax
from jax.experimental import pallas as pl
from jax.experimental.pallas import tpu as pltpu

F32 = jnp.float32
BF16 = jnp.bfloat16
I32 = jnp.int32

V7X_VMEM_BYTES = 64 * 1024 * 1024
VMEM_LIMIT_BYTES = V7X_VMEM_BYTES - 8 * 1024 * 1024
LANES = 128
SUBLANES = 8

SSM_GROUP_WIDTH = 16
SSM_STATE = 64
SSM_GROUPS_PER_SLAB = 8
FOX_HEAD_DIM = 64
N_EXPERT_GROUPS = 8
TOPK_EXPERT_GROUPS = 4
TOP_K = 8
ROUTED_SCALE = 2.5
LN_EPS = 1e-5
ADA_BLOCK = 64

NT_DIMS = (((1,), (1,)), ((), ()))


def _params(sem, vmem=VMEM_LIMIT_BYTES):
    return pltpu.CompilerParams(dimension_semantics=sem, vmem_limit_bytes=vmem)


def _dot(a, b):
    return jnp.dot(a, b, preferred_element_type=F32)


def _dot_nt(a, b):
    return lax.dot_general(a, b, NT_DIMS, preferred_element_type=F32)


def _split_bf16(x):
    hi = x.astype(BF16)
    lo = (x - hi.astype(F32)).astype(BF16)
    return hi, lo


def _log_sigmoid(x):
    return jnp.minimum(x, 0.0) - jnp.log1p(jnp.exp(-jnp.abs(x)))


def _gelu_tanh(x):
    c = np.float32(np.sqrt(2.0 / np.pi))
    return x * (0.5 * (1.0 + jnp.tanh(c * (x + 0.044715 * (x * x * x)))))


def _layer_norm(y, g, b):
    mu = jnp.mean(y, axis=-1, keepdims=True)
    yc = y - mu
    var = jnp.mean(yc * yc, axis=-1, keepdims=True)
    return yc * lax.rsqrt(var + LN_EPS) * g + b


def _ada_kernel(c_ref, w_ref, b_ref, o_ref):
    c = c_ref[...]
    a_hi, a_lo = _split_bf16(c * jax.nn.sigmoid(c))
    w_hi, w_lo = _split_bf16(w_ref[...])
    acc = _dot(a_hi, w_lo) + _dot(a_lo, w_hi)
    o_ref[...] = acc + _dot(a_hi, w_hi) + b_ref[...]


def _ada(c_pad, w_ada, b_ada):
    m, d = c_pad.shape
    n = w_ada.shape[1]
    tn = 1024
    return pl.pallas_call(
        _ada_kernel,
        out_shape=jax.ShapeDtypeStruct((m, n), F32),
        grid=(n // tn,),
        in_specs=[pl.BlockSpec((m, d), lambda j: (0, 0)),
                  pl.BlockSpec((d, tn), lambda j: (0, j)),
                  pl.BlockSpec((1, tn), lambda j: (0, j))],
        out_specs=pl.BlockSpec((m, tn), lambda j: (0, j)),
        compiler_params=_params(("arbitrary",)),
        name="ada",
    )(c_pad, w_ada, b_ada.reshape(1, n))


def _inproj_kernel(xp_ref, xs_ref, ada_ref, w_ref, wf_ref, bf_ref,
                   u_ref, kvp_ref, kvs_ref, g_ref, qkv_ref, lf_ref, h_scr,
                   *, nsub, q_scale, np_tiles):
    i = pl.program_id(0)
    j = pl.program_id(1)

    def modulate(x_ref):
        for s in range(nsub):
            rows = slice(s * ADA_BLOCK, (s + 1) * ADA_BLOCK)
            sh = ada_ref[s, 0:1, :]
            sc = ada_ref[s, 1:2, :]
            h_scr[rows, :] = (x_ref[rows, :] * (1.0 + sc) + sh).astype(BF16)

    @pl.when((j == 0) & (i < np_tiles))
    def _():
        modulate(xp_ref)

    @pl.when((j == 0) & (i >= np_tiles))
    def _():
        modulate(xs_ref)

    @pl.when(j == 0)
    def _():
        f_t = _dot(h_scr[...], wf_ref[...]).T
        lf_ref[...] = _log_sigmoid(f_t[:lf_ref.shape[0], :] + bf_ref[...])

    zt = _dot(h_scr[...], w_ref[...])

    @pl.when(j == 0)
    def _():
        u_ref[...] = zt

    @pl.when(j == 1)
    def _():
        qkv_ref[...] = (zt * q_scale).astype(BF16)

    @pl.when((j == 2) | (j == 3))
    def _():
        qkv_ref[...] = zt.astype(BF16)

    @pl.when(((j == 2) | (j == 3)) & (i < np_tiles))
    def _():
        kvp_ref[0] = zt

    @pl.when(((j == 2) | (j == 3)) & (i >= np_tiles))
    def _():
        kvs_ref[0] = zt

    @pl.when(j >= 4)
    def _():
        g_ref[...] = zt.astype(g_ref.dtype)


def _two_part_specs(tm, d, np_tiles, n_grid_args):
    if n_grid_args == 1:
        return [pl.BlockSpec((tm, d), lambda i: (jnp.minimum(i, np_tiles - 1), 0)),
                pl.BlockSpec((tm, d), lambda i: (jnp.maximum(i - np_tiles, 0), 0))]
    return [pl.BlockSpec((tm, d), lambda i, j: (jnp.minimum(i, np_tiles - 1), 0)),
            pl.BlockSpec((tm, d), lambda i, j: (jnp.maximum(i - np_tiles, 0), 0))]


def _inproj(x_p, x_s, ada_blk, w_main, wf_pad, bf_col, tm=512):
    n_p, d = x_p.shape
    n = n_p + x_s.shape[0]
    ds = d // 2
    h = bf_col.shape[0]
    nsub = tm // ADA_BLOCK
    ncol = w_main.shape[1] // ds
    np_tiles = n_p // tm
    kern = functools.partial(_inproj_kernel, nsub=nsub, q_scale=FOX_HEAD_DIM ** -0.5,
                             np_tiles=np_tiles)
    return pl.pallas_call(
        kern,
        out_shape=(jax.ShapeDtypeStruct((n, ds), F32),
                   jax.ShapeDtypeStruct((2, n_p, ds), F32),
                   jax.ShapeDtypeStruct((2, n - n_p, ds), F32),
                   jax.ShapeDtypeStruct((n, 2 * d), BF16),
                   jax.ShapeDtypeStruct((n, 3 * ds), BF16),
                   jax.ShapeDtypeStruct((h, n), F32)),
        grid=(n // tm, ncol),
        in_specs=_two_part_specs(tm, d, np_tiles, 2) + [
                  pl.BlockSpec((nsub, 6, d), lambda i, j: (i, 0, 0)),
                  pl.BlockSpec((d, ds), lambda i, j: (0, j)),
                  pl.BlockSpec((d, LANES), lambda i, j: (0, 0)),
                  pl.BlockSpec((h, 1), lambda i, j: (0, 0))],
        out_specs=(pl.BlockSpec((tm, ds), lambda i, j: (i, 0)),
                   pl.BlockSpec((1, tm, ds), lambda i, j: (
                       jnp.where(i < np_tiles, jnp.clip(j - 2, 0, 1), 1),
                       jnp.minimum(i, np_tiles - 1), 0)),
                   pl.BlockSpec((1, tm, ds), lambda i, j: (
                       jnp.where(i >= np_tiles, jnp.clip(j - 2, 0, 1), 0),
                       jnp.maximum(i - np_tiles, 0), 0)),
                   pl.BlockSpec((tm, ds), lambda i, j: (i, jnp.clip(j - 4, 0, ncol - 5))),
                   pl.BlockSpec((tm, ds), lambda i, j: (i, jnp.clip(j - 1, 0, 2))),
                   pl.BlockSpec((h, tm), lambda i, j: (0, i))),
        scratch_shapes=[pltpu.VMEM((tm, d), BF16)],
        compiler_params=_params(("arbitrary", "arbitrary")),
        name="inproj",
    )(x_p, x_s, ada_blk, w_main, wf_pad, bf_col)


def _ssm_disc_kernel(lr_ref, li_ref, ldt_ref, br_ref, bi_ref,
                     lbr_ref, lbi_ref, bbr_ref, bbi_ref):
    lr = jnp.minimum(lr_ref[...], -1e-4)
    li = li_ref[...]
    dt = jnp.exp(ldt_ref[...])
    er = jnp.exp(lr * dt)
    lbr = er * jnp.cos(li * dt)
    lbi = er * jnp.sin(li * dt)
    lbr_ref[...] = lbr
    lbi_ref[...] = lbi
    nr = lbr - 1.0
    den = lr * lr + li * li
    qr = (nr * lr + lbi * li) / den
    qi = (lbi * lr - nr * li) / den
    b_r = br_ref[...]
    b_i = bi_ref[...]
    bbr_ref[...] = qr * b_r - qi * b_i
    bbi_ref[...] = qr * b_i + qi * b_r


def _ssm_disc(lam_re, lam_im, log_dt, b_re, b_im):
    g, p, w = b_re.shape
    rep = lambda a: jnp.repeat(a, w, axis=1)
    shp = jax.ShapeDtypeStruct((g, p * w), F32)
    lbr, lbi, bbr, bbi = pl.pallas_call(
        _ssm_disc_kernel, out_shape=(shp, shp, shp, shp), name="ssm_disc",
    )(rep(lam_re), rep(lam_im), log_dt.reshape(g, 1),
      b_re.reshape(g, p * w), b_im.reshape(g, p * w))
    return (lbr[:, ::w], lbi[:, ::w], bbr.reshape(g, p, w), bbi.reshape(g, p, w))


def _to_state_layout(re, im):
    s, g, p = re.shape
    ns = g // SSM_GROUPS_PER_SLAB
    r = re.reshape(s, ns, 1, SSM_GROUPS_PER_SLAB * p)
    i = im.reshape(s, ns, 1, SSM_GROUPS_PER_SLAB * p)
    return jnp.concatenate([r, i], axis=2).reshape(s, 2 * g * p)


def _from_state_layout(x, g, p):
    s = x.shape[0]
    y = x.reshape(s, g // SSM_GROUPS_PER_SLAB, 2, SSM_GROUPS_PER_SLAB, p)
    return y[:, :, 0].reshape(s, g, p), y[:, :, 1].reshape(s, g, p)


def _block_diag_slabs(a):
    g, m, n = a.shape
    k = SSM_GROUPS_PER_SLAB
    a4 = a.reshape(g // k, k, m, n)
    eye = jnp.eye(k, dtype=bool)
    out = jnp.where(eye[None, :, None, :, None], a4[:, :, :, None, :], 0.0)
    return out.reshape(g // k, k * m, k * n)


def _ssm_kernel(*refs, n_refs, rpr, s_blk, tt, n_slab, sw):
    u_refs = refs[:n_refs]
    x0_ref, lam_ref, bd_ref, cd_ref, d_ref = refs[n_refs:n_refs + 5]
    o_refs = refs[n_refs + 5:2 * n_refs + 5]
    xl_ref = refs[2 * n_refs + 5]
    u_tm, bu, y_tm, st = refs[2 * n_refs + 6:]
    tb = pl.program_id(1)
    uw = SSM_GROUPS_PER_SLAB * SSM_GROUP_WIDTH
    npl = sw // LANES
    nph = npl // 2
    k_sub = SUBLANES // s_blk
    n_rows = s_blk * tt

    def seq_view(refs_, s):
        return (refs_[s], slice(None)) if n_refs == s_blk else (refs_[0], slice(s * tt, (s + 1) * tt))

    @pl.when(tb == 0)
    def _():
        st[...] = jnp.zeros(st.shape, F32)
        st[0:s_blk, :] = x0_ref[...]

    for s in range(s_blk):
        ref, rows = seq_view(u_refs, s)
        for j in range(n_slab):
            u_tm[j, pl.ds(s, tt, stride=s_blk), :] = ref[rows, j * uw:(j + 1) * uw]
    for j in range(n_slab):
        res = _dot(u_tm[j].astype(BF16), bd_ref[j])
        for q in range(npl):
            bu[j * npl + q] = res[:, q * LANES:(q + 1) * LANES]

    sub = lax.broadcasted_iota(I32, (SUBLANES, LANES), 0)
    for j in range(n_slab):
        c0 = j * sw
        a = [jnp.broadcast_to(lam_ref[:, c0 + q * LANES:c0 + (q + 1) * LANES], (SUBLANES, LANES))
             for q in range(npl)]

        def body(i, carry, j=j, a=a):
            rows = pl.ds(pl.multiple_of(i * SUBLANES, SUBLANES), SUBLANES)
            x = list(carry)
            b = [bu[j * npl + q, rows, :] for q in range(npl)]
            out = [None] * npl
            for step in range(k_sub):
                win = (sub >= step * s_blk) & (sub < (step + 1) * s_blk)
                for q in range(nph):
                    xr, xi = x[q], x[nph + q]
                    nr = a[q] * xr - a[nph + q] * xi + b[q]
                    ni = a[q] * xi + a[nph + q] * xr + b[nph + q]
                    out[q] = nr if step == 0 else jnp.where(win, nr, out[q])
                    out[nph + q] = ni if step == 0 else jnp.where(win, ni, out[nph + q])
                    if k_sub > 1:
                        nr = pltpu.roll(nr, s_blk, 0)
                        ni = pltpu.roll(ni, s_blk, 0)
                    x[q], x[nph + q] = nr, ni
            for q in range(npl):
                bu[j * npl + q, rows, :] = out[q]
            return tuple(x)

        init = tuple(st[:, c0 + q * LANES:c0 + (q + 1) * LANES] for q in range(npl))
        fin = lax.fori_loop(0, n_rows // SUBLANES, body, init, unroll=4)
        for q in range(npl):
            st[:, c0 + q * LANES:c0 + (q + 1) * LANES] = fin[q]

    for j in range(n_slab):
        xr16 = jnp.concatenate([bu[j * npl + q] for q in range(nph)], axis=1).astype(BF16)
        xi16 = jnp.concatenate([bu[j * npl + nph + q] for q in range(nph)], axis=1).astype(BF16)
        y_tm[j] = _dot(xr16, cd_ref[j, 0]) - _dot(xi16, cd_ref[j, 1])
    for s in range(s_blk):
        ref, rows = seq_view(u_refs, s)
        o_ref, o_rows = seq_view(o_refs, s)
        for j in range(n_slab):
            cols = slice(j * uw, (j + 1) * uw)
            y = y_tm[j, pl.ds(s, tt, stride=s_blk), :] + d_ref[:, cols] * ref[rows, cols]
            o_ref[o_rows, cols] = _gelu_tanh(y).astype(BF16)

    @pl.when(tb == pl.num_programs(1) - 1)
    def _():
        xl_ref[...] = st[0:s_blk, :]


def _ssm(z32, x0_lay, lam_lay, bd, cd, d_row, *, row0, n_seq, seq_len, s_blk, tt):
    n_slab = bd.shape[0]
    uw = bd.shape[1]
    sw = bd.shape[2]
    ds = n_slab * uw
    state_w = n_slab * sw
    n_tb = seq_len // tt
    n_sg = n_seq // s_blk
    if n_tb == 1:
        n_refs, rpr = 1, s_blk * tt
        assert row0 % rpr == 0
        in_maps = [lambda sg, tb: (row0 // rpr + sg, 0)]
        out_shape = [jax.ShapeDtypeStruct((n_seq * seq_len, ds), BF16)]
        out_maps = [lambda sg, tb: (sg, 0)]
    else:
        assert n_sg == 1 and row0 == 0
        n_refs, rpr = s_blk, tt
        in_maps = [functools.partial(lambda sg, tb, s: (s * n_tb + tb, 0), s=s) for s in range(s_blk)]
        out_shape = [jax.ShapeDtypeStruct((seq_len, ds), BF16)] * s_blk
        out_maps = [lambda sg, tb: (tb, 0)] * s_blk
    rows = n_refs * rpr
    kern = functools.partial(_ssm_kernel, n_refs=n_refs, rpr=rpr, s_blk=s_blk, tt=tt,
                             n_slab=n_slab, sw=sw)
    outs = pl.pallas_call(
        kern,
        out_shape=tuple(out_shape) + (jax.ShapeDtypeStruct((n_seq, state_w), F32),),
        grid=(n_sg, n_tb),
        in_specs=[pl.BlockSpec((rpr, ds), m) for m in in_maps] + [
            pl.BlockSpec((s_blk, state_w), lambda sg, tb: (sg, 0)),
            pl.BlockSpec((1, state_w), lambda sg, tb: (0, 0)),
            pl.BlockSpec(bd.shape, lambda sg, tb: (0, 0, 0)),
            pl.BlockSpec(cd.shape, lambda sg, tb: (0, 0, 0, 0)),
            pl.BlockSpec((1, ds), lambda sg, tb: (0, 0))],
        out_specs=tuple(pl.BlockSpec((rpr, ds), m) for m in out_maps) + (
            pl.BlockSpec((s_blk, state_w), lambda sg, tb: (sg, 0)),),
        scratch_shapes=[pltpu.VMEM((n_slab, rows, uw), F32),
                        pltpu.VMEM((state_w // LANES, rows, LANES), F32),
                        pltpu.VMEM((n_slab, rows, uw), F32),
                        pltpu.VMEM((SUBLANES, state_w), F32)],
        compiler_params=_params(("arbitrary", "arbitrary")),
        name="ssm",
    )(*([z32] * n_refs), x0_lay, lam_lay, bd, cd, d_row)
    return list(outs[:-1]), outs[-1]


def _cumsum_kernel(x_ref, o_ref, *, blk):
    r, t = x_ref.shape
    row = lax.broadcasted_iota(I32, (blk, blk), 0)
    col = lax.broadcasted_iota(I32, (blk, blk), 1)
    upper = jnp.where(row <= col, 1.0, 0.0).astype(BF16)
    carry = jnp.zeros((r, 1), F32)
    for c in range(t // blk):
        x = x_ref[:, c * blk:(c + 1) * blk]
        h1 = x.astype(BF16)
        r1 = x - h1.astype(F32)
        h2 = r1.astype(BF16)
        h3 = (r1 - h2.astype(F32)).astype(BF16)
        s = (_dot(h3, upper) + _dot(h2, upper)) + _dot(h1, upper) + carry
        o_ref[:, c * blk:(c + 1) * blk] = s
        carry = s[:, blk - 1:blk]


def _cumsum_lanes(x, blk=256, tr=64):
    r, t = x.shape
    tr = min(tr, r)
    return pl.pallas_call(
        functools.partial(_cumsum_kernel, blk=blk),
        out_shape=jax.ShapeDtypeStruct((r, t), F32),
        grid=(r // tr,),
        in_specs=[pl.BlockSpec((tr, t), lambda i: (i, 0))],
        out_specs=pl.BlockSpec((tr, t), lambda i: (i, 0)),
        compiler_params=_params(("arbitrary",)),
        name="cumsum",
    )(x)


def _head_lanes(nh):
    lane = lax.broadcasted_iota(I32, (1, nh * FOX_HEAD_DIM), 1)
    return [(lane >= h * FOX_HEAD_DIM) & (lane < (h + 1) * FOX_HEAD_DIM) for h in range(nh)]


def _per_head(sels, vals):
    reps = len(sels) * FOX_HEAD_DIM // LANES
    out = jnp.tile(vals[0], (1, reps)) if reps > 1 else vals[0]
    for sel, v in zip(sels[1:], vals[1:]):
        out = jnp.where(sel, jnp.tile(v, (1, reps)) if reps > 1 else v, out)
    return out


def _attn_step(q, k, v, ck, m_scr, l_scr, acc_scr, mask):
    nh = m_scr.shape[0]
    sels = _head_lanes(nh)
    pv, alphas = None, []
    for h in range(nh):
        qh = jnp.where(sels[h], q, jnp.zeros_like(q))
        s = _dot_nt(qh, k) - ck[h:h + 1, :]
        if mask is not None:
            s = jnp.where(mask, s, -jnp.inf)
        m_prev = m_scr[h]
        m_new = jnp.maximum(m_prev, jnp.max(s, axis=-1, keepdims=True))
        alpha = jnp.exp(m_prev - m_new)
        p = jnp.exp(s - m_new[:, :1])
        l_scr[h] = alpha * l_scr[h] + jnp.sum(p, axis=-1, keepdims=True)
        m_scr[h] = m_new
        pv_h = _dot(p.astype(BF16), v)
        pv = pv_h if pv is None else jnp.where(sels[h], pv_h, pv)
        alphas.append(alpha)
    acc_scr[...] = _per_head(sels, alphas) * acc_scr[...] + pv


def _attn_init(m_scr, l_scr, acc_scr):
    m_scr[...] = jnp.full(m_scr.shape, -jnp.inf, F32)
    l_scr[...] = jnp.zeros(l_scr.shape, F32)
    acc_scr[...] = jnp.zeros(acc_scr.shape, F32)


def _attn_finish(o_ref, l_scr, acc_scr):
    nh = l_scr.shape[0]
    l = _per_head(_head_lanes(nh), [l_scr[h] for h in range(nh)])
    o_ref[...] = (acc_scr[...] / l).astype(o_ref.dtype)


def _causal_mask(tq, tk):
    return (lax.broadcasted_iota(I32, (tq, tk), 1) <= lax.broadcasted_iota(I32, (tq, tk), 0))


def _attn_prompt_kernel(qt_ref, kt_ref, q_ref, k_ref, v_ref, ck_ref, o_ref, m_scr, l_scr, acc_scr,
                        *, tq):
    step = pl.program_id(2)
    qi, ki = qt_ref[step], kt_ref[step]

    @pl.when(ki == 0)
    def _():
        _attn_init(m_scr, l_scr, acc_scr)

    @pl.when(ki < qi)
    def _():
        _attn_step(q_ref[...], k_ref[...], v_ref[...], ck_ref[0, 0], m_scr, l_scr, acc_scr, None)

    @pl.when(ki == qi)
    def _():
        _attn_step(q_ref[...], k_ref[...], v_ref[...], ck_ref[0, 0], m_scr, l_scr, acc_scr,
                   _causal_mask(tq, tq))
        _attn_finish(o_ref, l_scr, acc_scr)


def _attn_prompt(qkv16, ck, *, n_batch, seq_len, n_h, nh, tq=512):
    nq = seq_len // tq
    lw = nh * FOX_HEAD_DIM
    ng = n_h // nh
    pairs = [(qi, ki) for qi in range(nq) for ki in range(qi + 1)]
    qt = jnp.asarray([p[0] for p in pairs], I32)
    kt = jnp.asarray([p[1] for p in pairs], I32)
    q_map = lambda b, g, s, qt, kt: (b * nq + qt[s], g)
    kv_map = lambda c: (lambda b, g, s, qt, kt: (b * nq + kt[s], c * ng + g))
    return pl.pallas_call(
        functools.partial(_attn_prompt_kernel, tq=tq),
        out_shape=jax.ShapeDtypeStruct((n_batch * seq_len, ng * lw), BF16),
        grid_spec=pltpu.PrefetchScalarGridSpec(
            num_scalar_prefetch=2,
            grid=(n_batch, ng, len(pairs)),
            in_specs=[pl.BlockSpec((tq, lw), q_map),
                      pl.BlockSpec((tq, lw), kv_map(1)),
                      pl.BlockSpec((tq, lw), kv_map(2)),
                      pl.BlockSpec((1, 1, nh, tq), lambda b, g, s, qt, kt: (b, g, 0, kt[s]))],
            out_specs=pl.BlockSpec((tq, lw), q_map),
            scratch_shapes=[pltpu.VMEM((nh, tq, LANES), F32), pltpu.VMEM((nh, tq, LANES), F32),
                            pltpu.VMEM((tq, lw), F32)]),
        compiler_params=_params(("arbitrary",) * 3),
        name="attn_prompt",
    )(qt, kt, qkv16, qkv16, qkv16, ck)


def _attn_sample_kernel(q_ref, kp_ref, vp_ref, kn_ref, vn_ref, ckp_ref, ckn_ref, o_ref, *, ts, nh):
    hd = FOX_HEAD_DIM
    lw = nh * hd
    lane = lax.broadcasted_iota(I32, (1, lw), 1)
    sels = [(lane >= h * hd) & (lane < (h + 1) * hd) for h in range(nh)]
    q = q_ref[...]
    qbd = jnp.concatenate([jnp.where(sels[h], q, jnp.zeros_like(q)) for h in range(nh)], axis=0)

    def update(state, k, v, ck, mask):
        m_prev, l_prev, acc = state
        s = _dot_nt(qbd, k)
        rows = []
        for h in range(nh):
            sh = s[h * ts:(h + 1) * ts, :] - ck[h:h + 1, :]
            rows.append(sh if mask is None else jnp.where(mask, sh, -jnp.inf))
        s = jnp.concatenate(rows, axis=0)
        m_new = jnp.maximum(m_prev, jnp.max(s, axis=-1, keepdims=True))
        alpha = jnp.exp(m_prev - m_new)
        p = jnp.exp(s - m_new)
        l_new = alpha * l_prev + jnp.sum(p, axis=-1, keepdims=True)
        return m_new, l_new, alpha * acc + _dot(p.astype(BF16), v)

    state = (jnp.full((nh * ts, 1), -jnp.inf, F32), jnp.zeros((nh * ts, 1), F32),
             jnp.zeros((nh * ts, lw), F32))
    state = update(state, kp_ref[0], vp_ref[0], ckp_ref[0, 0], None)
    _, l, acc = update(state, kn_ref[...], vn_ref[...], ckn_ref[0, 0], _causal_mask(ts, ts))
    res = acc / l
    out = res[0:ts, :]
    for h in range(1, nh):
        out = jnp.where(sels[h], res[h * ts:(h + 1) * ts, :], out)
    o_ref[...] = out.astype(o_ref.dtype)


def _attn_sample(qkv16, k_past, v_past, ck_past, ck_new, *, row0, n_batch, seq_len, n_h, nh=4):
    past = k_past.shape[1]
    lw = nh * FOX_HEAD_DIM
    ng = n_h // nh
    rb0 = row0 // seq_len
    new_map = lambda c: (lambda b, g: (rb0 + b, c * ng + g))
    return pl.pallas_call(
        functools.partial(_attn_sample_kernel, ts=seq_len, nh=nh),
        out_shape=jax.ShapeDtypeStruct((n_batch * seq_len, ng * lw), BF16),
        grid=(n_batch, ng),
        in_specs=[pl.BlockSpec((seq_len, lw), new_map(0)),
                  pl.BlockSpec((1, past, lw), lambda b, g: (b, 0, g)),
                  pl.BlockSpec((1, past, lw), lambda b, g: (b, 0, g)),
                  pl.BlockSpec((seq_len, lw), new_map(1)),
                  pl.BlockSpec((seq_len, lw), new_map(2)),
                  pl.BlockSpec((1, 1, nh, past), lambda b, g: (b, g, 0, 0)),
                  pl.BlockSpec((1, 1, nh, seq_len), lambda b, g: (b, g, 0, 0))],
        out_specs=pl.BlockSpec((seq_len, lw), lambda b, g: (b, g)),
        compiler_params=_params(("arbitrary",) * 2),
        name="attn_sample",
    )(qkv16, k_past, v_past, qkv16, qkv16, ck_past, ck_new)


def _glu_kernel(g_ref, wa_ref, wb_ref, o_ref):
    g = g_ref[...]
    o_ref[...] = _dot(g, wa_ref[...]) * jax.nn.sigmoid(_dot(g, wb_ref[...]))


def _glu(g16, w_glu16, tm=512, tn=1024):
    n, ds = g16.shape
    d = w_glu16.shape[1] // 2
    tn = min(tn, d)
    nb = d // tn
    return pl.pallas_call(
        _glu_kernel,
        out_shape=jax.ShapeDtypeStruct((n, d), F32),
        grid=(n // tm, nb),
        in_specs=[pl.BlockSpec((tm, ds), lambda i, j: (i, 0)),
                  pl.BlockSpec((ds, tn), lambda i, j: (0, j)),
                  pl.BlockSpec((ds, tn), lambda i, j: (0, nb + j))],
        out_specs=pl.BlockSpec((tm, tn), lambda i, j: (i, j)),
        compiler_params=_params(("arbitrary", "arbitrary")),
        name="glu",
    )(g16, w_glu16, w_glu16)


def _post_kernel(attp_ref, atts_ref, brs_ref, gs_ref, gf_ref, xp_ref, xs_ref, ada_ref, wfo_ref,
                 wo_ref, lg_ref, lb_ref, o_ref, *, nsub, alpha, np_tiles):
    i = pl.program_id(0)
    att = jnp.where(i < np_tiles, attp_ref[...], atts_ref[...])
    br_fox = _dot(att, wfo_ref[...])
    merged = (jax.nn.sigmoid(gs_ref[...].astype(F32)) * brs_ref[...]
              + jax.nn.sigmoid(gf_ref[...].astype(F32)) * br_fox)
    mix = _dot(merged.astype(BF16), wo_ref[...])

    def finish(x_ref):
        for s in range(nsub):
            rows = slice(s * ADA_BLOCK, (s + 1) * ADA_BLOCK)
            g1 = ada_ref[s, 2:3, :]
            y = alpha * x_ref[rows, :] + (1.0 + g1) * mix[rows, :]
            o_ref[rows, :] = _layer_norm(y, lg_ref[...], lb_ref[...])

    @pl.when(i < np_tiles)
    def _():
        finish(xp_ref)

    @pl.when(i >= np_tiles)
    def _():
        finish(xs_ref)


def _post(attn_p, attn_s, br_ssm, gates, x_p, x_s, ada_blk, w_fox16, w_out16, ln_g, ln_b, *,
          alpha, tm=256):
    n_p, d = x_p.shape
    n = n_p + x_s.shape[0]
    ds = d // 2
    nsub = tm // ADA_BLOCK
    np_tiles = n_p // tm
    return pl.pallas_call(
        functools.partial(_post_kernel, nsub=nsub, alpha=alpha, np_tiles=np_tiles),
        out_shape=jax.ShapeDtypeStruct((n, d), F32),
        grid=(n // tm,),
        in_specs=_two_part_specs(tm, ds, np_tiles, 1) + [
                  pl.BlockSpec((tm, d), lambda i: (i, 0)),
                  pl.BlockSpec((tm, d), lambda i: (i, 0)),
                  pl.BlockSpec((tm, d), lambda i: (i, 1)),
                  ] + _two_part_specs(tm, d, np_tiles, 1) + [
                  pl.BlockSpec((nsub, 6, d), lambda i: (i, 0, 0)),
                  pl.BlockSpec((ds, d), lambda i: (0, 0)),
                  pl.BlockSpec((d, d), lambda i: (0, 0)),
                  pl.BlockSpec((1, d), lambda i: (0, 0)),
                  pl.BlockSpec((1, d), lambda i: (0, 0))],
        out_specs=pl.BlockSpec((tm, d), lambda i: (i, 0)),
        compiler_params=_params(("arbitrary",)),
        name="post_mix",
    )(attn_p, attn_s, br_ssm, gates, gates, x_p, x_s, ada_blk, w_fox16, w_out16, ln_g, ln_b)


def _pack_halves(x):
    c = x.shape[1] // 2
    return pltpu.pack_elementwise([x[:, :c], x[:, c:]], packed_dtype=BF16)


def _unpack_halves(w):
    return tuple(pltpu.unpack_elementwise(w, index=i, packed_dtype=BF16, unpacked_dtype=F32)
                 for i in range(2))


def _store_token_tiles(ref, words):
    m, c = words.shape
    nsl = c // LANES
    for s in range(nsl):
        ref[pl.ds(s, m, stride=nsl), :] = words[:, s * LANES:(s + 1) * LANES]


def _load_token_tiles(ref, m):
    nsl = ref.shape[0] // m
    return jnp.concatenate([ref[pl.ds(s, m, stride=nsl), :] for s in range(nsl)], axis=1)


def _token_rows(t, nsl):
    start = t * nsl
    if nsl % SUBLANES == 0:
        start = pl.multiple_of(start, SUBLANES)
    return pl.ds(start, nsl)


def _router_kernel(x_ref, ada_ref, wr_ref, rb_ref, hp_ref, e_ref, w_ref, h_scr, *, nsub):
    ng = N_EXPERT_GROUPS
    for s in range(nsub):
        rows = slice(s * ADA_BLOCK, (s + 1) * ADA_BLOCK)
        h_scr[rows, :] = x_ref[rows, :] * (1.0 + ada_ref[s, 4:5, :]) + ada_ref[s, 3:4, :]
    _store_token_tiles(hp_ref, _pack_halves(h_scr[...]))
    h_hi, h_lo = _split_bf16(h_scr[...])
    w_hi, w_lo = _split_bf16(wr_ref[...])
    logits = (_dot_nt(w_hi, h_lo) + _dot_nt(w_lo, h_hi)) + _dot_nt(w_hi, h_hi)
    scores = jax.nn.sigmoid(logits)
    sel = scores + rb_ref[...]
    gsz = sel.shape[0] // ng
    tm = sel.shape[1]
    xs = [sel[j * ng:(j + 1) * ng, :] for j in range(gsz)]
    sc = [scores[j * ng:(j + 1) * ng, :] for j in range(gsz)]
    neg = -jnp.inf

    def lmax(v):
        out = v[0]
        for a in v[1:]:
            out = jnp.maximum(out, a)
        return out

    def lmin(v):
        out = v[0]
        for a in v[1:]:
            out = jnp.minimum(out, a)
        return out

    m1 = lmax(xs)
    i1 = lmin([jnp.where(xs[j] == m1, j, gsz) for j in range(gsz)])
    m2 = lmax([jnp.where(i1 == j, neg, xs[j]) for j in range(gsz)])
    cur = m1 + m2
    giota = lax.broadcasted_iota(I32, (ng, tm), 0)
    gsel = jnp.zeros((ng, tm), F32)
    for _ in range(TOPK_EXPERT_GROUPS):
        m = jnp.max(cur, axis=0, keepdims=True)
        gi = jnp.min(jnp.where(cur == m, giota, ng), axis=0, keepdims=True)
        hit = giota == gi
        gsel = jnp.where(hit, 1.0, gsel)
        cur = jnp.where(hit, neg, cur)
    gmask = gsel > 0.0
    xs = [jnp.where(gmask, x, neg) for x in xs]
    eid = [giota * gsz + j for j in range(gsz)]
    n_e = ng * gsz
    vals = []
    for r in range(TOP_K):
        m = jnp.max(lmax(xs), axis=0, keepdims=True)
        ci = jnp.min(lmin([jnp.where(xs[j] == m, eid[j], n_e) for j in range(gsz)]),
                     axis=0, keepdims=True)
        hits = [eid[j] == ci for j in range(gsz)]
        v = sum(jnp.where(hits[j], sc[j], 0.0) for j in range(gsz))
        vals.append(jnp.sum(v, axis=0, keepdims=True))
        xs = [jnp.where(hits[j], neg, xs[j]) for j in range(gsz)]
        e_ref[r:r + 1, :] = ci
    tot = sum(vals)
    for r in range(TOP_K):
        w_ref[r:r + 1, :] = vals[r] / tot * ROUTED_SCALE


def _router(x1, ada_blk, wr_perm, rb_perm, tm=512):
    n, d = x1.shape
    e = wr_perm.shape[0]
    nsub = tm // ADA_BLOCK
    return pl.pallas_call(
        functools.partial(_router_kernel, nsub=nsub),
        out_shape=(jax.ShapeDtypeStruct((n * (d // 2 // LANES), LANES), jnp.uint32),
                   jax.ShapeDtypeStruct((TOP_K, n), I32),
                   jax.ShapeDtypeStruct((TOP_K, n), F32)),
        grid=(n // tm,),
        in_specs=[pl.BlockSpec((tm, d), lambda i: (i, 0)),
                  pl.BlockSpec((nsub, 6, d), lambda i: (i, 0, 0)),
                  pl.BlockSpec((e, d), lambda i: (0, 0)),
                  pl.BlockSpec((e, 1), lambda i: (0, 0))],
        out_specs=(pl.BlockSpec((tm * (d // 2 // LANES), LANES), lambda i: (i, 0)),
                   pl.BlockSpec((TOP_K, tm), lambda i: (0, i)),
                   pl.BlockSpec((TOP_K, tm), lambda i: (0, i))),
        scratch_shapes=[pltpu.VMEM((tm, d), F32)],
        compiler_params=_params(("arbitrary",)),
        name="router",
    )(x1, ada_blk, wr_perm, rb_perm)


def _plan_kernel(e_ref, pos_ref, st_ref, vis_ref, rank_scr, *, n_e, blk, tm):
    n = e_ref.shape[1]
    nblk = n // blk
    row = lax.broadcasted_iota(I32, (blk, blk), 0)
    col = lax.broadcasted_iota(I32, (blk, blk), 1)
    upper = jnp.where(row <= col, 1.0, 0.0).astype(BF16)
    eid = lax.broadcasted_iota(I32, (n_e, blk), 0)

    def count_body(cb, carry):
        cols = pl.ds(pl.multiple_of(cb * blk, blk), blk)
        e_blk = e_ref[:, cols]
        hit = jnp.zeros((n_e, blk), F32)
        for k in range(TOP_K):
            hit = hit + jnp.where(e_blk[k:k + 1, :] == eid, 1.0, 0.0)
        cs = _dot(hit.astype(BF16), upper) + carry
        rank_scr[:, cols] = cs - hit
        return cs[:, blk - 1:blk]

    counts = lax.fori_loop(0, nblk, count_body, jnp.zeros((n_e, 1), F32))

    hi = jnp.floor(counts * (1.0 / 128.0))
    lo = counts - hi * 128.0
    er = lax.broadcasted_iota(I32, (n_e, n_e), 0)
    ec = lax.broadcasted_iota(I32, (n_e, n_e), 1)
    lower = jnp.where(ec < er, 1.0, 0.0).astype(BF16)
    lower_incl = jnp.where(ec <= er, 1.0, 0.0).astype(BF16)
    wide = lambda v: jnp.broadcast_to(v, (n_e, LANES)).astype(BF16)
    starts = (_dot(lower, wide(hi)) * 128.0 + _dot(lower, wide(lo)))[:, :1]
    st_ref[...] = jnp.broadcast_to(starts, st_ref.shape).astype(I32)

    inv_tm = 1.0 / tm
    nonempty = counts > 0.0
    first_t = jnp.floor(starts * inv_tm)
    nvis = jnp.where(nonempty, jnp.floor((starts + counts - 1.0) * inv_tm) - first_t + 1.0, 0.0)
    vend = _dot(lower_incl, wide(nvis))[:, :1]
    gidx = _dot(lower_incl, wide(jnp.where(nonempty, 1.0, 0.0)))[:, :1] - 1.0
    total = jnp.max(vend, axis=0, keepdims=True)
    nv = vis_ref.shape[1]
    viota = lax.broadcasted_iota(I32, (1, nv), 1).astype(F32)
    vc = jnp.minimum(viota, total - 1.0)
    e_v = jnp.sum(jnp.where(vend <= vc, 1.0, 0.0), axis=0, keepdims=True)
    eio = lax.broadcasted_iota(I32, (n_e, nv), 0).astype(F32)
    mine = eio == e_v
    pick = lambda colv: jnp.sum(jnp.where(mine, colv, 0.0), axis=0, keepdims=True)
    tile_v = pick(first_t) + (vc - pick(vend - nvis))
    g_v = pick(gidx)
    slot_v = g_v - 2.0 * jnp.floor(g_v * 0.5)
    is_next = nonempty & (gidx == g_v + 1.0)
    nxt_v = (jnp.sum(jnp.where(is_next, eio + 1.0, 0.0), axis=0, keepdims=True) - 1.0)
    rows = [tile_v, e_v, jnp.where(viota < total, 1.0, 0.0), slot_v, nxt_v]
    rows += [jnp.zeros((1, nv), F32)] * (vis_ref.shape[0] - len(rows))
    vis_ref[...] = jnp.concatenate(rows, axis=0).astype(I32)

    def pos_body(cb, c):
        cols = pl.ds(pl.multiple_of(cb * blk, blk), blk)
        e_blk = e_ref[:, cols]
        val = rank_scr[:, cols] + starts
        for k in range(TOP_K):
            p = jnp.sum(jnp.where(e_blk[k:k + 1, :] == eid, val, 0.0), axis=0, keepdims=True)
            pos_ref[k:k + 1, cols] = p.astype(I32)
        return c

    lax.fori_loop(0, nblk, pos_body, 0)


def _plan(eidx_t, n_e, tm, blk=256):
    k, n = eidx_t.shape
    n_visits = (n * k) // tm + n_e - 1
    nv = -(-n_visits // LANES) * LANES
    pos, starts, vis = pl.pallas_call(
        functools.partial(_plan_kernel, n_e=n_e, blk=blk, tm=tm),
        out_shape=(jax.ShapeDtypeStruct((k, n), I32), jax.ShapeDtypeStruct((n_e, LANES), I32),
                   jax.ShapeDtypeStruct((SUBLANES, nv), I32)),
        scratch_shapes=[pltpu.VMEM((n_e, n), F32)],
        compiler_params=pltpu.CompilerParams(vmem_limit_bytes=VMEM_LIMIT_BYTES),
        name="moe_plan",
    )(eidx_t)
    return pos, starts, vis[:, :n_visits]


def _dispatch_kernel(pos_ref, h_ref, xs_ref, sem, *, tm):
    i = pl.program_id(0)
    nsl = h_ref.shape[0] // tm

    def issue(r, c):
        base = (i * tm + r) * TOP_K
        src = h_ref.at[_token_rows(r, nsl), :]
        for k in range(TOP_K):
            p = pos_ref[base + k]
            pltpu.make_async_copy(src, xs_ref.at[_token_rows(p, nsl), :],
                                  sem).start(priority=k % 2)
        return c

    lax.fori_loop(0, tm, issue, 0)
    for k in range(TOP_K):
        pltpu.make_async_copy(h_ref, xs_ref.at[pl.ds(0, tm * nsl), :], sem).wait()


def _dispatch(pos_flat, h2t, n, tm=256):
    rows, lanes = h2t.shape
    nsl = rows // n
    return pl.pallas_call(
        functools.partial(_dispatch_kernel, tm=tm),
        out_shape=jax.ShapeDtypeStruct((rows * TOP_K, lanes), h2t.dtype),
        grid_spec=pltpu.PrefetchScalarGridSpec(
            num_scalar_prefetch=1,
            grid=(n // tm,),
            in_specs=[pl.BlockSpec((tm * nsl, lanes), lambda i, pos: (i, 0))],
            out_specs=pl.BlockSpec(memory_space=pl.ANY),
            scratch_shapes=[pltpu.SemaphoreType.DMA(())]),
        compiler_params=_params(("arbitrary",)),
        name="moe_dispatch",
    )(pos_flat, h2t)


def _gmm_kernel(vt_ref, ve_ref, vv_ref, vs_ref, nx_ref, bd_ref,
                xs_ref, wg_hbm, wu_hbm, wd_hbm, ys_ref,
                wg32, wu32, wd32, wg16, wu16, wd16, acc, sem, *, tm, chain):
    v = pl.program_id(0)
    t = vt_ref[v]
    e = ve_ref[v]
    slot = vs_ref[v]
    pv = jnp.maximum(v - 1, 0)
    first = v == 0
    valid = vv_ref[v] == 1
    r0 = t * tm
    lo = bd_ref[e]
    hi = bd_ref[e + 1]
    whole = (r0 >= lo) & (r0 + tm <= hi)

    def weight_copies(expert, s):
        return (pltpu.make_async_copy(wg_hbm.at[expert], wg32.at[s], sem.at[s, 0]),
                pltpu.make_async_copy(wu_hbm.at[expert], wu32.at[s], sem.at[s, 1]),
                pltpu.make_async_copy(wd_hbm.at[expert], wd32.at[s], sem.at[s, 2]))

    @pl.when(first)
    def _():
        for c in weight_copies(e, slot):
            c.start()

    @pl.when(first | (e != ve_ref[pv]))
    def _():
        for c in weight_copies(e, slot):
            c.wait()
        nxt = nx_ref[v]

        @pl.when(nxt >= 0)
        def _():
            for c in weight_copies(nxt, 1 - slot):
                c.start()

        wg16[...] = wg32[slot].astype(BF16)
        wu16[...] = wu32[slot].astype(BF16)
        wd16[...] = wd32[slot].astype(BF16)

    @pl.when(valid & jnp.logical_not(whole) & (first | (t != vt_ref[pv])))
    def _():
        acc[...] = jnp.zeros(acc.shape, F32)

    @pl.when(valid)
    def _():
        xw = _load_token_tiles(xs_ref, tm)
        parts = []
        for c0 in range(0, tm, chain):
            x = jnp.concatenate(_unpack_halves(xw[c0:c0 + chain]), axis=1).astype(BF16)
            g = _dot(x, wg16[...])
            u = _dot(x, wu16[...])
            act = (g * jax.nn.sigmoid(g) * u).astype(BF16)
            parts.append(_dot(act, wd16[...]))
        y = parts[0] if len(parts) == 1 else jnp.concatenate(parts, axis=0)

        @pl.when(whole)
        def _():
            _store_token_tiles(ys_ref, _pack_halves(y))

        @pl.when(jnp.logical_not(whole))
        def _():
            row = r0 + lax.broadcasted_iota(I32, (tm, 1), 0)
            acc[...] += jnp.where((row >= lo) & (row < hi), y, 0.0)
            _store_token_tiles(ys_ref, _pack_halves(acc[...]))


def _gmm(tile, expert, valid, slot, nxt, bounds, xs, w_gate, w_up, w_down, tm=256):
    n_e, d, f = w_gate.shape
    nsl = d // 2 // LANES
    n_visits = tile.shape[0]
    row_map = lambda v, vt, ve, vv, vs, nx, bd: (vt[v], 0)
    return pl.pallas_call(
        functools.partial(_gmm_kernel, tm=tm, chain=min(tm, 256)),
        out_shape=jax.ShapeDtypeStruct(xs.shape, xs.dtype),
        grid_spec=pltpu.PrefetchScalarGridSpec(
            num_scalar_prefetch=6,
            grid=(n_visits,),
            in_specs=[pl.BlockSpec((tm * nsl, LANES), row_map),
                      pl.BlockSpec(memory_space=pl.ANY),
                      pl.BlockSpec(memory_space=pl.ANY),
                      pl.BlockSpec(memory_space=pl.ANY)],
            out_specs=pl.BlockSpec((tm * nsl, LANES), row_map),
            scratch_shapes=[pltpu.VMEM((2, d, f), F32), pltpu.VMEM((2, d, f), F32),
                            pltpu.VMEM((2, f, d), F32),
                            pltpu.VMEM((d, f), BF16), pltpu.VMEM((d, f), BF16),
                            pltpu.VMEM((f, d), BF16),
                            pltpu.VMEM((tm, d), F32),
                            pltpu.SemaphoreType.DMA((2, 3))]),
        compiler_params=_params(("arbitrary",)),
        name="moe_experts",
    )(tile, expert, valid, slot, nxt, bounds, xs, w_gate, w_up, w_down)


def _combine_kernel(pos_ref, ys_ref, w_ref, hp_ref, x_ref, ada_ref, wsg_ref, wsu_ref, wsd_ref,
                    lg_ref, lb_ref, op_ref, os_ref, buf, ffn_scr, sem, *, tm, nsub, alpha, np_tiles):
    i = pl.program_id(0)
    nsl = hp_ref.shape[0] // tm

    def issue(r, c):
        base = (i * tm + r) * TOP_K
        dst_rows = _token_rows(r, nsl)
        for k in range(TOP_K):
            p = pos_ref[base + k]
            pltpu.make_async_copy(ys_ref.at[_token_rows(p, nsl), :], buf.at[k, dst_rows, :],
                                  sem).start(priority=k % 2)
        return c

    lax.fori_loop(0, tm, issue, 0)

    h16 = jnp.concatenate(_unpack_halves(_load_token_tiles(hp_ref, tm)), axis=1).astype(BF16)
    g = _dot(h16, wsg_ref[...])
    u = _dot(h16, wsu_ref[...])
    shared = _dot((g * jax.nn.sigmoid(g) * u).astype(BF16), wsd_ref[...])

    for k in range(TOP_K):
        pltpu.make_async_copy(ys_ref.at[pl.ds(0, tm * nsl), :], buf.at[k], sem).wait()
    ffn_scr[...] = shared
    half = ffn_scr.shape[1] // 2
    rblk = min(tm, 128)
    for r0 in range(0, tm, rblk):
        wb = [jnp.broadcast_to(w_ref[r0:r0 + rblk, k:k + 1], (rblk, LANES)) for k in range(TOP_K)]
        for s in range(nsl):
            lo = hi = None
            for k in range(TOP_K):
                a, b = _unpack_halves(buf[k, pl.ds(r0 * nsl + s, rblk, stride=nsl), :])
                lo = wb[k] * a if lo is None else lo + wb[k] * a
                hi = wb[k] * b if hi is None else hi + wb[k] * b
            ffn_scr[r0:r0 + rblk, s * LANES:(s + 1) * LANES] += lo
            ffn_scr[r0:r0 + rblk, half + s * LANES:half + (s + 1) * LANES] += hi

    def finish(o_ref):
        for s in range(nsub):
            rows = slice(s * ADA_BLOCK, (s + 1) * ADA_BLOCK)
            g2 = ada_ref[s, 5:6, :]
            y = alpha * x_ref[rows, :] + (1.0 + g2) * ffn_scr[rows, :]
            o_ref[rows, :] = _layer_norm(y, lg_ref[...], lb_ref[...])

    @pl.when(i < np_tiles)
    def _():
        finish(op_ref)

    @pl.when(i >= np_tiles)
    def _():
        finish(os_ref)


def _combine(pos_flat, ys, w_tok, h2p, x1, ada_blk, wsg16, wsu16, wsd16, ln_g, ln_b, *,
             alpha, n_p, tm=256):
    n, d = x1.shape
    nsl = d // 2 // LANES
    f = wsg16.shape[1]
    nsub = tm // ADA_BLOCK
    np_tiles = n_p // tm
    return pl.pallas_call(
        functools.partial(_combine_kernel, tm=tm, nsub=nsub, alpha=alpha, np_tiles=np_tiles),
        out_shape=(jax.ShapeDtypeStruct((n_p, d), F32), jax.ShapeDtypeStruct((n - n_p, d), F32)),
        grid_spec=pltpu.PrefetchScalarGridSpec(
            num_scalar_prefetch=1,
            grid=(n // tm,),
            in_specs=[pl.BlockSpec(memory_space=pl.ANY),
                      pl.BlockSpec((tm, TOP_K), lambda i, pos: (i, 0)),
                      pl.BlockSpec((tm * nsl, LANES), lambda i, pos: (i, 0)),
                      pl.BlockSpec((tm, d), lambda i, pos: (i, 0)),
                      pl.BlockSpec((nsub, 6, d), lambda i, pos: (i, 0, 0)),
                      pl.BlockSpec((d, f), lambda i, pos: (0, 0)),
                      pl.BlockSpec((d, f), lambda i, pos: (0, 0)),
                      pl.BlockSpec((f, d), lambda i, pos: (0, 0)),
                      pl.BlockSpec((1, d), lambda i, pos: (0, 0)),
                      pl.BlockSpec((1, d), lambda i, pos: (0, 0))],
            out_specs=(pl.BlockSpec((tm, d), lambda i, pos: (jnp.minimum(i, np_tiles - 1), 0)),
                       pl.BlockSpec((tm, d), lambda i, pos: (jnp.maximum(i - np_tiles, 0), 0))),
            scratch_shapes=[pltpu.VMEM((TOP_K, tm * nsl, LANES), ys.dtype),
                            pltpu.VMEM((tm, d), F32),
                            pltpu.SemaphoreType.DMA(())]),
        compiler_params=_params(("arbitrary",)),
        name="moe_combine",
    )(pos_flat, ys, w_tok, h2p, x1, ada_blk, wsg16, wsu16, wsd16, ln_g, ln_b)


def _layer(x_p, x_s, c_pad, blk_batch, cache_k, cache_v, layer, past_logf, st_re, st_im, lp, dims):
    bp, tp, bs, ts, d = dims
    n_p, n_s = bp * tp, bs * ts
    n = n_p + n_s
    ds = d // 2
    n_h = ds // FOX_HEAD_DIM
    g_n, p_n = lp["ssm_lambda_re"].shape
    n_e = lp["w_router"].shape[1]
    depth_alpha = lp["alpha"]

    ada = _ada(c_pad, lp["w_ada"], lp["b_ada"])
    ada_blk = ada.reshape(ada.shape[0], 6, d)[blk_batch]

    w_in = lp["w_in"]
    w_main = jnp.concatenate([w_in[:, :4 * ds], w_in[:, 4 * ds + n_h:]], axis=1).astype(BF16)
    wf_pad = jnp.pad(w_in[:, 4 * ds:4 * ds + n_h], ((0, 0), (0, LANES - n_h))).astype(BF16)
    zu, kv_p, kv_s, gates, qkv16, logf_t = _inproj(x_p, x_s, ada_blk, w_main, wf_pad,
                                                   lp["b_f"].reshape(n_h, 1))

    lbr, lbi, bbr, bbi = _ssm_disc(lp["ssm_lambda_re"], lp["ssm_lambda_im"], lp["ssm_log_dt"],
                                   lp["ssm_b_re"], lp["ssm_b_im"])
    lam_lay = _to_state_layout(lbr[None], lbi[None])
    bd = jnp.concatenate([_block_diag_slabs(bbr.transpose(0, 2, 1)),
                          _block_diag_slabs(bbi.transpose(0, 2, 1))], axis=2).astype(BF16)
    cd = jnp.stack([_block_diag_slabs(lp["ssm_c_re"].transpose(0, 2, 1)),
                    _block_diag_slabs(lp["ssm_c_im"].transpose(0, 2, 1))], axis=1).astype(BF16)
    d_row = lp["ssm_d"].reshape(1, ds)
    x0_p = jnp.zeros((bp, 2 * g_n * p_n), F32)
    x0_s = _to_state_layout(st_re, st_im)
    g_p, xl_p = _ssm(zu, x0_p, lam_lay, bd, cd, d_row, row0=0, n_seq=bp, seq_len=tp,
                     s_blk=bp, tt=min(128, tp))
    g_s, xl_s = _ssm(zu, x0_s, lam_lay, bd, cd, d_row, row0=n_p, n_seq=bs, seq_len=ts,
                     s_blk=min(8, bs), tt=ts)
    g16 = jnp.concatenate(g_p + g_s, axis=0)
    br_ssm = _glu(g16, lp["w_glu"].astype(BF16))

    lf_p = logf_t[:, :n_p].reshape(n_h, bp, tp).transpose(1, 0, 2)
    lf_s = logf_t[:, n_p:].reshape(n_h, bs, ts).transpose(1, 0, 2)
    nh_p = 4 if n_h % 4 == 0 else 2
    ck_p = _cumsum_lanes(lf_p.reshape(bp * n_h, tp)).reshape(bp, n_h // nh_p, nh_p, tp)
    past = past_logf.shape[1]
    cat = jnp.concatenate([past_logf.astype(F32).transpose(0, 2, 1), lf_s], axis=2)
    width = -(-(past + ts) // 256) * 256
    cat = jnp.pad(cat, ((0, 0), (0, 0), (0, width - past - ts)))
    nh_s = 4
    ck_s = _cumsum_lanes(cat.reshape(bs * n_h, width)).reshape(bs, n_h // nh_s, nh_s, width)
    attn_p = _attn_prompt(qkv16, ck_p, n_batch=bp, seq_len=tp, n_h=n_h, nh=nh_p, tq=min(512, tp))
    attn_s = _attn_sample(qkv16, cache_k[layer].reshape(bs, past, ds).astype(BF16),
                          cache_v[layer].reshape(bs, past, ds).astype(BF16),
                          ck_s[..., :past], ck_s[..., past:past + ts],
                          row0=n_p, n_batch=bs, seq_len=ts, n_h=n_h, nh=nh_s)

    x1 = _post(attn_p, attn_s, br_ssm, gates, x_p, x_s, ada_blk, lp["w_fox_o"].astype(BF16),
               lp["w_out"].astype(BF16), lp["ln1_g"].reshape(1, d), lp["ln1_b"].reshape(1, d),
               alpha=depth_alpha)

    gsz = n_e // N_EXPERT_GROUPS
    perm = lambda a: a.reshape(N_EXPERT_GROUPS, gsz, -1).transpose(1, 0, 2).reshape(n_e, -1)
    h2p, eidx_t, w_t = _router(x1, ada_blk, perm(lp["w_router"].T), perm(lp["router_bias"].reshape(n_e, 1)))
    tm_e = 256
    pos_t, starts, vis = _plan(eidx_t, n_e, tm_e)
    pos_flat = pos_t.T.reshape(n * TOP_K)
    bounds = jnp.concatenate([starts[:, 0], jnp.full((1,), n * TOP_K, I32)])
    xs = _dispatch(pos_flat, h2p, n)
    ys = _gmm(vis[0], vis[1], vis[2], vis[3], vis[4], bounds, xs, lp["w_exp_gate"], lp["w_exp_up"],
              lp["w_exp_down"], tm=tm_e)
    x2_p, x2_s = _combine(pos_flat, ys, w_t.T, h2p, x1, ada_blk, lp["w_sh_gate"].astype(BF16),
                          lp["w_sh_up"].astype(BF16), lp["w_sh_down"].astype(BF16),
                          lp["ln2_g"].reshape(1, d), lp["ln2_b"].reshape(1, d),
                          alpha=depth_alpha, n_p=n_p)

    k_new, v_new = (kv_p[0], kv_s[0]), (kv_p[1], kv_s[1])
    logf = logf_t.T
    ssm_p = _from_state_layout(xl_p, g_n, p_n)
    ssm_s = _from_state_layout(xl_s, g_n, p_n)
    return x2_p, x2_s, k_new, v_new, logf, ssm_p, ssm_s


def kernel(x_prompt, x_sample, cache_k, cache_v, cache_logf, state_ssm_re, state_ssm_im, c_prompt, c_sample, w_ada, b_ada, w_in, b_f, ssm_lambda_re, ssm_lambda_im, ssm_log_dt, ssm_b_re, ssm_b_im, ssm_c_re, ssm_c_im, ssm_d, w_glu, w_fox_o, w_out, ln1_g, ln1_b, w_router, router_bias, w_exp_gate, w_exp_up, w_exp_down, w_sh_gate, w_sh_up, w_sh_down, ln2_g, ln2_b):
    bp, tp, d = x_prompt.shape
    bs, ts, _ = x_sample.shape
    depth = w_ada.shape[0]
    n_p, n_s = bp * tp, bs * ts
    n_h = d // 2 // FOX_HEAD_DIM
    assert tp % ADA_BLOCK == 0 and ts == ADA_BLOCK
    alpha = (2.0 * depth) ** 0.25

    x_p, x_s = x_prompt.reshape(n_p, d), x_sample.reshape(n_s, d)
    c_all = jnp.concatenate([c_prompt, c_sample], axis=0)
    c_pad = jnp.pad(c_all, ((0, -(bp + bs) % 16), (0, 0)))
    blk_batch = np.concatenate([np.repeat(np.arange(bp), tp // ADA_BLOCK),
                                bp + np.repeat(np.arange(bs), ts // ADA_BLOCK)])
    dims = (bp, tp, bs, ts, d)
    outs_p, outs_s = [], []
    for l in range(depth):
        lp = dict(w_ada=w_ada[l], b_ada=b_ada[l], w_in=w_in[l], b_f=b_f[l],
                  ssm_lambda_re=ssm_lambda_re[l], ssm_lambda_im=ssm_lambda_im[l],
                  ssm_log_dt=ssm_log_dt[l], ssm_b_re=ssm_b_re[l], ssm_b_im=ssm_b_im[l],
                  ssm_c_re=ssm_c_re[l], ssm_c_im=ssm_c_im[l], ssm_d=ssm_d[l], w_glu=w_glu[l],
                  w_fox_o=w_fox_o[l], w_out=w_out[l], ln1_g=ln1_g[l], ln1_b=ln1_b[l],
                  w_router=w_router[l], router_bias=router_bias[l], w_exp_gate=w_exp_gate[l],
                  w_exp_up=w_exp_up[l], w_exp_down=w_exp_down[l], w_sh_gate=w_sh_gate[l],
                  w_sh_up=w_sh_up[l], w_sh_down=w_sh_down[l], ln2_g=ln2_g[l], ln2_b=ln2_b[l],
                  alpha=alpha)
        x_p, x_s, k_new, v_new, logf, ssm_p, ssm_s = _layer(
            x_p, x_s, c_pad, blk_batch, cache_k, cache_v, l, cache_logf[l],
            state_ssm_re[l].astype(F32), state_ssm_im[l].astype(F32), lp, dims)
        hd = FOX_HEAD_DIM
        outs_p.append((k_new[0].reshape(bp, tp, n_h, hd), v_new[0].reshape(bp, tp, n_h, hd),
                       logf[:n_p].reshape(bp, tp, n_h), ssm_p[0], ssm_p[1]))
        outs_s.append((k_new[1].reshape(bs, ts, n_h, hd), v_new[1].reshape(bs, ts, n_h, hd),
                       logf[n_p:].reshape(bs, ts, n_h), ssm_s[0], ssm_s[1]))
    stack = lambda outs, i: jnp.stack([o[i] for o in outs])
    return (x_p.reshape(bp, tp, d), x_s.reshape(bs, ts, d),
            stack(outs_p, 0), stack(outs_p, 1), stack(outs_p, 2), stack(outs_p, 3), stack(outs_p, 4),
            stack(outs_s, 0), stack(outs_s, 1), stack(outs_s, 2), stack(outs_s, 3), stack(outs_s, 4))
```

```python
import functools

import jax
import jax.numpy as jnp
import numpy as np
from jax import lax
from jax.experimental import pallas as pl
from jax.experimental.pallas import tpu as pltpu

F32 = jnp.float32
BF16 = jnp.bfloat16
I32 = jnp.int32

V7X_VMEM_BYTES = 64 * 1024 * 1024
VMEM_LIMIT_BYTES = V7X_VMEM_BYTES - 8 * 1024 * 1024
LANES = 128
SUBLANES = 8

SSM_GROUP_WIDTH = 16
SSM_STATE = 64
SSM_GROUPS_PER_SLAB = 8
FOX_HEAD_DIM = 64
N_EXPERT_GROUPS = 8
TOPK_EXPERT_GROUPS = 4
TOP_K = 8
ROUTED_SCALE = 2.5
LN_EPS = 1e-5
ADA_BLOCK = 64

NT_DIMS = (((1,), (1,)), ((), ()))


def _params(sem, vmem=VMEM_LIMIT_BYTES):
    return pltpu.CompilerParams(dimension_semantics=sem, vmem_limit_bytes=vmem)


def _dot(a, b):
    return jnp.dot(a, b, preferred_element_type=F32)


def _dot_nt(a, b):
    return lax.dot_general(a, b, NT_DIMS, preferred_element_type=F32)


def _split_bf16(x):
    hi = x.astype(BF16)
    lo = (x - hi.astype(F32)).astype(BF16)
    return hi, lo


def _log_sigmoid(x):
    return jnp.minimum(x, 0.0) - jnp.log1p(jnp.exp(-jnp.abs(x)))


def _gelu_tanh(x):
    c = np.float32(np.sqrt(2.0 / np.pi))
    return x * (0.5 * (1.0 + jnp.tanh(c * (x + 0.044715 * (x * x * x)))))


def _layer_norm(y, g, b):
    mu = jnp.mean(y, axis=-1, keepdims=True)
    yc = y - mu
    var = jnp.mean(yc * yc, axis=-1, keepdims=True)
    return yc * lax.rsqrt(var + LN_EPS) * g + b


def _ada_kernel(c_ref, w_ref, b_ref, o_ref):
    c = c_ref[...]
    a_hi, a_lo = _split_bf16(c * jax.nn.sigmoid(c))
    w_hi, w_lo = _split_bf16(w_ref[...])
    acc = _dot(a_hi, w_lo) + _dot(a_lo, w_hi)
    o_ref[...] = acc + _dot(a_hi, w_hi) + b_ref[...]


def _ada(c_pad, w_ada, b_ada):
    m, d = c_pad.shape
    n = w_ada.shape[1]
    tn = 1024
    return pl.pallas_call(
        _ada_kernel,
        out_shape=jax.ShapeDtypeStruct((m, n), F32),
        grid=(n // tn,),
        in_specs=[pl.BlockSpec((m, d), lambda j: (0, 0)),
                  pl.BlockSpec((d, tn), lambda j: (0, j)),
                  pl.BlockSpec((1, tn), lambda j: (0, j))],
        out_specs=pl.BlockSpec((m, tn), lambda j: (0, j)),
        compiler_params=_params(("arbitrary",)),
        name="ada",
    )(c_pad, w_ada, b_ada.reshape(1, n))


def _inproj_kernel(xp_ref, xs_ref, ada_ref, w_ref, wf_ref, bf_ref,
                   u_ref, kvp_ref, kvs_ref, g_ref, qkv_ref, lf_ref, h_scr,
                   *, nsub, q_scale, np_tiles):
    i = pl.program_id(0)
    j = pl.program_id(1)

    def modulate(x_ref):
        for s in range(nsub):
            rows = slice(s * ADA_BLOCK, (s + 1) * ADA_BLOCK)
            sh = ada_ref[s, 0:1, :]
            sc = ada_ref[s, 1:2, :]
            h_scr[rows, :] = (x_ref[rows, :] * (1.0 + sc) + sh).astype(BF16)

    @pl.when((j == 0) & (i < np_tiles))
    def _():
        modulate(xp_ref)

    @pl.when((j == 0) & (i >= np_tiles))
    def _():
        modulate(xs_ref)

    @pl.when(j == 0)
    def _():
        f_t = _dot(h_scr[...], wf_ref[...]).T
        lf_ref[...] = _log_sigmoid(f_t[:lf_ref.shape[0], :] + bf_ref[...])

    zt = _dot(h_scr[...], w_ref[...])

    @pl.when(j == 0)
    def _():
        u_ref[...] = zt

    @pl.when(j == 1)
    def _():
        qkv_ref[...] = (zt * q_scale).astype(BF16)

    @pl.when((j == 2) | (j == 3))
    def _():
        qkv_ref[...] = zt.astype(BF16)

    @pl.when(((j == 2) | (j == 3)) & (i < np_tiles))
    def _():
        kvp_ref[0] = zt

    @pl.when(((j == 2) | (j == 3)) & (i >= np_tiles))
    def _():
        kvs_ref[0] = zt

    @pl.when(j >= 4)
    def _():
        g_ref[...] = zt.astype(g_ref.dtype)


def _two_part_specs(tm, d, np_tiles, n_grid_args):
    if n_grid_args == 1:
        return [pl.BlockSpec((tm, d), lambda i: (jnp.minimum(i, np_tiles - 1), 0)),
                pl.BlockSpec((tm, d), lambda i: (jnp.maximum(i - np_tiles, 0), 0))]
    return [pl.BlockSpec((tm, d), lambda i, j: (jnp.minimum(i, np_tiles - 1), 0)),
            pl.BlockSpec((tm, d), lambda i, j: (jnp.maximum(i - np_tiles, 0), 0))]


def _inproj(x_p, x_s, ada_blk, w_main, wf_pad, bf_col, tm=512):
    n_p, d = x_p.shape
    n = n_p + x_s.shape[0]
    ds = d // 2
    h = bf_col.shape[0]
    nsub = tm // ADA_BLOCK
    ncol = w_main.shape[1] // ds
    np_tiles = n_p // tm
    kern = functools.partial(_inproj_kernel, nsub=nsub, q_scale=FOX_HEAD_DIM ** -0.5,
                             np_tiles=np_tiles)
    return pl.pallas_call(
        kern,
        out_shape=(jax.ShapeDtypeStruct((n, ds), F32),
                   jax.ShapeDtypeStruct((2, n_p, ds), F32),
                   jax.ShapeDtypeStruct((2, n - n_p, ds), F32),
                   jax.ShapeDtypeStruct((n, 2 * d), BF16),
                   jax.ShapeDtypeStruct((n, 3 * ds), BF16),
                   jax.ShapeDtypeStruct((h, n), F32)),
        grid=(n // tm, ncol),
        in_specs=_two_part_specs(tm, d, np_tiles, 2) + [
                  pl.BlockSpec((nsub, 6, d), lambda i, j: (i, 0, 0)),
                  pl.BlockSpec((d, ds), lambda i, j: (0, j)),
                  pl.BlockSpec((d, LANES), lambda i, j: (0, 0)),
                  pl.BlockSpec((h, 1), lambda i, j: (0, 0))],
        out_specs=(pl.BlockSpec((tm, ds), lambda i, j: (i, 0)),
                   pl.BlockSpec((1, tm, ds), lambda i, j: (
                       jnp.where(i < np_tiles, jnp.clip(j - 2, 0, 1), 1),
                       jnp.minimum(i, np_tiles - 1), 0)),
                   pl.BlockSpec((1, tm, ds), lambda i, j: (
                       jnp.where(i >= np_tiles, jnp.clip(j - 2, 0, 1), 0),
                       jnp.maximum(i - np_tiles, 0), 0)),
                   pl.BlockSpec((tm, ds), lambda i, j: (i, jnp.clip(j - 4, 0, ncol - 5))),
                   pl.BlockSpec((tm, ds), lambda i, j: (i, jnp.clip(j - 1, 0, 2))),
                   pl.BlockSpec((h, tm), lambda i, j: (0, i))),
        scratch_shapes=[pltpu.VMEM((tm, d), BF16)],
        compiler_params=_params(("arbitrary", "arbitrary")),
        name="inproj",
    )(x_p, x_s, ada_blk, w_main, wf_pad, bf_col)


def _ssm_disc_kernel(lr_ref, li_ref, ldt_ref, br_ref, bi_ref,
                     lbr_ref, lbi_ref, bbr_ref, bbi_ref):
    lr = jnp.minimum(lr_ref[...], -1e-4)
    li = li_ref[...]
    dt = jnp.exp(ldt_ref[...])
    er = jnp.exp(lr * dt)
    lbr = er * jnp.cos(li * dt)
    lbi = er * jnp.sin(li * dt)
    lbr_ref[...] = lbr
    lbi_ref[...] = lbi
    nr = lbr - 1.0
    den = lr * lr + li * li
    qr = (nr * lr + lbi * li) / den
    qi = (lbi * lr - nr * li) / den
    b_r = br_ref[...]
    b_i = bi_ref[...]
    bbr_ref[...] = qr * b_r - qi * b_i
    bbi_ref[...] = qr * b_i + qi * b_r


def _ssm_disc(lam_re, lam_im, log_dt, b_re, b_im):
    g, p, w = b_re.shape
    rep = lambda a: jnp.repeat(a, w, axis=1)
    shp = jax.ShapeDtypeStruct((g, p * w), F32)
    lbr, lbi, bbr, bbi = pl.pallas_call(
        _ssm_disc_kernel, out_shape=(shp, shp, shp, shp), name="ssm_disc",
    )(rep(lam_re), rep(lam_im), log_dt.reshape(g, 1),
      b_re.reshape(g, p * w), b_im.reshape(g, p * w))
    return (lbr[:, ::w], lbi[:, ::w], bbr.reshape(g, p, w), bbi.reshape(g, p, w))


def _to_state_layout(re, im):
    s, g, p = re.shape
    ns = g // SSM_GROUPS_PER_SLAB
    r = re.reshape(s, ns, 1, SSM_GROUPS_PER_SLAB * p)
    i = im.reshape(s, ns, 1, SSM_GROUPS_PER_SLAB * p)
    return jnp.concatenate([r, i], axis=2).reshape(s, 2 * g * p)


def _from_state_layout(x, g, p):
    s = x.shape[0]
    y = x.reshape(s, g // SSM_GROUPS_PER_SLAB, 2, SSM_GROUPS_PER_SLAB, p)
    return y[:, :, 0].reshape(s, g, p), y[:, :, 1].reshape(s, g, p)


def _block_diag_slabs(a):
    g, m, n = a.shape
    k = SSM_GROUPS_PER_SLAB
    a4 = a.reshape(g // k, k, m, n)
    eye = jnp.eye(k, dtype=bool)
    out = jnp.where(eye[None, :, None, :, None], a4[:, :, :, None, :], 0.0)
    return out.reshape(g // k, k * m, k * n)


def _ssm_kernel(*refs, n_refs, rpr, s_blk, tt, n_slab, sw):
    u_refs = refs[:n_refs]
    x0_ref, lam_ref, bd_ref, cd_ref, d_ref = refs[n_refs:n_refs + 5]
    o_refs = refs[n_refs + 5:2 * n_refs + 5]
    xl_ref = refs[2 * n_refs + 5]
    u_tm, bu, y_tm, st = refs[2 * n_refs + 6:]
    tb = pl.program_id(1)
    uw = SSM_GROUPS_PER_SLAB * SSM_GROUP_WIDTH
    npl = sw // LANES
    nph = npl // 2
    k_sub = SUBLANES // s_blk
    n_rows = s_blk * tt

    def seq_view(refs_, s):
        return (refs_[s], slice(None)) if n_refs == s_blk else (refs_[0], slice(s * tt, (s + 1) * tt))

    @pl.when(tb == 0)
    def _():
        st[...] = jnp.zeros(st.shape, F32)
        st[0:s_blk, :] = x0_ref[...]

    for s in range(s_blk):
        ref, rows = seq_view(u_refs, s)
        for j in range(n_slab):
            u_tm[j, pl.ds(s, tt, stride=s_blk), :] = ref[rows, j * uw:(j + 1) * uw]
    for j in range(n_slab):
        res = _dot(u_tm[j].astype(BF16), bd_ref[j])
        for q in range(npl):
            bu[j * npl + q] = res[:, q * LANES:(q + 1) * LANES]

    sub = lax.broadcasted_iota(I32, (SUBLANES, LANES), 0)
    for j in range(n_slab):
        c0 = j * sw
        a = [jnp.broadcast_to(lam_ref[:, c0 + q * LANES:c0 + (q + 1) * LANES], (SUBLANES, LANES))
             for q in range(npl)]

        def body(i, carry, j=j, a=a):
            rows = pl.ds(pl.multiple_of(i * SUBLANES, SUBLANES), SUBLANES)
            x = list(carry)
            b = [bu[j * npl + q, rows, :] for q in range(npl)]
            out = [None] * npl
            for step in range(k_sub):
                win = (sub >= step * s_blk) & (sub < (step + 1) * s_blk)
                for q in range(nph):
                    xr, xi = x[q], x[nph + q]
                    nr = a[q] * xr - a[nph + q] * xi + b[q]
                    ni = a[q] * xi + a[nph + q] * xr + b[nph + q]
                    out[q] = nr if step == 0 else jnp.where(win, nr, out[q])
                    out[nph + q] = ni if step == 0 else jnp.where(win, ni, out[nph + q])
                    if k_sub > 1:
                        nr = pltpu.roll(nr, s_blk, 0)
                        ni = pltpu.roll(ni, s_blk, 0)
                    x[q], x[nph + q] = nr, ni
            for q in range(npl):
                bu[j * npl + q, rows, :] = out[q]
            return tuple(x)

        init = tuple(st[:, c0 + q * LANES:c0 + (q + 1) * LANES] for q in range(npl))
        fin = lax.fori_loop(0, n_rows // SUBLANES, body, init, unroll=4)
        for q in range(npl):
            st[:, c0 + q * LANES:c0 + (q + 1) * LANES] = fin[q]

    for j in range(n_slab):
        xr16 = jnp.concatenate([bu[j * npl + q] for q in range(nph)], axis=1).astype(BF16)
        xi16 = jnp.concatenate([bu[j * npl + nph + q] for q in range(nph)], axis=1).astype(BF16)
        y_tm[j] = _dot(xr16, cd_ref[j, 0]) - _dot(xi16, cd_ref[j, 1])
    for s in range(s_blk):
        ref, rows = seq_view(u_refs, s)
        o_ref, o_rows = seq_view(o_refs, s)
        for j in range(n_slab):
            cols = slice(j * uw, (j + 1) * uw)
            y = y_tm[j, pl.ds(s, tt, stride=s_blk), :] + d_ref[:, cols] * ref[rows, cols]
            o_ref[o_rows, cols] = _gelu_tanh(y).astype(BF16)

    @pl.when(tb == pl.num_programs(1) - 1)
    def _():
        xl_ref[...] = st[0:s_blk, :]


def _ssm(z32, x0_lay, lam_lay, bd, cd, d_row, *, row0, n_seq, seq_len, s_blk, tt):
    n_slab = bd.shape[0]
    uw = bd.shape[1]
    sw = bd.shape[2]
    ds = n_slab * uw
    state_w = n_slab * sw
    n_tb = seq_len // tt
    n_sg = n_seq // s_blk
    if n_tb == 1:
        n_refs, rpr = 1, s_blk * tt
        assert row0 % rpr == 0
        in_maps = [lambda sg, tb: (row0 // rpr + sg, 0)]
        out_shape = [jax.ShapeDtypeStruct((n_seq * seq_len, ds), BF16)]
        out_maps = [lambda sg, tb: (sg, 0)]
    else:
        assert n_sg == 1 and row0 == 0
        n_refs, rpr = s_blk, tt
        in_maps = [functools.partial(lambda sg, tb, s: (s * n_tb + tb, 0), s=s) for s in range(s_blk)]
        out_shape = [jax.ShapeDtypeStruct((seq_len, ds), BF16)] * s_blk
        out_maps = [lambda sg, tb: (tb, 0)] * s_blk
    rows = n_refs * rpr
    kern = functools.partial(_ssm_kernel, n_refs=n_refs, rpr=rpr, s_blk=s_blk, tt=tt,
                             n_slab=n_slab, sw=sw)
    outs = pl.pallas_call(
        kern,
        out_shape=tuple(out_shape) + (jax.ShapeDtypeStruct((n_seq, state_w), F32),),
        grid=(n_sg, n_tb),
        in_specs=[pl.BlockSpec((rpr, ds), m) for m in in_maps] + [
            pl.BlockSpec((s_blk, state_w), lambda sg, tb: (sg, 0)),
            pl.BlockSpec((1, state_w), lambda sg, tb: (0, 0)),
            pl.BlockSpec(bd.shape, lambda sg, tb: (0, 0, 0)),
            pl.BlockSpec(cd.shape, lambda sg, tb: (0, 0, 0, 0)),
            pl.BlockSpec((1, ds), lambda sg, tb: (0, 0))],
        out_specs=tuple(pl.BlockSpec((rpr, ds), m) for m in out_maps) + (
            pl.BlockSpec((s_blk, state_w), lambda sg, tb: (sg, 0)),),
        scratch_shapes=[pltpu.VMEM((n_slab, rows, uw), F32),
                        pltpu.VMEM((state_w // LANES, rows, LANES), F32),
                        pltpu.VMEM((n_slab, rows, uw), F32),
                        pltpu.VMEM((SUBLANES, state_w), F32)],
        compiler_params=_params(("arbitrary", "arbitrary")),
        name="ssm",
    )(*([z32] * n_refs), x0_lay, lam_lay, bd, cd, d_row)
    return list(outs[:-1]), outs[-1]


def _cumsum_kernel(x_ref, o_ref, *, blk):
    r, t = x_ref.shape
    row = lax.broadcasted_iota(I32, (blk, blk), 0)
    col = lax.broadcasted_iota(I32, (blk, blk), 1)
    upper = jnp.where(row <= col, 1.0, 0.0).astype(BF16)
    carry = jnp.zeros((r, 1), F32)
    for c in range(t // blk):
        x = x_ref[:, c * blk:(c + 1) * blk]
        h1 = x.astype(BF16)
        r1 = x - h1.astype(F32)
        h2 = r1.astype(BF16)
        h3 = (r1 - h2.astype(F32)).astype(BF16)
        s = (_dot(h3, upper) + _dot(h2, upper)) + _dot(h1, upper) + carry
        o_ref[:, c * blk:(c + 1) * blk] = s
        carry = s[:, blk - 1:blk]


def _cumsum_lanes(x, blk=256, tr=64):
    r, t = x.shape
    tr = min(tr, r)
    return pl.pallas_call(
        functools.partial(_cumsum_kernel, blk=blk),
        out_shape=jax.ShapeDtypeStruct((r, t), F32),
        grid=(r // tr,),
        in_specs=[pl.BlockSpec((tr, t), lambda i: (i, 0))],
        out_specs=pl.BlockSpec((tr, t), lambda i: (i, 0)),
        compiler_params=_params(("arbitrary",)),
        name="cumsum",
    )(x)


def _head_lanes(nh):
    lane = lax.broadcasted_iota(I32, (1, nh * FOX_HEAD_DIM), 1)
    return [(lane >= h * FOX_HEAD_DIM) & (lane < (h + 1) * FOX_HEAD_DIM) for h in range(nh)]


def _per_head(sels, vals):
    reps = len(sels) * FOX_HEAD_DIM // LANES
    out = jnp.tile(vals[0], (1, reps)) if reps > 1 else vals[0]
    for sel, v in zip(sels[1:], vals[1:]):
        out = jnp.where(sel, jnp.tile(v, (1, reps)) if reps > 1 else v, out)
    return out


def _attn_step(q, k, v, ck, m_scr, l_scr, acc_scr, mask):
    nh = m_scr.shape[0]
    sels = _head_lanes(nh)
    pv, alphas = None, []
    for h in range(nh):
        qh = jnp.where(sels[h], q, jnp.zeros_like(q))
        s = _dot_nt(qh, k) - ck[h:h + 1, :]
        if mask is not None:
            s = jnp.where(mask, s, -jnp.inf)
        m_prev = m_scr[h]
        m_new = jnp.maximum(m_prev, jnp.max(s, axis=-1, keepdims=True))
        alpha = jnp.exp(m_prev - m_new)
        p = jnp.exp(s - m_new[:, :1])
        l_scr[h] = alpha * l_scr[h] + jnp.sum(p, axis=-1, keepdims=True)
        m_scr[h] = m_new
        pv_h = _dot(p.astype(BF16), v)
        pv = pv_h if pv is None else jnp.where(sels[h], pv_h, pv)
        alphas.append(alpha)
    acc_scr[...] = _per_head(sels, alphas) * acc_scr[...] + pv


def _attn_init(m_scr, l_scr, acc_scr):
    m_scr[...] = jnp.full(m_scr.shape, -jnp.inf, F32)
    l_scr[...] = jnp.zeros(l_scr.shape, F32)
    acc_scr[...] = jnp.zeros(acc_scr.shape, F32)


def _attn_finish(o_ref, l_scr, acc_scr):
    nh = l_scr.shape[0]
    l = _per_head(_head_lanes(nh), [l_scr[h] for h in range(nh)])
    o_ref[...] = (acc_scr[...] / l).astype(o_ref.dtype)


def _causal_mask(tq, tk):
    return (lax.broadcasted_iota(I32, (tq, tk), 1) <= lax.broadcasted_iota(I32, (tq, tk), 0))


def _attn_prompt_kernel(qt_ref, kt_ref, q_ref, k_ref, v_ref, ck_ref, o_ref, m_scr, l_scr, acc_scr,
                        *, tq):
    step = pl.program_id(2)
    qi, ki = qt_ref[step], kt_ref[step]

    @pl.when(ki == 0)
    def _():
        _attn_init(m_scr, l_scr, acc_scr)

    @pl.when(ki < qi)
    def _():
        _attn_step(q_ref[...], k_ref[...], v_ref[...], ck_ref[0, 0], m_scr, l_scr, acc_scr, None)

    @pl.when(ki == qi)
    def _():
        _attn_step(q_ref[...], k_ref[...], v_ref[...], ck_ref[0, 0], m_scr, l_scr, acc_scr,
                   _causal_mask(tq, tq))
        _attn_finish(o_ref, l_scr, acc_scr)


def _attn_prompt(qkv16, ck, *, n_batch, seq_len, n_h, nh, tq=512):
    nq = seq_len // tq
    lw = nh * FOX_HEAD_DIM
    ng = n_h // nh
    pairs = [(qi, ki) for qi in range(nq) for ki in range(qi + 1)]
    qt = jnp.asarray([p[0] for p in pairs], I32)
    kt = jnp.asarray([p[1] for p in pairs], I32)
    q_map = lambda b, g, s, qt, kt: (b * nq + qt[s], g)
    kv_map = lambda c: (lambda b, g, s, qt, kt: (b * nq + kt[s], c * ng + g))
    return pl.pallas_call(
        functools.partial(_attn_prompt_kernel, tq=tq),
        out_shape=jax.ShapeDtypeStruct((n_batch * seq_len, ng * lw), BF16),
        grid_spec=pltpu.PrefetchScalarGridSpec(
            num_scalar_prefetch=2,
            grid=(n_batch, ng, len(pairs)),
            in_specs=[pl.BlockSpec((tq, lw), q_map),
                      pl.BlockSpec((tq, lw), kv_map(1)),
                      pl.BlockSpec((tq, lw), kv_map(2)),
                      pl.BlockSpec((1, 1, nh, tq), lambda b, g, s, qt, kt: (b, g, 0, kt[s]))],
            out_specs=pl.BlockSpec((tq, lw), q_map),
            scratch_shapes=[pltpu.VMEM((nh, tq, LANES), F32), pltpu.VMEM((nh, tq, LANES), F32),
                            pltpu.VMEM((tq, lw), F32)]),
        compiler_params=_params(("arbitrary",) * 3),
        name="attn_prompt",
    )(qt, kt, qkv16, qkv16, qkv16, ck)


def _attn_sample_kernel(q_ref, kp_ref, vp_ref, kn_ref, vn_ref, ckp_ref, ckn_ref, o_ref, *, ts, nh):
    hd = FOX_HEAD_DIM
    lw = nh * hd
    lane = lax.broadcasted_iota(I32, (1, lw), 1)
    sels = [(lane >= h * hd) & (lane < (h + 1) * hd) for h in range(nh)]
    q = q_ref[...]
    qbd = jnp.concatenate([jnp.where(sels[h], q, jnp.zeros_like(q)) for h in range(nh)], axis=0)

    def update(state, k, v, ck, mask):
        m_prev, l_prev, acc = state
        s = _dot_nt(qbd, k)
        rows = []
        for h in range(nh):
            sh = s[h * ts:(h + 1) * ts, :] - ck[h:h + 1, :]
            rows.append(sh if mask is None else jnp.where(mask, sh, -jnp.inf))
        s = jnp.concatenate(rows, axis=0)
        m_new = jnp.maximum(m_prev, jnp.max(s, axis=-1, keepdims=True))
        alpha = jnp.exp(m_prev - m_new)
        p = jnp.exp(s - m_new)
        l_new = alpha * l_prev + jnp.sum(p, axis=-1, keepdims=True)
        return m_new, l_new, alpha * acc + _dot(p.astype(BF16), v)

    state = (jnp.full((nh * ts, 1), -jnp.inf, F32), jnp.zeros((nh * ts, 1), F32),
             jnp.zeros((nh * ts, lw), F32))
    state = update(state, kp_ref[0].astype(BF16), vp_ref[0].astype(BF16), ckp_ref[0, 0], None)
    _, l, acc = update(state, kn_ref[...], vn_ref[...], ckn_ref[0, 0], _causal_mask(ts, ts))
    res = acc / l
    out = res[0:ts, :]
    for h in range(1, nh):
        out = jnp.where(sels[h], res[h * ts:(h + 1) * ts, :], out)
    o_ref[...] = out.astype(o_ref.dtype)


def _attn_sample(qkv16, k_past, v_past, ck_past, ck_new, *, row0, n_batch, seq_len, n_h, nh=4):
    past = k_past.shape[1]
    lw = nh * FOX_HEAD_DIM
    ng = n_h // nh
    rb0 = row0 // seq_len
    new_map = lambda c: (lambda b, g: (rb0 + b, c * ng + g))
    return pl.pallas_call(
        functools.partial(_attn_sample_kernel, ts=seq_len, nh=nh),
        out_shape=jax.ShapeDtypeStruct((n_batch * seq_len, ng * lw), BF16),
        grid=(n_batch, ng),
        in_specs=[pl.BlockSpec((seq_len, lw), new_map(0)),
                  pl.BlockSpec((1, past, lw), lambda b, g: (b, 0, g)),
                  pl.BlockSpec((1, past, lw), lambda b, g: (b, 0, g)),
                  pl.BlockSpec((seq_len, lw), new_map(1)),
                  pl.BlockSpec((seq_len, lw), new_map(2)),
                  pl.BlockSpec((1, 1, nh, past), lambda b, g: (b, g, 0, 0)),
                  pl.BlockSpec((1, 1, nh, seq_len), lambda b, g: (b, g, 0, 0))],
        out_specs=pl.BlockSpec((seq_len, lw), lambda b, g: (b, g)),
        compiler_params=_params(("arbitrary",) * 2),
        name="attn_sample",
    )(qkv16, k_past, v_past, qkv16, qkv16, ck_past, ck_new)


def _glu_kernel(g_ref, wa_ref, wb_ref, o_ref):
    g = g_ref[...]
    o_ref[...] = (_dot(g, wa_ref[...]) * jax.nn.sigmoid(_dot(g, wb_ref[...]))).astype(o_ref.dtype)


def _glu(g16, w_glu16, tm=512, tn=1024):
    n, ds = g16.shape
    d = w_glu16.shape[1] // 2
    tn = min(tn, d)
    nb = d // tn
    return pl.pallas_call(
        _glu_kernel,
        out_shape=jax.ShapeDtypeStruct((n, d), BF16),
        grid=(n // tm, nb),
        in_specs=[pl.BlockSpec((tm, ds), lambda i, j: (i, 0)),
                  pl.BlockSpec((ds, tn), lambda i, j: (0, j)),
                  pl.BlockSpec((ds, tn), lambda i, j: (0, nb + j))],
        out_specs=pl.BlockSpec((tm, tn), lambda i, j: (i, j)),
        compiler_params=_params(("arbitrary", "arbitrary")),
        name="glu",
    )(g16, w_glu16, w_glu16)


def _post_kernel(attp_ref, atts_ref, brs_ref, gs_ref, gf_ref, xp_ref, xs_ref, ada_ref, wfo_ref,
                 wo_ref, lg_ref, lb_ref, o_ref, *, nsub, alpha, np_tiles):
    i = pl.program_id(0)
    att = jnp.where(i < np_tiles, attp_ref[...], atts_ref[...])
    br_fox = _dot(att, wfo_ref[...])
    merged = (jax.nn.sigmoid(gs_ref[...].astype(F32)) * brs_ref[...].astype(F32)
              + jax.nn.sigmoid(gf_ref[...].astype(F32)) * br_fox)
    mix = _dot(merged.astype(BF16), wo_ref[...])

    def finish(x_ref):
        for s in range(nsub):
            rows = slice(s * ADA_BLOCK, (s + 1) * ADA_BLOCK)
            g1 = ada_ref[s, 2:3, :]
            y = alpha * x_ref[rows, :] + (1.0 + g1) * mix[rows, :]
            o_ref[rows, :] = _layer_norm(y, lg_ref[...], lb_ref[...])

    @pl.when(i < np_tiles)
    def _():
        finish(xp_ref)

    @pl.when(i >= np_tiles)
    def _():
        finish(xs_ref)


def _post(attn_p, attn_s, br_ssm, gates, x_p, x_s, ada_blk, w_fox16, w_out16, ln_g, ln_b, *,
          alpha, tm=256):
    n_p, d = x_p.shape
    n = n_p + x_s.shape[0]
    ds = d // 2
    nsub = tm // ADA_BLOCK
    np_tiles = n_p // tm
    return pl.pallas_call(
        functools.partial(_post_kernel, nsub=nsub, alpha=alpha, np_tiles=np_tiles),
        out_shape=jax.ShapeDtypeStruct((n, d), F32),
        grid=(n // tm,),
        in_specs=_two_part_specs(tm, ds, np_tiles, 1) + [
                  pl.BlockSpec((tm, d), lambda i: (i, 0)),
                  pl.BlockSpec((tm, d), lambda i: (i, 0)),
                  pl.BlockSpec((tm, d), lambda i: (i, 1)),
                  ] + _two_part_specs(tm, d, np_tiles, 1) + [
                  pl.BlockSpec((nsub, 6, d), lambda i: (i, 0, 0)),
                  pl.BlockSpec((ds, d), lambda i: (0, 0)),
                  pl.BlockSpec((d, d), lambda i: (0, 0)),
                  pl.BlockSpec((1, d), lambda i: (0, 0)),
                  pl.BlockSpec((1, d), lambda i: (0, 0))],
        out_specs=pl.BlockSpec((tm, d), lambda i: (i, 0)),
        compiler_params=_params(("arbitrary",)),
        name="post_mix",
    )(attn_p, attn_s, br_ssm, gates, gates, x_p, x_s, ada_blk, w_fox16, w_out16, ln_g, ln_b)


def _pack_halves(x):
    c = x.shape[1] // 2
    return pltpu.pack_elementwise([x[:, :c], x[:, c:]], packed_dtype=BF16)


def _unpack_halves(w):
    return tuple(pltpu.unpack_elementwise(w, index=i, packed_dtype=BF16, unpacked_dtype=F32)
                 for i in range(2))


def _store_token_tiles(ref, words):
    m, c = words.shape
    nsl = c // LANES
    for s in range(nsl):
        ref[pl.ds(s, m, stride=nsl), :] = words[:, s * LANES:(s + 1) * LANES]


def _load_token_tiles(ref, m):
    nsl = ref.shape[0] // m
    return jnp.concatenate([ref[pl.ds(s, m, stride=nsl), :] for s in range(nsl)], axis=1)


def _token_rows(t, nsl):
    start = t * nsl
    if nsl % SUBLANES == 0:
        start = pl.multiple_of(start, SUBLANES)
    return pl.ds(start, nsl)


def _router_kernel(x_ref, ada_ref, wr_ref, rb_ref, hp_ref, e_ref, w_ref, h_scr, *, nsub):
    ng = N_EXPERT_GROUPS
    for s in range(nsub):
        rows = slice(s * ADA_BLOCK, (s + 1) * ADA_BLOCK)
        h_scr[rows, :] = x_ref[rows, :] * (1.0 + ada_ref[s, 4:5, :]) + ada_ref[s, 3:4, :]
    _store_token_tiles(hp_ref, _pack_halves(h_scr[...]))
    h_hi, h_lo = _split_bf16(h_scr[...])
    w_hi, w_lo = _split_bf16(wr_ref[...])
    logits = (_dot_nt(w_hi, h_lo) + _dot_nt(w_lo, h_hi)) + _dot_nt(w_hi, h_hi)
    scores = jax.nn.sigmoid(logits)
    sel = scores + rb_ref[...]
    gsz = sel.shape[0] // ng
    tm = sel.shape[1]
    xs = [sel[j * ng:(j + 1) * ng, :] for j in range(gsz)]
    sc = [scores[j * ng:(j + 1) * ng, :] for j in range(gsz)]
    neg = -jnp.inf

    def lmax(v):
        out = v[0]
        for a in v[1:]:
            out = jnp.maximum(out, a)
        return out

    def lmin(v):
        out = v[0]
        for a in v[1:]:
            out = jnp.minimum(out, a)
        return out

    m1 = lmax(xs)
    i1 = lmin([jnp.where(xs[j] == m1, j, gsz) for j in range(gsz)])
    m2 = lmax([jnp.where(i1 == j, neg, xs[j]) for j in range(gsz)])
    cur = m1 + m2
    giota = lax.broadcasted_iota(I32, (ng, tm), 0)
    gsel = jnp.zeros((ng, tm), F32)
    for _ in range(TOPK_EXPERT_GROUPS):
        m = jnp.max(cur, axis=0, keepdims=True)
        gi = jnp.min(jnp.where(cur == m, giota, ng), axis=0, keepdims=True)
        hit = giota == gi
        gsel = jnp.where(hit, 1.0, gsel)
        cur = jnp.where(hit, neg, cur)
    gmask = gsel > 0.0
    xs = [jnp.where(gmask, x, neg) for x in xs]
    eid = [giota * gsz + j for j in range(gsz)]
    n_e = ng * gsz
    vals = []
    for r in range(TOP_K):
        m = jnp.max(lmax(xs), axis=0, keepdims=True)
        ci = jnp.min(lmin([jnp.where(xs[j] == m, eid[j], n_e) for j in range(gsz)]),
                     axis=0, keepdims=True)
        hits = [eid[j] == ci for j in range(gsz)]
        v = sum(jnp.where(hits[j], sc[j], 0.0) for j in range(gsz))
        vals.append(jnp.sum(v, axis=0, keepdims=True))
        xs = [jnp.where(hits[j], neg, xs[j]) for j in range(gsz)]
        e_ref[r:r + 1, :] = ci
    tot = sum(vals)
    for r in range(TOP_K):
        w_ref[r:r + 1, :] = vals[r] / tot * ROUTED_SCALE


def _router(x1, ada_blk, wr_perm, rb_perm, tm=512):
    n, d = x1.shape
    e = wr_perm.shape[0]
    nsub = tm // ADA_BLOCK
    return pl.pallas_call(
        functools.partial(_router_kernel, nsub=nsub),
        out_shape=(jax.ShapeDtypeStruct((n * (d // 2 // LANES), LANES), jnp.uint32),
                   jax.ShapeDtypeStruct((TOP_K, n), I32),
                   jax.ShapeDtypeStruct((TOP_K, n), F32)),
        grid=(n // tm,),
        in_specs=[pl.BlockSpec((tm, d), lambda i: (i, 0)),
                  pl.BlockSpec((nsub, 6, d), lambda i: (i, 0, 0)),
                  pl.BlockSpec((e, d), lambda i: (0, 0)),
                  pl.BlockSpec((e, 1), lambda i: (0, 0))],
        out_specs=(pl.BlockSpec((tm * (d // 2 // LANES), LANES), lambda i: (i, 0)),
                   pl.BlockSpec((TOP_K, tm), lambda i: (0, i)),
                   pl.BlockSpec((TOP_K, tm), lambda i: (0, i))),
        scratch_shapes=[pltpu.VMEM((tm, d), F32)],
        compiler_params=_params(("arbitrary",)),
        name="router",
    )(x1, ada_blk, wr_perm, rb_perm)


def _plan_kernel(e_ref, pos_ref, st_ref, vis_ref, rank_scr, *, n_e, blk, tm):
    n = e_ref.shape[1]
    nblk = n // blk
    row = lax.broadcasted_iota(I32, (blk, blk), 0)
    col = lax.broadcasted_iota(I32, (blk, blk), 1)
    upper = jnp.where(row <= col, 1.0, 0.0).astype(BF16)
    eid = lax.broadcasted_iota(I32, (n_e, blk), 0)

    def count_body(cb, carry):
        cols = pl.ds(pl.multiple_of(cb * blk, blk), blk)
        e_blk = e_ref[:, cols]
        hit = jnp.zeros((n_e, blk), F32)
        for k in range(TOP_K):
            hit = hit + jnp.where(e_blk[k:k + 1, :] == eid, 1.0, 0.0)
        cs = _dot(hit.astype(BF16), upper) + carry
        rank_scr[:, cols] = cs - hit
        return cs[:, blk - 1:blk]

    counts = lax.fori_loop(0, nblk, count_body, jnp.zeros((n_e, 1), F32))

    hi = jnp.floor(counts * (1.0 / 128.0))
    lo = counts - hi * 128.0
    er = lax.broadcasted_iota(I32, (n_e, n_e), 0)
    ec = lax.broadcasted_iota(I32, (n_e, n_e), 1)
    lower = jnp.where(ec < er, 1.0, 0.0).astype(BF16)
    lower_incl = jnp.where(ec <= er, 1.0, 0.0).astype(BF16)
    wide = lambda v: jnp.broadcast_to(v, (n_e, LANES)).astype(BF16)
    starts = (_dot(lower, wide(hi)) * 128.0 + _dot(lower, wide(lo)))[:, :1]
    st_ref[...] = jnp.broadcast_to(starts, st_ref.shape).astype(I32)

    inv_tm = 1.0 / tm
    nonempty = counts > 0.0
    first_t = jnp.floor(starts * inv_tm)
    nvis = jnp.where(nonempty, jnp.floor((starts + counts - 1.0) * inv_tm) - first_t + 1.0, 0.0)
    vend = _dot(lower_incl, wide(nvis))[:, :1]
    gidx = _dot(lower_incl, wide(jnp.where(nonempty, 1.0, 0.0)))[:, :1] - 1.0
    total = jnp.max(vend, axis=0, keepdims=True)
    nv = vis_ref.shape[1]
    viota = lax.broadcasted_iota(I32, (1, nv), 1).astype(F32)
    vc = jnp.minimum(viota, total - 1.0)
    e_v = jnp.sum(jnp.where(vend <= vc, 1.0, 0.0), axis=0, keepdims=True)
    eio = lax.broadcasted_iota(I32, (n_e, nv), 0).astype(F32)
    mine = eio == e_v
    pick = lambda colv: jnp.sum(jnp.where(mine, colv, 0.0), axis=0, keepdims=True)
    tile_v = pick(first_t) + (vc - pick(vend - nvis))
    g_v = pick(gidx)
    slot_v = g_v - 2.0 * jnp.floor(g_v * 0.5)
    is_next = nonempty & (gidx == g_v + 1.0)
    nxt_v = (jnp.sum(jnp.where(is_next, eio + 1.0, 0.0), axis=0, keepdims=True) - 1.0)
    rows = [tile_v, e_v, jnp.where(viota < total, 1.0, 0.0), slot_v, nxt_v]
    rows += [jnp.zeros((1, nv), F32)] * (vis_ref.shape[0] - len(rows))
    vis_ref[...] = jnp.concatenate(rows, axis=0).astype(I32)

    def pos_body(cb, c):
        cols = pl.ds(pl.multiple_of(cb * blk, blk), blk)
        e_blk = e_ref[:, cols]
        val = rank_scr[:, cols] + starts
        for k in range(TOP_K):
            p = jnp.sum(jnp.where(e_blk[k:k + 1, :] == eid, val, 0.0), axis=0, keepdims=True)
            pos_ref[k:k + 1, cols] = p.astype(I32)
        return c

    lax.fori_loop(0, nblk, pos_body, 0)


def _plan(eidx_t, n_e, tm, blk=256):
    k, n = eidx_t.shape
    n_visits = (n * k) // tm + n_e - 1
    nv = -(-n_visits // LANES) * LANES
    pos, starts, vis = pl.pallas_call(
        functools.partial(_plan_kernel, n_e=n_e, blk=blk, tm=tm),
        out_shape=(jax.ShapeDtypeStruct((k, n), I32), jax.ShapeDtypeStruct((n_e, LANES), I32),
                   jax.ShapeDtypeStruct((SUBLANES, nv), I32)),
        scratch_shapes=[pltpu.VMEM((n_e, n), F32)],
        compiler_params=pltpu.CompilerParams(vmem_limit_bytes=VMEM_LIMIT_BYTES),
        name="moe_plan",
    )(eidx_t)
    return pos, starts, vis[:, :n_visits]


def _dispatch_kernel(pos_ref, h_ref, xs_ref, sem, *, tm):
    i = pl.program_id(0)
    nsl = h_ref.shape[0] // tm

    def issue(r, c):
        base = (i * tm + r) * TOP_K
        src = h_ref.at[_token_rows(r, nsl), :]
        for k in range(TOP_K):
            p = pos_ref[base + k]
            pltpu.make_async_copy(src, xs_ref.at[_token_rows(p, nsl), :],
                                  sem).start(priority=k % 2)
        return c

    lax.fori_loop(0, tm, issue, 0)
    for k in range(TOP_K):
        pltpu.make_async_copy(h_ref, xs_ref.at[pl.ds(0, tm * nsl), :], sem).wait()


def _dispatch(pos_flat, h2t, n, tm=256):
    rows, lanes = h2t.shape
    nsl = rows // n
    return pl.pallas_call(
        functools.partial(_dispatch_kernel, tm=tm),
        out_shape=jax.ShapeDtypeStruct((rows * TOP_K, lanes), h2t.dtype),
        grid_spec=pltpu.PrefetchScalarGridSpec(
            num_scalar_prefetch=1,
            grid=(n // tm,),
            in_specs=[pl.BlockSpec((tm * nsl, lanes), lambda i, pos: (i, 0))],
            out_specs=pl.BlockSpec(memory_space=pl.ANY),
            scratch_shapes=[pltpu.SemaphoreType.DMA(())]),
        compiler_params=_params(("arbitrary",)),
        name="moe_dispatch",
    )(pos_flat, h2t)


def _gmm_kernel(vt_ref, ve_ref, vv_ref, vs_ref, nx_ref, bd_ref,
                xs_ref, wg_hbm, wu_hbm, wd_hbm, ys_ref,
                wg32, wu32, wd32, wg16, wu16, wd16, acc, sem, *, tm, chain):
    v = pl.program_id(0)
    t = vt_ref[v]
    e = ve_ref[v]
    slot = vs_ref[v]
    pv = jnp.maximum(v - 1, 0)
    first = v == 0
    valid = vv_ref[v] == 1
    r0 = t * tm
    lo = bd_ref[e]
    hi = bd_ref[e + 1]
    whole = (r0 >= lo) & (r0 + tm <= hi)

    def weight_copies(expert, s):
        return (pltpu.make_async_copy(wg_hbm.at[expert], wg32.at[s], sem.at[s, 0]),
                pltpu.make_async_copy(wu_hbm.at[expert], wu32.at[s], sem.at[s, 1]),
                pltpu.make_async_copy(wd_hbm.at[expert], wd32.at[s], sem.at[s, 2]))

    @pl.when(first)
    def _():
        for c in weight_copies(e, slot):
            c.start()

    @pl.when(first | (e != ve_ref[pv]))
    def _():
        for c in weight_copies(e, slot):
            c.wait()
        nxt = nx_ref[v]

        @pl.when(nxt >= 0)
        def _():
            for c in weight_copies(nxt, 1 - slot):
                c.start()

        wg16[...] = wg32[slot].astype(BF16)
        wu16[...] = wu32[slot].astype(BF16)
        wd16[...] = wd32[slot].astype(BF16)

    @pl.when(valid & jnp.logical_not(whole) & (first | (t != vt_ref[pv])))
    def _():
        acc[...] = jnp.zeros(acc.shape, F32)

    @pl.when(valid)
    def _():
        xw = _load_token_tiles(xs_ref, tm)
        parts = []
        for c0 in range(0, tm, chain):
            x = jnp.concatenate(_unpack_halves(xw[c0:c0 + chain]), axis=1).astype(BF16)
            g = _dot(x, wg16[...])
            u = _dot(x, wu16[...])
            act = (g * jax.nn.sigmoid(g) * u).astype(BF16)
            parts.append(_dot(act, wd16[...]))
        y = parts[0] if len(parts) == 1 else jnp.concatenate(parts, axis=0)

        @pl.when(whole)
        def _():
            _store_token_tiles(ys_ref, _pack_halves(y))

        @pl.when(jnp.logical_not(whole))
        def _():
            row = r0 + lax.broadcasted_iota(I32, (tm, 1), 0)
            acc[...] += jnp.where((row >= lo) & (row < hi), y, 0.0)
            _store_token_tiles(ys_ref, _pack_halves(acc[...]))


def _gmm(tile, expert, valid, slot, nxt, bounds, xs, w_gate, w_up, w_down, tm=256):
    n_e, d, f = w_gate.shape
    nsl = d // 2 // LANES
    n_visits = tile.shape[0]
    row_map = lambda v, vt, ve, vv, vs, nx, bd: (vt[v], 0)
    return pl.pallas_call(
        functools.partial(_gmm_kernel, tm=tm, chain=min(tm, 256)),
        out_shape=jax.ShapeDtypeStruct(xs.shape, xs.dtype),
        grid_spec=pltpu.PrefetchScalarGridSpec(
            num_scalar_prefetch=6,
            grid=(n_visits,),
            in_specs=[pl.BlockSpec((tm * nsl, LANES), row_map),
                      pl.BlockSpec(memory_space=pl.ANY),
                      pl.BlockSpec(memory_space=pl.ANY),
                      pl.BlockSpec(memory_space=pl.ANY)],
            out_specs=pl.BlockSpec((tm * nsl, LANES), row_map),
            scratch_shapes=[pltpu.VMEM((2, d, f), F32), pltpu.VMEM((2, d, f), F32),
                            pltpu.VMEM((2, f, d), F32),
                            pltpu.VMEM((d, f), BF16), pltpu.VMEM((d, f), BF16),
                            pltpu.VMEM((f, d), BF16),
                            pltpu.VMEM((tm, d), F32),
                            pltpu.SemaphoreType.DMA((2, 3))]),
        compiler_params=_params(("arbitrary",)),
        name="moe_experts",
    )(tile, expert, valid, slot, nxt, bounds, xs, w_gate, w_up, w_down)


def _combine_kernel(pos_ref, ys_ref, w_ref, hp_ref, x_ref, ada_ref, wsg_ref, wsu_ref, wsd_ref,
                    lg_ref, lb_ref, op_ref, os_ref, buf, ffn_scr, sem, *, tm, nsub, alpha, np_tiles):
    i = pl.program_id(0)
    nsl = hp_ref.shape[0] // tm

    def issue(r, c):
        base = (i * tm + r) * TOP_K
        dst_rows = _token_rows(r, nsl)
        for k in range(TOP_K):
            p = pos_ref[base + k]
            pltpu.make_async_copy(ys_ref.at[_token_rows(p, nsl), :], buf.at[k, dst_rows, :],
                                  sem).start(priority=k % 2)
        return c

    lax.fori_loop(0, tm, issue, 0)

    h16 = jnp.concatenate(_unpack_halves(_load_token_tiles(hp_ref, tm)), axis=1).astype(BF16)
    g = _dot(h16, wsg_ref[...])
    u = _dot(h16, wsu_ref[...])
    shared = _dot((g * jax.nn.sigmoid(g) * u).astype(BF16), wsd_ref[...])

    for k in range(TOP_K):
        pltpu.make_async_copy(ys_ref.at[pl.ds(0, tm * nsl), :], buf.at[k], sem).wait()
    ffn_scr[...] = shared
    half = ffn_scr.shape[1] // 2
    rblk = min(tm, 128)
    for r0 in range(0, tm, rblk):
        wb = [jnp.broadcast_to(w_ref[r0:r0 + rblk, k:k + 1], (rblk, LANES)) for k in range(TOP_K)]
        for s in range(nsl):
            lo = hi = None
            for k in range(TOP_K):
                a, b = _unpack_halves(buf[k, pl.ds(r0 * nsl + s, rblk, stride=nsl), :])
                lo = wb[k] * a if lo is None else lo + wb[k] * a
                hi = wb[k] * b if hi is None else hi + wb[k] * b
            ffn_scr[r0:r0 + rblk, s * LANES:(s + 1) * LANES] += lo
            ffn_scr[r0:r0 + rblk, half + s * LANES:half + (s + 1) * LANES] += hi

    def finish(o_ref):
        for s in range(nsub):
            rows = slice(s * ADA_BLOCK, (s + 1) * ADA_BLOCK)
            g2 = ada_ref[s, 5:6, :]
            y = alpha * x_ref[rows, :] + (1.0 + g2) * ffn_scr[rows, :]
            o_ref[rows, :] = _layer_norm(y, lg_ref[...], lb_ref[...])

    @pl.when(i < np_tiles)
    def _():
        finish(op_ref)

    @pl.when(i >= np_tiles)
    def _():
        finish(os_ref)


def _combine(pos_flat, ys, w_tok, h2p, x1, ada_blk, wsg16, wsu16, wsd16, ln_g, ln_b, *,
             alpha, n_p, tm=256):
    n, d = x1.shape
    nsl = d // 2 // LANES
    f = wsg16.shape[1]
    nsub = tm // ADA_BLOCK
    np_tiles = n_p // tm
    return pl.pallas_call(
        functools.partial(_combine_kernel, tm=tm, nsub=nsub, alpha=alpha, np_tiles=np_tiles),
        out_shape=(jax.ShapeDtypeStruct((n_p, d), F32), jax.ShapeDtypeStruct((n - n_p, d), F32)),
        grid_spec=pltpu.PrefetchScalarGridSpec(
            num_scalar_prefetch=1,
            grid=(n // tm,),
            in_specs=[pl.BlockSpec(memory_space=pl.ANY),
                      pl.BlockSpec((tm, TOP_K), lambda i, pos: (i, 0)),
                      pl.BlockSpec((tm * nsl, LANES), lambda i, pos: (i, 0)),
                      pl.BlockSpec((tm, d), lambda i, pos: (i, 0)),
                      pl.BlockSpec((nsub, 6, d), lambda i, pos: (i, 0, 0)),
                      pl.BlockSpec((d, f), lambda i, pos: (0, 0)),
                      pl.BlockSpec((d, f), lambda i, pos: (0, 0)),
                      pl.BlockSpec((f, d), lambda i, pos: (0, 0)),
                      pl.BlockSpec((1, d), lambda i, pos: (0, 0)),
                      pl.BlockSpec((1, d), lambda i, pos: (0, 0))],
            out_specs=(pl.BlockSpec((tm, d), lambda i, pos: (jnp.minimum(i, np_tiles - 1), 0)),
                       pl.BlockSpec((tm, d), lambda i, pos: (jnp.maximum(i - np_tiles, 0), 0))),
            scratch_shapes=[pltpu.VMEM((TOP_K, tm * nsl, LANES), ys.dtype),
                            pltpu.VMEM((tm, d), F32),
                            pltpu.SemaphoreType.DMA(())]),
        compiler_params=_params(("arbitrary",)),
        name="moe_combine",
    )(pos_flat, ys, w_tok, h2p, x1, ada_blk, wsg16, wsu16, wsd16, ln_g, ln_b)


def _layer(x_p, x_s, c_pad, blk_batch, cache_k, cache_v, layer, past_logf, st_re, st_im, lp, dims):
    bp, tp, bs, ts, d = dims
    n_p, n_s = bp * tp, bs * ts
    n = n_p + n_s
    ds = d // 2
    n_h = ds // FOX_HEAD_DIM
    g_n, p_n = lp["ssm_lambda_re"].shape
    n_e = lp["w_router"].shape[1]
    depth_alpha = lp["alpha"]

    ada = _ada(c_pad, lp["w_ada"], lp["b_ada"])
    ada_blk = ada.reshape(ada.shape[0], 6, d)[blk_batch]

    w_in = lp["w_in"]
    w_main = jnp.concatenate([w_in[:, :4 * ds], w_in[:, 4 * ds + n_h:]], axis=1).astype(BF16)
    wf_pad = jnp.pad(w_in[:, 4 * ds:4 * ds + n_h], ((0, 0), (0, LANES - n_h))).astype(BF16)
    zu, kv_p, kv_s, gates, qkv16, logf_t = _inproj(x_p, x_s, ada_blk, w_main, wf_pad,
                                                   lp["b_f"].reshape(n_h, 1))

    lbr, lbi, bbr, bbi = _ssm_disc(lp["ssm_lambda_re"], lp["ssm_lambda_im"], lp["ssm_log_dt"],
                                   lp["ssm_b_re"], lp["ssm_b_im"])
    lam_lay = _to_state_layout(lbr[None], lbi[None])
    bd = jnp.concatenate([_block_diag_slabs(bbr.transpose(0, 2, 1)),
                          _block_diag_slabs(bbi.transpose(0, 2, 1))], axis=2).astype(BF16)
    cd = jnp.stack([_block_diag_slabs(lp["ssm_c_re"].transpose(0, 2, 1)),
                    _block_diag_slabs(lp["ssm_c_im"].transpose(0, 2, 1))], axis=1).astype(BF16)
    d_row = lp["ssm_d"].reshape(1, ds)
    x0_p = jnp.zeros((bp, 2 * g_n * p_n), F32)
    x0_s = _to_state_layout(st_re, st_im)
    g_p, xl_p = _ssm(zu, x0_p, lam_lay, bd, cd, d_row, row0=0, n_seq=bp, seq_len=tp,
                     s_blk=bp, tt=min(128, tp))
    g_s, xl_s = _ssm(zu, x0_s, lam_lay, bd, cd, d_row, row0=n_p, n_seq=bs, seq_len=ts,
                     s_blk=min(8, bs), tt=ts)
    g16 = jnp.concatenate(g_p + g_s, axis=0)
    br_ssm = _glu(g16, lp["w_glu"].astype(BF16))

    lf_p = logf_t[:, :n_p].reshape(n_h, bp, tp).transpose(1, 0, 2)
    lf_s = logf_t[:, n_p:].reshape(n_h, bs, ts).transpose(1, 0, 2)
    nh_p = 4 if n_h % 4 == 0 else 2
    ck_p = _cumsum_lanes(lf_p.reshape(bp * n_h, tp)).reshape(bp, n_h // nh_p, nh_p, tp)
    past = past_logf.shape[1]
    cat = jnp.concatenate([past_logf.astype(F32).transpose(0, 2, 1), lf_s], axis=2)
    width = -(-(past + ts) // 256) * 256
    cat = jnp.pad(cat, ((0, 0), (0, 0), (0, width - past - ts)))
    nh_s = 4
    ck_s = _cumsum_lanes(cat.reshape(bs * n_h, width)).reshape(bs, n_h // nh_s, nh_s, width)
    attn_p = _attn_prompt(qkv16, ck_p, n_batch=bp, seq_len=tp, n_h=n_h, nh=nh_p, tq=min(512, tp))
    attn_s = _attn_sample(qkv16, cache_k[layer].reshape(bs, past, ds),
                          cache_v[layer].reshape(bs, past, ds),
                          ck_s[..., :past], ck_s[..., past:past + ts],
                          row0=n_p, n_batch=bs, seq_len=ts, n_h=n_h, nh=nh_s)

    x1 = _post(attn_p, attn_s, br_ssm, gates, x_p, x_s, ada_blk, lp["w_fox_o"].astype(BF16),
               lp["w_out"].astype(BF16), lp["ln1_g"].reshape(1, d), lp["ln1_b"].reshape(1, d),
               alpha=depth_alpha)

    gsz = n_e // N_EXPERT_GROUPS
    perm = lambda a: a.reshape(N_EXPERT_GROUPS, gsz, -1).transpose(1, 0, 2).reshape(n_e, -1)
    h2p, eidx_t, w_t = _router(x1, ada_blk, perm(lp["w_router"].T), perm(lp["router_bias"].reshape(n_e, 1)))
    tm_e = 256
    pos_t, starts, vis = _plan(eidx_t, n_e, tm_e)
    pos_flat = pos_t.T.reshape(n * TOP_K)
    bounds = jnp.concatenate([starts[:, 0], jnp.full((1,), n * TOP_K, I32)])
    xs = _dispatch(pos_flat, h2p, n)
    ys = _gmm(vis[0], vis[1], vis[2], vis[3], vis[4], bounds, xs, lp["w_exp_gate"], lp["w_exp_up"],
              lp["w_exp_down"], tm=tm_e)
    x2_p, x2_s = _combine(pos_flat, ys, w_t.T, h2p, x1, ada_blk, lp["w_sh_gate"].astype(BF16),
                          lp["w_sh_up"].astype(BF16), lp["w_sh_down"].astype(BF16),
                          lp["ln2_g"].reshape(1, d), lp["ln2_b"].reshape(1, d),
                          alpha=depth_alpha, n_p=n_p)

    k_new, v_new = (kv_p[0], kv_s[0]), (kv_p[1], kv_s[1])
    logf = logf_t.T
    ssm_p = _from_state_layout(xl_p, g_n, p_n)
    ssm_s = _from_state_layout(xl_s, g_n, p_n)
    return x2_p, x2_s, k_new, v_new, logf, ssm_p, ssm_s


def kernel(x_prompt, x_sample, cache_k, cache_v, cache_logf, state_ssm_re, state_ssm_im, c_prompt, c_sample, w_ada, b_ada, w_in, b_f, ssm_lambda_re, ssm_lambda_im, ssm_log_dt, ssm_b_re, ssm_b_im, ssm_c_re, ssm_c_im, ssm_d, w_glu, w_fox_o, w_out, ln1_g, ln1_b, w_router, router_bias, w_exp_gate, w_exp_up, w_exp_down, w_sh_gate, w_sh_up, w_sh_down, ln2_g, ln2_b):
    bp, tp, d = x_prompt.shape
    bs, ts, _ = x_sample.shape
    depth = w_ada.shape[0]
    n_p, n_s = bp * tp, bs * ts
    n_h = d // 2 // FOX_HEAD_DIM
    assert tp % ADA_BLOCK == 0 and ts == ADA_BLOCK
    alpha = (2.0 * depth) ** 0.25

    x_p, x_s = x_prompt.reshape(n_p, d), x_sample.reshape(n_s, d)
    c_all = jnp.concatenate([c_prompt, c_sample], axis=0)
    c_pad = jnp.pad(c_all, ((0, -(bp + bs) % 16), (0, 0)))
    blk_batch = np.concatenate([np.repeat(np.arange(bp), tp // ADA_BLOCK),
                                bp + np.repeat(np.arange(bs), ts // ADA_BLOCK)])
    dims = (bp, tp, bs, ts, d)
    outs_p, outs_s = [], []
    for l in range(depth):
        lp = dict(w_ada=w_ada[l], b_ada=b_ada[l], w_in=w_in[l], b_f=b_f[l],
                  ssm_lambda_re=ssm_lambda_re[l], ssm_lambda_im=ssm_lambda_im[l],
                  ssm_log_dt=ssm_log_dt[l], ssm_b_re=ssm_b_re[l], ssm_b_im=ssm_b_im[l],
                  ssm_c_re=ssm_c_re[l], ssm_c_im=ssm_c_im[l], ssm_d=ssm_d[l], w_glu=w_glu[l],
                  w_fox_o=w_fox_o[l], w_out=w_out[l], ln1_g=ln1_g[l], ln1_b=ln1_b[l],
                  w_router=w_router[l], router_bias=router_bias[l], w_exp_gate=w_exp_gate[l],
                  w_exp_up=w_exp_up[l], w_exp_down=w_exp_down[l], w_sh_gate=w_sh_gate[l],
                  w_sh_up=w_sh_up[l], w_sh_down=w_sh_down[l], ln2_g=ln2_g[l], ln2_b=ln2_b[l],
                  alpha=alpha)
        x_p, x_s, k_new, v_new, logf, ssm_p, ssm_s = _layer(
            x_p, x_s, c_pad, blk_batch, cache_k, cache_v, l, cache_logf[l],
            state_ssm_re[l].astype(F32), state_ssm_im[l].astype(F32), lp, dims)
        hd = FOX_HEAD_DIM
        outs_p.append((k_new[0].reshape(bp, tp, n_h, hd), v_new[0].reshape(bp, tp, n_h, hd),
                       logf[:n_p].reshape(bp, tp, n_h), ssm_p[0], ssm_p[1]))
        outs_s.append((k_new[1].reshape(bs, ts, n_h, hd), v_new[1].reshape(bs, ts, n_h, hd),
                       logf[n_p:].reshape(bs, ts, n_h), ssm_s[0], ssm_s[1]))
    stack = lambda outs, i: jnp.stack([o[i] for o in outs])
    return (x_p.reshape(bp, tp, d), x_s.reshape(bs, ts, d),
            stack(outs_p, 0), stack(outs_p, 1), stack(outs_p, 2), stack(outs_p, 3), stack(outs_p, 4),
            stack(outs_s, 0), stack(outs_s, 1), stack(outs_s, 2), stack(outs_s, 3), stack(outs_s, 4))
```

```python
import functools

import jax
import jax.numpy as jnp
import numpy as np
from jax import lax
from jax.experimental import pallas as pl
from jax.experimental.pallas import tpu as pltpu

F32 = jnp.float32
BF16 = jnp.bfloat16
I32 = jnp.int32

V7X_VMEM_BYTES = 64 * 1024 * 1024
VMEM_LIMIT_BYTES = V7X_VMEM_BYTES - 8 * 1024 * 1024
LANES = 128
SUBLANES = 8

SSM_GROUP_WIDTH = 16
SSM_STATE = 64
SSM_GROUPS_PER_SLAB = 8
FOX_HEAD_DIM = 64
N_EXPERT_GROUPS = 8
TOPK_EXPERT_GROUPS = 4
TOP_K = 8
ROUTED_SCALE = 2.5
LN_EPS = 1e-5
ADA_BLOCK = 64

NT_DIMS = (((1,), (1,)), ((), ()))


def _params(sem, vmem=VMEM_LIMIT_BYTES):
    return pltpu.CompilerParams(dimension_semantics=sem, vmem_limit_bytes=vmem)


def _dot(a, b):
    return jnp.dot(a, b, preferred_element_type=F32)


def _dot_nt(a, b):
    return lax.dot_general(a, b, NT_DIMS, preferred_element_type=F32)


def _split_bf16(x):
    hi = x.astype(BF16)
    lo = (x - hi.astype(F32)).astype(BF16)
    return hi, lo


def _log_sigmoid(x):
    return jnp.minimum(x, 0.0) - jnp.log1p(jnp.exp(-jnp.abs(x)))


def _gelu_tanh(x):
    c = np.float32(np.sqrt(2.0 / np.pi))
    return x * (0.5 * (1.0 + jnp.tanh(c * (x + 0.044715 * (x * x * x)))))


def _layer_norm(y, g, b):
    mu = jnp.mean(y, axis=-1, keepdims=True)
    yc = y - mu
    var = jnp.mean(yc * yc, axis=-1, keepdims=True)
    return yc * lax.rsqrt(var + LN_EPS) * g + b


def _ada_kernel(c_ref, w_ref, b_ref, o_ref):
    c = c_ref[...]
    a_hi, a_lo = _split_bf16(c * jax.nn.sigmoid(c))
    w_hi, w_lo = _split_bf16(w_ref[...])
    acc = _dot(a_hi, w_lo) + _dot(a_lo, w_hi)
    o_ref[...] = acc + _dot(a_hi, w_hi) + b_ref[...]


def _ada(c_pad, w_ada, b_ada):
    m, d = c_pad.shape
    n = w_ada.shape[1]
    tn = 1024
    return pl.pallas_call(
        _ada_kernel,
        out_shape=jax.ShapeDtypeStruct((m, n), F32),
        grid=(n // tn,),
        in_specs=[pl.BlockSpec((m, d), lambda j: (0, 0)),
                  pl.BlockSpec((d, tn), lambda j: (0, j)),
                  pl.BlockSpec((1, tn), lambda j: (0, j))],
        out_specs=pl.BlockSpec((m, tn), lambda j: (0, j)),
        compiler_params=_params(("arbitrary",)),
        name="ada",
    )(c_pad, w_ada, b_ada.reshape(1, n))


def _inproj_kernel(xp_ref, xs_ref, ada_ref, w_ref, wf_ref, bf_ref,
                   u_ref, kvp_ref, kvs_ref, g_ref, qkv_ref, lf_ref, h_scr,
                   *, nsub, q_scale, np_tiles):
    i = pl.program_id(0)
    j = pl.program_id(1)

    def modulate(x_ref):
        for s in range(nsub):
            rows = slice(s * ADA_BLOCK, (s + 1) * ADA_BLOCK)
            sh = ada_ref[s, 0:1, :]
            sc = ada_ref[s, 1:2, :]
            h_scr[rows, :] = (x_ref[rows, :] * (1.0 + sc) + sh).astype(BF16)

    @pl.when((j == 0) & (i < np_tiles))
    def _():
        modulate(xp_ref)

    @pl.when((j == 0) & (i >= np_tiles))
    def _():
        modulate(xs_ref)

    @pl.when(j == 0)
    def _():
        f_t = _dot(h_scr[...], wf_ref[...]).T
        lf_ref[...] = _log_sigmoid(f_t[:lf_ref.shape[0], :] + bf_ref[...])

    zt = _dot(h_scr[...], w_ref[...])

    @pl.when(j == 0)
    def _():
        u_ref[...] = zt

    @pl.when(j == 1)
    def _():
        qkv_ref[...] = (zt * q_scale).astype(BF16)

    @pl.when((j == 2) | (j == 3))
    def _():
        qkv_ref[...] = zt.astype(BF16)

    @pl.when(((j == 2) | (j == 3)) & (i < np_tiles))
    def _():
        kvp_ref[0] = zt

    @pl.when(((j == 2) | (j == 3)) & (i >= np_tiles))
    def _():
        kvs_ref[0] = zt

    @pl.when(j >= 4)
    def _():
        g_ref[...] = zt.astype(g_ref.dtype)


def _two_part_specs(tm, d, np_tiles, n_grid_args):
    if n_grid_args == 1:
        return [pl.BlockSpec((tm, d), lambda i: (jnp.minimum(i, np_tiles - 1), 0)),
                pl.BlockSpec((tm, d), lambda i: (jnp.maximum(i - np_tiles, 0), 0))]
    return [pl.BlockSpec((tm, d), lambda i, j: (jnp.minimum(i, np_tiles - 1), 0)),
            pl.BlockSpec((tm, d), lambda i, j: (jnp.maximum(i - np_tiles, 0), 0))]


def _inproj(x_p, x_s, ada_blk, w_main, wf_pad, bf_col, tm=512):
    n_p, d = x_p.shape
    n = n_p + x_s.shape[0]
    ds = d // 2
    h = bf_col.shape[0]
    nsub = tm // ADA_BLOCK
    ncol = w_main.shape[1] // ds
    np_tiles = n_p // tm
    kern = functools.partial(_inproj_kernel, nsub=nsub, q_scale=FOX_HEAD_DIM ** -0.5,
                             np_tiles=np_tiles)
    return pl.pallas_call(
        kern,
        out_shape=(jax.ShapeDtypeStruct((n, ds), F32),
                   jax.ShapeDtypeStruct((2, n_p, ds), F32),
                   jax.ShapeDtypeStruct((2, n - n_p, ds), F32),
                   jax.ShapeDtypeStruct((n, 2 * d), BF16),
                   jax.ShapeDtypeStruct((n, 3 * ds), BF16),
                   jax.ShapeDtypeStruct((h, n), F32)),
        grid=(n // tm, ncol),
        in_specs=_two_part_specs(tm, d, np_tiles, 2) + [
                  pl.BlockSpec((nsub, 6, d), lambda i, j: (i, 0, 0)),
                  pl.BlockSpec((d, ds), lambda i, j: (0, j)),
                  pl.BlockSpec((d, LANES), lambda i, j: (0, 0)),
                  pl.BlockSpec((h, 1), lambda i, j: (0, 0))],
        out_specs=(pl.BlockSpec((tm, ds), lambda i, j: (i, 0)),
                   pl.BlockSpec((1, tm, ds), lambda i, j: (
                       jnp.where(i < np_tiles, jnp.clip(j - 2, 0, 1), 1),
                       jnp.minimum(i, np_tiles - 1), 0)),
                   pl.BlockSpec((1, tm, ds), lambda i, j: (
                       jnp.where(i >= np_tiles, jnp.clip(j - 2, 0, 1), 0),
                       jnp.maximum(i - np_tiles, 0), 0)),
                   pl.BlockSpec((tm, ds), lambda i, j: (i, jnp.clip(j - 4, 0, ncol - 5))),
                   pl.BlockSpec((tm, ds), lambda i, j: (i, jnp.clip(j - 1, 0, 2))),
                   pl.BlockSpec((h, tm), lambda i, j: (0, i))),
        scratch_shapes=[pltpu.VMEM((tm, d), BF16)],
        compiler_params=_params(("arbitrary", "arbitrary")),
        name="inproj",
    )(x_p, x_s, ada_blk, w_main, wf_pad, bf_col)


def _ssm_disc_kernel(lr_ref, li_ref, ldt_ref, br_ref, bi_ref,
                     lbr_ref, lbi_ref, bbr_ref, bbi_ref):
    lr = jnp.minimum(lr_ref[...], -1e-4)
    li = li_ref[...]
    dt = jnp.exp(ldt_ref[...])
    er = jnp.exp(lr * dt)
    lbr = er * jnp.cos(li * dt)
    lbi = er * jnp.sin(li * dt)
    lbr_ref[...] = lbr
    lbi_ref[...] = lbi
    nr = lbr - 1.0
    den = lr * lr + li * li
    qr = (nr * lr + lbi * li) / den
    qi = (lbi * lr - nr * li) / den
    b_r = br_ref[...]
    b_i = bi_ref[...]
    bbr_ref[...] = qr * b_r - qi * b_i
    bbi_ref[...] = qr * b_i + qi * b_r


def _ssm_disc(lam_re, lam_im, log_dt, b_re, b_im):
    g, p, w = b_re.shape
    rep = lambda a: jnp.repeat(a, w, axis=1)
    shp = jax.ShapeDtypeStruct((g, p * w), F32)
    lbr, lbi, bbr, bbi = pl.pallas_call(
        _ssm_disc_kernel, out_shape=(shp, shp, shp, shp), name="ssm_disc",
    )(rep(lam_re), rep(lam_im), log_dt.reshape(g, 1),
      b_re.reshape(g, p * w), b_im.reshape(g, p * w))
    return (lbr[:, ::w], lbi[:, ::w], bbr.reshape(g, p, w), bbi.reshape(g, p, w))


def _to_state_layout(re, im):
    s, g, p = re.shape
    ns = g // SSM_GROUPS_PER_SLAB
    r = re.reshape(s, ns, 1, SSM_GROUPS_PER_SLAB * p)
    i = im.reshape(s, ns, 1, SSM_GROUPS_PER_SLAB * p)
    return jnp.concatenate([r, i], axis=2).reshape(s, 2 * g * p)


def _from_state_layout(x, g, p):
    s = x.shape[0]
    y = x.reshape(s, g // SSM_GROUPS_PER_SLAB, 2, SSM_GROUPS_PER_SLAB, p)
    return y[:, :, 0].reshape(s, g, p), y[:, :, 1].reshape(s, g, p)


def _block_diag_slabs(a):
    g, m, n = a.shape
    k = SSM_GROUPS_PER_SLAB
    a4 = a.reshape(g // k, k, m, n)
    eye = jnp.eye(k, dtype=bool)
    out = jnp.where(eye[None, :, None, :, None], a4[:, :, :, None, :], 0.0)
    return out.reshape(g // k, k * m, k * n)


def _ssm_kernel(*refs, n_refs, rpr, s_blk, tt, n_slab, sw):
    u_refs = refs[:n_refs]
    x0_ref, lam_ref, bd_ref, cd_ref, d_ref = refs[n_refs:n_refs + 5]
    o_refs = refs[n_refs + 5:2 * n_refs + 5]
    xl_ref = refs[2 * n_refs + 5]
    u_tm, bu, y_tm, st = refs[2 * n_refs + 6:]
    tb = pl.program_id(1)
    uw = SSM_GROUPS_PER_SLAB * SSM_GROUP_WIDTH
    npl = sw // LANES
    nph = npl // 2
    k_sub = SUBLANES // s_blk
    n_rows = s_blk * tt

    def seq_view(refs_, s):
        return (refs_[s], slice(None)) if n_refs == s_blk else (refs_[0], slice(s * tt, (s + 1) * tt))

    @pl.when(tb == 0)
    def _():
        st[...] = jnp.zeros(st.shape, F32)
        st[0:s_blk, :] = x0_ref[...]

    for s in range(s_blk):
        ref, rows = seq_view(u_refs, s)
        for j in range(n_slab):
            u_tm[j, pl.ds(s, tt, stride=s_blk), :] = ref[rows, j * uw:(j + 1) * uw]
    for j in range(n_slab):
        res = _dot(u_tm[j].astype(BF16), bd_ref[j])
        for q in range(npl):
            bu[j * npl + q] = res[:, q * LANES:(q + 1) * LANES]

    sub = lax.broadcasted_iota(I32, (SUBLANES, LANES), 0)
    for j in range(n_slab):
        c0 = j * sw
        a = [jnp.broadcast_to(lam_ref[:, c0 + q * LANES:c0 + (q + 1) * LANES], (SUBLANES, LANES))
             for q in range(npl)]

        def body(i, carry, j=j, a=a):
            rows = pl.ds(pl.multiple_of(i * SUBLANES, SUBLANES), SUBLANES)
            x = list(carry)
            b = [bu[j * npl + q, rows, :] for q in range(npl)]
            out = [None] * npl
            for step in range(k_sub):
                win = (sub >= step * s_blk) & (sub < (step + 1) * s_blk)
                for q in range(nph):
                    xr, xi = x[q], x[nph + q]
                    nr = a[q] * xr - a[nph + q] * xi + b[q]
                    ni = a[q] * xi + a[nph + q] * xr + b[nph + q]
                    out[q] = nr if step == 0 else jnp.where(win, nr, out[q])
                    out[nph + q] = ni if step == 0 else jnp.where(win, ni, out[nph + q])
                    if k_sub > 1:
                        nr = pltpu.roll(nr, s_blk, 0)
                        ni = pltpu.roll(ni, s_blk, 0)
                    x[q], x[nph + q] = nr, ni
            for q in range(npl):
                bu[j * npl + q, rows, :] = out[q]
            return tuple(x)

        init = tuple(st[:, c0 + q * LANES:c0 + (q + 1) * LANES] for q in range(npl))
        fin = lax.fori_loop(0, n_rows // SUBLANES, body, init, unroll=4)
        for q in range(npl):
            st[:, c0 + q * LANES:c0 + (q + 1) * LANES] = fin[q]

    for j in range(n_slab):
        xr16 = jnp.concatenate([bu[j * npl + q] for q in range(nph)], axis=1).astype(BF16)
        xi16 = jnp.concatenate([bu[j * npl + nph + q] for q in range(nph)], axis=1).astype(BF16)
        y_tm[j] = _dot(xr16, cd_ref[j, 0]) - _dot(xi16, cd_ref[j, 1])
    for s in range(s_blk):
        ref, rows = seq_view(u_refs, s)
        o_ref, o_rows = seq_view(o_refs, s)
        for j in range(n_slab):
            cols = slice(j * uw, (j + 1) * uw)
            y = y_tm[j, pl.ds(s, tt, stride=s_blk), :] + d_ref[:, cols] * ref[rows, cols]
            o_ref[o_rows, cols] = _gelu_tanh(y).astype(BF16)

    @pl.when(tb == pl.num_programs(1) - 1)
    def _():
        xl_ref[...] = st[0:s_blk, :]


def _ssm(z32, x0_lay, lam_lay, bd, cd, d_row, *, row0, n_seq, seq_len, s_blk, tt):
    n_slab = bd.shape[0]
    uw = bd.shape[1]
    sw = bd.shape[2]
    ds = n_slab * uw
    state_w = n_slab * sw
    n_tb = seq_len // tt
    n_sg = n_seq // s_blk
    if n_tb == 1:
        n_refs, rpr = 1, s_blk * tt
        assert row0 % rpr == 0
        in_maps = [lambda sg, tb: (row0 // rpr + sg, 0)]
        out_shape = [jax.ShapeDtypeStruct((n_seq * seq_len, ds), BF16)]
        out_maps = [lambda sg, tb: (sg, 0)]
    else:
        assert n_sg == 1 and row0 == 0
        n_refs, rpr = s_blk, tt
        in_maps = [functools.partial(lambda sg, tb, s: (s * n_tb + tb, 0), s=s) for s in range(s_blk)]
        out_shape = [jax.ShapeDtypeStruct((seq_len, ds), BF16)] * s_blk
        out_maps = [lambda sg, tb: (tb, 0)] * s_blk
    rows = n_refs * rpr
    kern = functools.partial(_ssm_kernel, n_refs=n_refs, rpr=rpr, s_blk=s_blk, tt=tt,
                             n_slab=n_slab, sw=sw)
    outs = pl.pallas_call(
        kern,
        out_shape=tuple(out_shape) + (jax.ShapeDtypeStruct((n_seq, state_w), F32),),
        grid=(n_sg, n_tb),
        in_specs=[pl.BlockSpec((rpr, ds), m) for m in in_maps] + [
            pl.BlockSpec((s_blk, state_w), lambda sg, tb: (sg, 0)),
            pl.BlockSpec((1, state_w), lambda sg, tb: (0, 0)),
            pl.BlockSpec(bd.shape, lambda sg, tb: (0, 0, 0)),
            pl.BlockSpec(cd.shape, lambda sg, tb: (0, 0, 0, 0)),
            pl.BlockSpec((1, ds), lambda sg, tb: (0, 0))],
        out_specs=tuple(pl.BlockSpec((rpr, ds), m) for m in out_maps) + (
            pl.BlockSpec((s_blk, state_w), lambda sg, tb: (sg, 0)),),
        scratch_shapes=[pltpu.VMEM((n_slab, rows, uw), F32),
                        pltpu.VMEM((state_w // LANES, rows, LANES), F32),
                        pltpu.VMEM((n_slab, rows, uw), F32),
                        pltpu.VMEM((SUBLANES, state_w), F32)],
        compiler_params=_params(("arbitrary", "arbitrary")),
        name="ssm",
    )(*([z32] * n_refs), x0_lay, lam_lay, bd, cd, d_row)
    return list(outs[:-1]), outs[-1]


def _cumsum_kernel(x_ref, o_ref, *, blk):
    r, t = x_ref.shape
    row = lax.broadcasted_iota(I32, (blk, blk), 0)
    col = lax.broadcasted_iota(I32, (blk, blk), 1)
    upper = jnp.where(row <= col, 1.0, 0.0).astype(BF16)
    carry = jnp.zeros((r, 1), F32)
    for c in range(t // blk):
        x = x_ref[:, c * blk:(c + 1) * blk]
        h1 = x.astype(BF16)
        r1 = x - h1.astype(F32)
        h2 = r1.astype(BF16)
        h3 = (r1 - h2.astype(F32)).astype(BF16)
        s = (_dot(h3, upper) + _dot(h2, upper)) + _dot(h1, upper) + carry
        o_ref[:, c * blk:(c + 1) * blk] = s
        carry = s[:, blk - 1:blk]


def _cumsum_lanes(x, blk=256, tr=64):
    r, t = x.shape
    tr = min(tr, r)
    return pl.pallas_call(
        functools.partial(_cumsum_kernel, blk=blk),
        out_shape=jax.ShapeDtypeStruct((r, t), F32),
        grid=(r // tr,),
        in_specs=[pl.BlockSpec((tr, t), lambda i: (i, 0))],
        out_specs=pl.BlockSpec((tr, t), lambda i: (i, 0)),
        compiler_params=_params(("arbitrary",)),
        name="cumsum",
    )(x)


def _head_lanes(nh):
    lane = lax.broadcasted_iota(I32, (1, nh * FOX_HEAD_DIM), 1)
    return [(lane >= h * FOX_HEAD_DIM) & (lane < (h + 1) * FOX_HEAD_DIM) for h in range(nh)]


def _per_head(sels, vals):
    reps = len(sels) * FOX_HEAD_DIM // LANES
    out = jnp.tile(vals[0], (1, reps)) if reps > 1 else vals[0]
    for sel, v in zip(sels[1:], vals[1:]):
        out = jnp.where(sel, jnp.tile(v, (1, reps)) if reps > 1 else v, out)
    return out


def _attn_step(q, k, v, ck, m_scr, l_scr, acc_scr, mask):
    nh = m_scr.shape[0]
    sels = _head_lanes(nh)
    pv, alphas = None, []
    for h in range(nh):
        qh = jnp.where(sels[h], q, jnp.zeros_like(q))
        s = _dot_nt(qh, k) - ck[h:h + 1, :]
        if mask is not None:
            s = jnp.where(mask, s, -jnp.inf)
        m_prev = m_scr[h]
        m_new = jnp.maximum(m_prev, jnp.max(s, axis=-1, keepdims=True))
        alpha = jnp.exp(m_prev - m_new)
        p = jnp.exp(s - m_new[:, :1])
        l_scr[h] = alpha * l_scr[h] + jnp.sum(p, axis=-1, keepdims=True)
        m_scr[h] = m_new
        pv_h = _dot(p.astype(BF16), v)
        pv = pv_h if pv is None else jnp.where(sels[h], pv_h, pv)
        alphas.append(alpha)
    acc_scr[...] = _per_head(sels, alphas) * acc_scr[...] + pv


def _attn_init(m_scr, l_scr, acc_scr):
    m_scr[...] = jnp.full(m_scr.shape, -jnp.inf, F32)
    l_scr[...] = jnp.zeros(l_scr.shape, F32)
    acc_scr[...] = jnp.zeros(acc_scr.shape, F32)


def _attn_finish(o_ref, l_scr, acc_scr):
    nh = l_scr.shape[0]
    l = _per_head(_head_lanes(nh), [l_scr[h] for h in range(nh)])
    o_ref[...] = (acc_scr[...] / l).astype(o_ref.dtype)


def _causal_mask(tq, tk):
    return (lax.broadcasted_iota(I32, (tq, tk), 1) <= lax.broadcasted_iota(I32, (tq, tk), 0))


def _attn_prompt_kernel(qt_ref, kt_ref, q_ref, k_ref, v_ref, ck_ref, o_ref, m_scr, l_scr, acc_scr,
                        *, tq):
    step = pl.program_id(2)
    qi, ki = qt_ref[step], kt_ref[step]

    @pl.when(ki == 0)
    def _():
        _attn_init(m_scr, l_scr, acc_scr)

    @pl.when(ki < qi)
    def _():
        _attn_step(q_ref[...], k_ref[...], v_ref[...], ck_ref[0, 0], m_scr, l_scr, acc_scr, None)

    @pl.when(ki == qi)
    def _():
        _attn_step(q_ref[...], k_ref[...], v_ref[...], ck_ref[0, 0], m_scr, l_scr, acc_scr,
                   _causal_mask(tq, tq))
        _attn_finish(o_ref, l_scr, acc_scr)


def _attn_prompt(qkv16, ck, *, n_batch, seq_len, n_h, nh, tq=512):
    nq = seq_len // tq
    lw = nh * FOX_HEAD_DIM
    ng = n_h // nh
    pairs = [(qi, ki) for qi in range(nq) for ki in range(qi + 1)]
    qt = jnp.asarray([p[0] for p in pairs], I32)
    kt = jnp.asarray([p[1] for p in pairs], I32)
    q_map = lambda b, g, s, qt, kt: (b * nq + qt[s], g)
    kv_map = lambda c: (lambda b, g, s, qt, kt: (b * nq + kt[s], c * ng + g))
    return pl.pallas_call(
        functools.partial(_attn_prompt_kernel, tq=tq),
        out_shape=jax.ShapeDtypeStruct((n_batch * seq_len, ng * lw), BF16),
        grid_spec=pltpu.PrefetchScalarGridSpec(
            num_scalar_prefetch=2,
            grid=(n_batch, ng, len(pairs)),
            in_specs=[pl.BlockSpec((tq, lw), q_map),
                      pl.BlockSpec((tq, lw), kv_map(1)),
                      pl.BlockSpec((tq, lw), kv_map(2)),
                      pl.BlockSpec((1, 1, nh, tq), lambda b, g, s, qt, kt: (b, g, 0, kt[s]))],
            out_specs=pl.BlockSpec((tq, lw), q_map),
            scratch_shapes=[pltpu.VMEM((nh, tq, LANES), F32), pltpu.VMEM((nh, tq, LANES), F32),
                            pltpu.VMEM((tq, lw), F32)]),
        compiler_params=_params(("arbitrary",) * 3),
        name="attn_prompt",
    )(qt, kt, qkv16, qkv16, qkv16, ck)


def _attn_sample_kernel(q_ref, kp_ref, vp_ref, kn_ref, vn_ref, ckp_ref, ckn_ref, o_ref, *, ts, nh):
    hd = FOX_HEAD_DIM
    lw = nh * hd
    lane = lax.broadcasted_iota(I32, (1, lw), 1)
    sels = [(lane >= h * hd) & (lane < (h + 1) * hd) for h in range(nh)]
    q = q_ref[...]
    qbd = jnp.concatenate([jnp.where(sels[h], q, jnp.zeros_like(q)) for h in range(nh)], axis=0)

    def update(state, k, v, ck, mask):
        m_prev, l_prev, acc = state
        s = _dot_nt(qbd, k)
        rows = []
        for h in range(nh):
            sh = s[h * ts:(h + 1) * ts, :] - ck[h:h + 1, :]
            rows.append(sh if mask is None else jnp.where(mask, sh, -jnp.inf))
        s = jnp.concatenate(rows, axis=0)
        m_new = jnp.maximum(m_prev, jnp.max(s, axis=-1, keepdims=True))
        alpha = jnp.exp(m_prev - m_new)
        p = jnp.exp(s - m_new)
        l_new = alpha * l_prev + jnp.sum(p, axis=-1, keepdims=True)
        return m_new, l_new, alpha * acc + _dot(p.astype(BF16), v)

    state = (jnp.full((nh * ts, 1), -jnp.inf, F32), jnp.zeros((nh * ts, 1), F32),
             jnp.zeros((nh * ts, lw), F32))
    state = update(state, kp_ref[0].astype(BF16), vp_ref[0].astype(BF16), ckp_ref[0, 0], None)
    _, l, acc = update(state, kn_ref[...], vn_ref[...], ckn_ref[0, 0], _causal_mask(ts, ts))
    res = acc / l
    out = res[0:ts, :]
    for h in range(1, nh):
        out = jnp.where(sels[h], res[h * ts:(h + 1) * ts, :], out)
    o_ref[...] = out.astype(o_ref.dtype)


def _attn_sample(qkv16, k_past, v_past, ck_past, ck_new, *, row0, n_batch, seq_len, n_h, nh=4):
    past = k_past.shape[1]
    lw = nh * FOX_HEAD_DIM
    ng = n_h // nh
    rb0 = row0 // seq_len
    new_map = lambda c: (lambda b, g: (rb0 + b, c * ng + g))
    return pl.pallas_call(
        functools.partial(_attn_sample_kernel, ts=seq_len, nh=nh),
        out_shape=jax.ShapeDtypeStruct((n_batch * seq_len, ng * lw), BF16),
        grid=(n_batch, ng),
        in_specs=[pl.BlockSpec((seq_len, lw), new_map(0)),
                  pl.BlockSpec((1, past, lw), lambda b, g: (b, 0, g)),
                  pl.BlockSpec((1, past, lw), lambda b, g: (b, 0, g)),
                  pl.BlockSpec((seq_len, lw), new_map(1)),
                  pl.BlockSpec((seq_len, lw), new_map(2)),
                  pl.BlockSpec((1, 1, nh, past), lambda b, g: (b, g, 0, 0)),
                  pl.BlockSpec((1, 1, nh, seq_len), lambda b, g: (b, g, 0, 0))],
        out_specs=pl.BlockSpec((seq_len, lw), lambda b, g: (b, g)),
        compiler_params=_params(("arbitrary",) * 2),
        name="attn_sample",
    )(qkv16, k_past, v_past, qkv16, qkv16, ck_past, ck_new)


def _glu_kernel(g_ref, wa_ref, wb_ref, o_ref):
    g = g_ref[...]
    o_ref[...] = (_dot(g, wa_ref[...]) * jax.nn.sigmoid(_dot(g, wb_ref[...]))).astype(o_ref.dtype)


def _glu(g16, w_glu16, tm=512, tn=1024):
    n, ds = g16.shape
    d = w_glu16.shape[1] // 2
    tn = min(tn, d)
    nb = d // tn
    return pl.pallas_call(
        _glu_kernel,
        out_shape=jax.ShapeDtypeStruct((n, d), BF16),
        grid=(n // tm, nb),
        in_specs=[pl.BlockSpec((tm, ds), lambda i, j: (i, 0)),
                  pl.BlockSpec((ds, tn), lambda i, j: (0, j)),
                  pl.BlockSpec((ds, tn), lambda i, j: (0, nb + j))],
        out_specs=pl.BlockSpec((tm, tn), lambda i, j: (i, j)),
        compiler_params=_params(("arbitrary", "arbitrary")),
        name="glu",
    )(g16, w_glu16, w_glu16)


def _post_kernel(attp_ref, atts_ref, brs_ref, gs_ref, gf_ref, xp_ref, xs_ref, ada_ref, wfo_ref,
                 wo_ref, lg_ref, lb_ref, o_ref, *, nsub, alpha, np_tiles):
    i = pl.program_id(0)
    att = jnp.where(i < np_tiles, attp_ref[...], atts_ref[...])
    br_fox = _dot(att, wfo_ref[...])
    merged = (jax.nn.sigmoid(gs_ref[...].astype(F32)) * brs_ref[...].astype(F32)
              + jax.nn.sigmoid(gf_ref[...].astype(F32)) * br_fox)
    mix = _dot(merged.astype(BF16), wo_ref[...])

    def finish(x_ref):
        for s in range(nsub):
            rows = slice(s * ADA_BLOCK, (s + 1) * ADA_BLOCK)
            g1 = ada_ref[s, 2:3, :]
            y = alpha * x_ref[rows, :] + (1.0 + g1) * mix[rows, :]
            o_ref[rows, :] = _layer_norm(y, lg_ref[...], lb_ref[...])

    @pl.when(i < np_tiles)
    def _():
        finish(xp_ref)

    @pl.when(i >= np_tiles)
    def _():
        finish(xs_ref)


def _post(attn_p, attn_s, br_ssm, gates, x_p, x_s, ada_blk, w_fox16, w_out16, ln_g, ln_b, *,
          alpha, tm=256):
    n_p, d = x_p.shape
    n = n_p + x_s.shape[0]
    ds = d // 2
    nsub = tm // ADA_BLOCK
    np_tiles = n_p // tm
    return pl.pallas_call(
        functools.partial(_post_kernel, nsub=nsub, alpha=alpha, np_tiles=np_tiles),
        out_shape=jax.ShapeDtypeStruct((n, d), F32),
        grid=(n // tm,),
        in_specs=_two_part_specs(tm, ds, np_tiles, 1) + [
                  pl.BlockSpec((tm, d), lambda i: (i, 0)),
                  pl.BlockSpec((tm, d), lambda i: (i, 0)),
                  pl.BlockSpec((tm, d), lambda i: (i, 1)),
                  ] + _two_part_specs(tm, d, np_tiles, 1) + [
                  pl.BlockSpec((nsub, 6, d), lambda i: (i, 0, 0)),
                  pl.BlockSpec((ds, d), lambda i: (0, 0)),
                  pl.BlockSpec((d, d), lambda i: (0, 0)),
                  pl.BlockSpec((1, d), lambda i: (0, 0)),
                  pl.BlockSpec((1, d), lambda i: (0, 0))],
        out_specs=pl.BlockSpec((tm, d), lambda i: (i, 0)),
        compiler_params=_params(("arbitrary",)),
        name="post_mix",
    )(attn_p, attn_s, br_ssm, gates, gates, x_p, x_s, ada_blk, w_fox16, w_out16, ln_g, ln_b)


def _pack_halves(x):
    c = x.shape[1] // 2
    return pltpu.pack_elementwise([x[:, :c], x[:, c:]], packed_dtype=BF16)


def _unpack_halves(w):
    return tuple(pltpu.unpack_elementwise(w, index=i, packed_dtype=BF16, unpacked_dtype=F32)
                 for i in range(2))


def _store_token_tiles(ref, words):
    m, c = words.shape
    nsl = c // LANES
    for s in range(nsl):
        ref[pl.ds(s, m, stride=nsl), :] = words[:, s * LANES:(s + 1) * LANES]


def _load_token_tiles(ref, m):
    nsl = ref.shape[0] // m
    return jnp.concatenate([ref[pl.ds(s, m, stride=nsl), :] for s in range(nsl)], axis=1)


def _token_rows(t, nsl):
    start = t * nsl
    if nsl % SUBLANES == 0:
        start = pl.multiple_of(start, SUBLANES)
    return pl.ds(start, nsl)


def _router_kernel(x_ref, ada_ref, wr_ref, rb_ref, hp_ref, e_ref, w_ref, h_scr, *, nsub):
    ng = N_EXPERT_GROUPS
    for s in range(nsub):
        rows = slice(s * ADA_BLOCK, (s + 1) * ADA_BLOCK)
        h_scr[rows, :] = x_ref[rows, :] * (1.0 + ada_ref[s, 4:5, :]) + ada_ref[s, 3:4, :]
    _store_token_tiles(hp_ref, _pack_halves(h_scr[...]))
    h_hi, h_lo = _split_bf16(h_scr[...])
    w_hi, w_lo = _split_bf16(wr_ref[...])
    logits = (_dot_nt(w_hi, h_lo) + _dot_nt(w_lo, h_hi)) + _dot_nt(w_hi, h_hi)
    scores = jax.nn.sigmoid(logits)
    sel = scores + rb_ref[...]
    gsz = sel.shape[0] // ng
    tm = sel.shape[1]
    xs = [sel[j * ng:(j + 1) * ng, :] for j in range(gsz)]
    sc = [scores[j * ng:(j + 1) * ng, :] for j in range(gsz)]
    neg = -jnp.inf

    def lmax(v):
        out = v[0]
        for a in v[1:]:
            out = jnp.maximum(out, a)
        return out

    def lmin(v):
        out = v[0]
        for a in v[1:]:
            out = jnp.minimum(out, a)
        return out

    m1 = lmax(xs)
    i1 = lmin([jnp.where(xs[j] == m1, j, gsz) for j in range(gsz)])
    m2 = lmax([jnp.where(i1 == j, neg, xs[j]) for j in range(gsz)])
    cur = m1 + m2
    giota = lax.broadcasted_iota(I32, (ng, tm), 0)
    gsel = jnp.zeros((ng, tm), F32)
    for _ in range(TOPK_EXPERT_GROUPS):
        m = jnp.max(cur, axis=0, keepdims=True)
        gi = jnp.min(jnp.where(cur == m, giota, ng), axis=0, keepdims=True)
        hit = giota == gi
        gsel = jnp.where(hit, 1.0, gsel)
        cur = jnp.where(hit, neg, cur)
    gmask = gsel > 0.0
    xs = [jnp.where(gmask, x, neg) for x in xs]
    eid = [giota * gsz + j for j in range(gsz)]
    n_e = ng * gsz
    vals = []
    for r in range(TOP_K):
        m = jnp.max(lmax(xs), axis=0, keepdims=True)
        ci = jnp.min(lmin([jnp.where(xs[j] == m, eid[j], n_e) for j in range(gsz)]),
                     axis=0, keepdims=True)
        hits = [eid[j] == ci for j in range(gsz)]
        v = sum(jnp.where(hits[j], sc[j], 0.0) for j in range(gsz))
        vals.append(jnp.sum(v, axis=0, keepdims=True))
        xs = [jnp.where(hits[j], neg, xs[j]) for j in range(gsz)]
        e_ref[r:r + 1, :] = ci
    tot = sum(vals)
    for r in range(TOP_K):
        w_ref[r:r + 1, :] = vals[r] / tot * ROUTED_SCALE


def _router(x1, ada_blk, wr_perm, rb_perm, tm=512):
    n, d = x1.shape
    e = wr_perm.shape[0]
    nsub = tm // ADA_BLOCK
    return pl.pallas_call(
        functools.partial(_router_kernel, nsub=nsub),
        out_shape=(jax.ShapeDtypeStruct((n * (d // 2 // LANES), LANES), jnp.uint32),
                   jax.ShapeDtypeStruct((TOP_K, n), I32),
                   jax.ShapeDtypeStruct((TOP_K, n), F32)),
        grid=(n // tm,),
        in_specs=[pl.BlockSpec((tm, d), lambda i: (i, 0)),
                  pl.BlockSpec((nsub, 6, d), lambda i: (i, 0, 0)),
                  pl.BlockSpec((e, d), lambda i: (0, 0)),
                  pl.BlockSpec((e, 1), lambda i: (0, 0))],
        out_specs=(pl.BlockSpec((tm * (d // 2 // LANES), LANES), lambda i: (i, 0)),
                   pl.BlockSpec((TOP_K, tm), lambda i: (0, i)),
                   pl.BlockSpec((TOP_K, tm), lambda i: (0, i))),
        scratch_shapes=[pltpu.VMEM((tm, d), F32)],
        compiler_params=_params(("arbitrary",)),
        name="router",
    )(x1, ada_blk, wr_perm, rb_perm)


def _plan_kernel(e_ref, pos_ref, st_ref, vis_ref, rank_scr, *, n_e, blk, tm):
    n = e_ref.shape[1]
    nblk = n // blk
    row = lax.broadcasted_iota(I32, (blk, blk), 0)
    col = lax.broadcasted_iota(I32, (blk, blk), 1)
    upper = jnp.where(row <= col, 1.0, 0.0).astype(BF16)
    eid = lax.broadcasted_iota(I32, (n_e, blk), 0)

    def count_body(cb, carry):
        cols = pl.ds(pl.multiple_of(cb * blk, blk), blk)
        e_blk = e_ref[:, cols]
        hit = jnp.zeros((n_e, blk), F32)
        for k in range(TOP_K):
            hit = hit + jnp.where(e_blk[k:k + 1, :] == eid, 1.0, 0.0)
        cs = _dot(hit.astype(BF16), upper) + carry
        rank_scr[:, cols] = cs - hit
        return cs[:, blk - 1:blk]

    counts = lax.fori_loop(0, nblk, count_body, jnp.zeros((n_e, 1), F32))

    hi = jnp.floor(counts * (1.0 / 128.0))
    lo = counts - hi * 128.0
    er = lax.broadcasted_iota(I32, (n_e, n_e), 0)
    ec = lax.broadcasted_iota(I32, (n_e, n_e), 1)
    lower = jnp.where(ec < er, 1.0, 0.0).astype(BF16)
    lower_incl = jnp.where(ec <= er, 1.0, 0.0).astype(BF16)
    wide = lambda v: jnp.broadcast_to(v, (n_e, LANES)).astype(BF16)
    starts = (_dot(lower, wide(hi)) * 128.0 + _dot(lower, wide(lo)))[:, :1]
    st_ref[...] = jnp.broadcast_to(starts, st_ref.shape).astype(I32)

    inv_tm = 1.0 / tm
    nonempty = counts > 0.0
    first_t = jnp.floor(starts * inv_tm)
    nvis = jnp.where(nonempty, jnp.floor((starts + counts - 1.0) * inv_tm) - first_t + 1.0, 0.0)
    vend = _dot(lower_incl, wide(nvis))[:, :1]
    gidx = _dot(lower_incl, wide(jnp.where(nonempty, 1.0, 0.0)))[:, :1] - 1.0
    total = jnp.max(vend, axis=0, keepdims=True)
    nv = vis_ref.shape[1]
    viota = lax.broadcasted_iota(I32, (1, nv), 1).astype(F32)
    vc = jnp.minimum(viota, total - 1.0)
    e_v = jnp.sum(jnp.where(vend <= vc, 1.0, 0.0), axis=0, keepdims=True)
    eio = lax.broadcasted_iota(I32, (n_e, nv), 0).astype(F32)
    mine = eio == e_v
    pick = lambda colv: jnp.sum(jnp.where(mine, colv, 0.0), axis=0, keepdims=True)
    tile_v = pick(first_t) + (vc - pick(vend - nvis))
    g_v = pick(gidx)
    slot_v = g_v - 2.0 * jnp.floor(g_v * 0.5)
    is_next = nonempty & (gidx == g_v + 1.0)
    nxt_v = (jnp.sum(jnp.where(is_next, eio + 1.0, 0.0), axis=0, keepdims=True) - 1.0)
    rows = [tile_v, e_v, jnp.where(viota < total, 1.0, 0.0), slot_v, nxt_v]
    rows += [jnp.zeros((1, nv), F32)] * (vis_ref.shape[0] - len(rows))
    vis_ref[...] = jnp.concatenate(rows, axis=0).astype(I32)

    def pos_body(cb, c):
        cols = pl.ds(pl.multiple_of(cb * blk, blk), blk)
        e_blk = e_ref[:, cols]
        val = rank_scr[:, cols] + starts
        for k in range(TOP_K):
            p = jnp.sum(jnp.where(e_blk[k:k + 1, :] == eid, val, 0.0), axis=0, keepdims=True)
            pos_ref[k:k + 1, cols] = p.astype(I32)
        return c

    lax.fori_loop(0, nblk, pos_body, 0)


def _plan(eidx_t, n_e, tm, blk=256):
    k, n = eidx_t.shape
    n_visits = (n * k) // tm + n_e - 1
    nv = -(-n_visits // LANES) * LANES
    pos, starts, vis = pl.pallas_call(
        functools.partial(_plan_kernel, n_e=n_e, blk=blk, tm=tm),
        out_shape=(jax.ShapeDtypeStruct((k, n), I32), jax.ShapeDtypeStruct((n_e, LANES), I32),
                   jax.ShapeDtypeStruct((SUBLANES, nv), I32)),
        scratch_shapes=[pltpu.VMEM((n_e, n), F32)],
        compiler_params=pltpu.CompilerParams(vmem_limit_bytes=VMEM_LIMIT_BYTES),
        name="moe_plan",
    )(eidx_t)
    return pos, starts, vis[:, :n_visits]


def _dispatch_kernel(pos_ref, h_ref, xs_ref, sem, *, tm):
    i = pl.program_id(0)
    nsl = h_ref.shape[0] // tm

    def issue(r, c):
        base = (i * tm + r) * TOP_K
        src = h_ref.at[_token_rows(r, nsl), :]
        for k in range(TOP_K):
            p = pos_ref[base + k]
            pltpu.make_async_copy(src, xs_ref.at[_token_rows(p, nsl), :],
                                  sem).start(priority=k % 2)
        return c

    lax.fori_loop(0, tm, issue, 0)
    for k in range(TOP_K):
        pltpu.make_async_copy(h_ref, xs_ref.at[pl.ds(0, tm * nsl), :], sem).wait()


def _dispatch(pos_flat, h2t, n, tm=256):
    rows, lanes = h2t.shape
    nsl = rows // n
    return pl.pallas_call(
        functools.partial(_dispatch_kernel, tm=tm),
        out_shape=jax.ShapeDtypeStruct((rows * TOP_K, lanes), h2t.dtype),
        grid_spec=pltpu.PrefetchScalarGridSpec(
            num_scalar_prefetch=1,
            grid=(n // tm,),
            in_specs=[pl.BlockSpec((tm * nsl, lanes), lambda i, pos: (i, 0))],
            out_specs=pl.BlockSpec(memory_space=pl.ANY),
            scratch_shapes=[pltpu.SemaphoreType.DMA(())]),
        compiler_params=_params(("arbitrary",)),
        name="moe_dispatch",
    )(pos_flat, h2t)


def _gmm_kernel(vt_ref, ve_ref, vv_ref, vs_ref, nx_ref, bd_ref,
                xs_ref, wg_hbm, wu_hbm, wd_hbm, ys_ref,
                wg32, wu32, wd32, wg16, wu16, wd16, acc, sem, *, tm, chain):
    v = pl.program_id(0)
    t = vt_ref[v]
    e = ve_ref[v]
    slot = vs_ref[v]
    pv = jnp.maximum(v - 1, 0)
    first = v == 0
    valid = vv_ref[v] == 1
    r0 = t * tm
    lo = bd_ref[e]
    hi = bd_ref[e + 1]
    whole = (r0 >= lo) & (r0 + tm <= hi)

    def weight_copies(expert, s):
        return (pltpu.make_async_copy(wg_hbm.at[expert], wg32.at[s], sem.at[s, 0]),
                pltpu.make_async_copy(wu_hbm.at[expert], wu32.at[s], sem.at[s, 1]),
                pltpu.make_async_copy(wd_hbm.at[expert], wd32.at[s], sem.at[s, 2]))

    @pl.when(first)
    def _():
        for c in weight_copies(e, slot):
            c.start()

    @pl.when(first | (e != ve_ref[pv]))
    def _():
        for c in weight_copies(e, slot):
            c.wait()
        nxt = nx_ref[v]

        @pl.when(nxt >= 0)
        def _():
            for c in weight_copies(nxt, 1 - slot):
                c.start()

        wg16[...] = wg32[slot].astype(BF16)
        wu16[...] = wu32[slot].astype(BF16)
        wd16[...] = wd32[slot].astype(BF16)

    @pl.when(valid & jnp.logical_not(whole) & (first | (t != vt_ref[pv])))
    def _():
        acc[...] = jnp.zeros(acc.shape, F32)

    @pl.when(valid)
    def _():
        xw = _load_token_tiles(xs_ref, tm)
        parts = []
        for c0 in range(0, tm, chain):
            x = jnp.concatenate(_unpack_halves(xw[c0:c0 + chain]), axis=1).astype(BF16)
            g = _dot(x, wg16[...])
            u = _dot(x, wu16[...])
            act = (g * jax.nn.sigmoid(g) * u).astype(BF16)
            parts.append(_dot(act, wd16[...]))
        y = parts[0] if len(parts) == 1 else jnp.concatenate(parts, axis=0)

        @pl.when(whole)
        def _():
            _store_token_tiles(ys_ref, _pack_halves(y))

        @pl.when(jnp.logical_not(whole))
        def _():
            row = r0 + lax.broadcasted_iota(I32, (tm, 1), 0)
            acc[...] += jnp.where((row >= lo) & (row < hi), y, 0.0)
            _store_token_tiles(ys_ref, _pack_halves(acc[...]))


def _gmm(tile, expert, valid, slot, nxt, bounds, xs, w_gate, w_up, w_down, tm=256):
    n_e, d, f = w_gate.shape
    nsl = d // 2 // LANES
    n_visits = tile.shape[0]
    row_map = lambda v, vt, ve, vv, vs, nx, bd: (vt[v], 0)
    return pl.pallas_call(
        functools.partial(_gmm_kernel, tm=tm, chain=min(tm, 256)),
        out_shape=jax.ShapeDtypeStruct(xs.shape, xs.dtype),
        grid_spec=pltpu.PrefetchScalarGridSpec(
            num_scalar_prefetch=6,
            grid=(n_visits,),
            in_specs=[pl.BlockSpec((tm * nsl, LANES), row_map),
                      pl.BlockSpec(memory_space=pl.ANY),
                      pl.BlockSpec(memory_space=pl.ANY),
                      pl.BlockSpec(memory_space=pl.ANY)],
            out_specs=pl.BlockSpec((tm * nsl, LANES), row_map),
            scratch_shapes=[pltpu.VMEM((2, d, f), F32), pltpu.VMEM((2, d, f), F32),
                            pltpu.VMEM((2, f, d), F32),
                            pltpu.VMEM((d, f), BF16), pltpu.VMEM((d, f), BF16),
                            pltpu.VMEM((f, d), BF16),
                            pltpu.VMEM((tm, d), F32),
                            pltpu.SemaphoreType.DMA((2, 3))]),
        compiler_params=_params(("arbitrary",)),
        name="moe_experts",
    )(tile, expert, valid, slot, nxt, bounds, xs, w_gate, w_up, w_down)


def _combine_kernel(pos_ref, ys_ref, w_ref, hp_ref, x_ref, ada_ref, wsg_ref, wsu_ref, wsd_ref,
                    lg_ref, lb_ref, op_ref, os_ref, buf, ffn_scr, sem, *, tm, nsub, alpha, np_tiles):
    i = pl.program_id(0)
    nsl = hp_ref.shape[0] // tm
    slot = i % 2

    def gather_tile(tile, s):
        def issue(r, c):
            base = (tile * tm + r) * TOP_K
            dst_rows = _token_rows(r, nsl)
            for k in range(TOP_K):
                p = pos_ref[base + k]
                pltpu.make_async_copy(ys_ref.at[_token_rows(p, nsl), :], buf.at[s, k, dst_rows, :],
                                      sem.at[s]).start(priority=k % 2)
            return c

        lax.fori_loop(0, tm, issue, 0)

    @pl.when(i == 0)
    def _():
        gather_tile(0, 0)

    @pl.when(i + 1 < pl.num_programs(0))
    def _():
        gather_tile(i + 1, 1 - slot)

    h16 = jnp.concatenate(_unpack_halves(_load_token_tiles(hp_ref, tm)), axis=1).astype(BF16)
    g = _dot(h16, wsg_ref[...])
    u = _dot(h16, wsu_ref[...])
    shared = _dot((g * jax.nn.sigmoid(g) * u).astype(BF16), wsd_ref[...])

    for k in range(TOP_K):
        pltpu.make_async_copy(ys_ref.at[pl.ds(0, tm * nsl), :], buf.at[slot, k], sem.at[slot]).wait()
    ffn_scr[...] = shared
    half = ffn_scr.shape[1] // 2
    rblk = min(tm, 128)
    for r0 in range(0, tm, rblk):
        wb = [jnp.broadcast_to(w_ref[r0:r0 + rblk, k:k + 1], (rblk, LANES)) for k in range(TOP_K)]
        for s in range(nsl):
            lo = hi = None
            for k in range(TOP_K):
                a, b = _unpack_halves(buf[slot, k, pl.ds(r0 * nsl + s, rblk, stride=nsl), :])
                lo = wb[k] * a if lo is None else lo + wb[k] * a
                hi = wb[k] * b if hi is None else hi + wb[k] * b
            ffn_scr[r0:r0 + rblk, s * LANES:(s + 1) * LANES] += lo
            ffn_scr[r0:r0 + rblk, half + s * LANES:half + (s + 1) * LANES] += hi

    def finish(o_ref):
        for s in range(nsub):
            rows = slice(s * ADA_BLOCK, (s + 1) * ADA_BLOCK)
            g2 = ada_ref[s, 5:6, :]
            y = alpha * x_ref[rows, :] + (1.0 + g2) * ffn_scr[rows, :]
            o_ref[rows, :] = _layer_norm(y, lg_ref[...], lb_ref[...])

    @pl.when(i < np_tiles)
    def _():
        finish(op_ref)

    @pl.when(i >= np_tiles)
    def _():
        finish(os_ref)


def _combine(pos_flat, ys, w_tok, h2p, x1, ada_blk, wsg16, wsu16, wsd16, ln_g, ln_b, *,
             alpha, n_p, tm=256):
    n, d = x1.shape
    nsl = d // 2 // LANES
    f = wsg16.shape[1]
    nsub = tm // ADA_BLOCK
    np_tiles = n_p // tm
    return pl.pallas_call(
        functools.partial(_combine_kernel, tm=tm, nsub=nsub, alpha=alpha, np_tiles=np_tiles),
        out_shape=(jax.ShapeDtypeStruct((n_p, d), F32), jax.ShapeDtypeStruct((n - n_p, d), F32)),
        grid_spec=pltpu.PrefetchScalarGridSpec(
            num_scalar_prefetch=1,
            grid=(n // tm,),
            in_specs=[pl.BlockSpec(memory_space=pl.ANY),
                      pl.BlockSpec((tm, TOP_K), lambda i, pos: (i, 0)),
                      pl.BlockSpec((tm * nsl, LANES), lambda i, pos: (i, 0)),
                      pl.BlockSpec((tm, d), lambda i, pos: (i, 0)),
                      pl.BlockSpec((nsub, 6, d), lambda i, pos: (i, 0, 0)),
                      pl.BlockSpec((d, f), lambda i, pos: (0, 0)),
                      pl.BlockSpec((d, f), lambda i, pos: (0, 0)),
                      pl.BlockSpec((f, d), lambda i, pos: (0, 0)),
                      pl.BlockSpec((1, d), lambda i, pos: (0, 0)),
                      pl.BlockSpec((1, d), lambda i, pos: (0, 0))],
            out_specs=(pl.BlockSpec((tm, d), lambda i, pos: (jnp.minimum(i, np_tiles - 1), 0)),
                       pl.BlockSpec((tm, d), lambda i, pos: (jnp.maximum(i - np_tiles, 0), 0))),
            scratch_shapes=[pltpu.VMEM((2, TOP_K, tm * nsl, LANES), ys.dtype),
                            pltpu.VMEM((tm, d), F32),
                            pltpu.SemaphoreType.DMA((2,))]),
        compiler_params=_params(("arbitrary",)),
        name="moe_combine",
    )(pos_flat, ys, w_tok, h2p, x1, ada_blk, wsg16, wsu16, wsd16, ln_g, ln_b)


def _layer(x_p, x_s, c_pad, blk_batch, cache_k, cache_v, layer, past_logf, st_re, st_im, lp, dims):
    bp, tp, bs, ts, d = dims
    n_p, n_s = bp * tp, bs * ts
    n = n_p + n_s
    ds = d // 2
    n_h = ds // FOX_HEAD_DIM
    g_n, p_n = lp["ssm_lambda_re"].shape
    n_e = lp["w_router"].shape[1]
    depth_alpha = lp["alpha"]

    ada = _ada(c_pad, lp["w_ada"], lp["b_ada"])
    ada_blk = ada.reshape(ada.shape[0], 6, d)[blk_batch]

    w_in = lp["w_in"]
    w_main = jnp.concatenate([w_in[:, :4 * ds], w_in[:, 4 * ds + n_h:]], axis=1).astype(BF16)
    wf_pad = jnp.pad(w_in[:, 4 * ds:4 * ds + n_h], ((0, 0), (0, LANES - n_h))).astype(BF16)
    zu, kv_p, kv_s, gates, qkv16, logf_t = _inproj(x_p, x_s, ada_blk, w_main, wf_pad,
                                                   lp["b_f"].reshape(n_h, 1))

    lbr, lbi, bbr, bbi = _ssm_disc(lp["ssm_lambda_re"], lp["ssm_lambda_im"], lp["ssm_log_dt"],
                                   lp["ssm_b_re"], lp["ssm_b_im"])
    lam_lay = _to_state_layout(lbr[None], lbi[None])
    bd = jnp.concatenate([_block_diag_slabs(bbr.transpose(0, 2, 1)),
                          _block_diag_slabs(bbi.transpose(0, 2, 1))], axis=2).astype(BF16)
    cd = jnp.stack([_block_diag_slabs(lp["ssm_c_re"].transpose(0, 2, 1)),
                    _block_diag_slabs(lp["ssm_c_im"].transpose(0, 2, 1))], axis=1).astype(BF16)
    d_row = lp["ssm_d"].reshape(1, ds)
    x0_p = jnp.zeros((bp, 2 * g_n * p_n), F32)
    x0_s = _to_state_layout(st_re, st_im)
    g_p, xl_p = _ssm(zu, x0_p, lam_lay, bd, cd, d_row, row0=0, n_seq=bp, seq_len=tp,
                     s_blk=bp, tt=min(128, tp))
    g_s, xl_s = _ssm(zu, x0_s, lam_lay, bd, cd, d_row, row0=n_p, n_seq=bs, seq_len=ts,
                     s_blk=min(8, bs), tt=ts)
    g16 = jnp.concatenate(g_p + g_s, axis=0)
    br_ssm = _glu(g16, lp["w_glu"].astype(BF16))

    lf_p = logf_t[:, :n_p].reshape(n_h, bp, tp).transpose(1, 0, 2)
    lf_s = logf_t[:, n_p:].reshape(n_h, bs, ts).transpose(1, 0, 2)
    nh_p = 4 if n_h % 4 == 0 else 2
    ck_p = _cumsum_lanes(lf_p.reshape(bp * n_h, tp)).reshape(bp, n_h // nh_p, nh_p, tp)
    past = past_logf.shape[1]
    cat = jnp.concatenate([past_logf.astype(F32).transpose(0, 2, 1), lf_s], axis=2)
    width = -(-(past + ts) // 256) * 256
    cat = jnp.pad(cat, ((0, 0), (0, 0), (0, width - past - ts)))
    nh_s = 4
    ck_s = _cumsum_lanes(cat.reshape(bs * n_h, width)).reshape(bs, n_h // nh_s, nh_s, width)
    attn_p = _attn_prompt(qkv16, ck_p, n_batch=bp, seq_len=tp, n_h=n_h, nh=nh_p, tq=min(512, tp))
    attn_s = _attn_sample(qkv16, cache_k[layer].reshape(bs, past, ds),
                          cache_v[layer].reshape(bs, past, ds),
                          ck_s[..., :past], ck_s[..., past:past + ts],
                          row0=n_p, n_batch=bs, seq_len=ts, n_h=n_h, nh=nh_s)

    x1 = _post(attn_p, attn_s, br_ssm, gates, x_p, x_s, ada_blk, lp["w_fox_o"].astype(BF16),
               lp["w_out"].astype(BF16), lp["ln1_g"].reshape(1, d), lp["ln1_b"].reshape(1, d),
               alpha=depth_alpha)

    gsz = n_e // N_EXPERT_GROUPS
    perm = lambda a: a.reshape(N_EXPERT_GROUPS, gsz, -1).transpose(1, 0, 2).reshape(n_e, -1)
    h2p, eidx_t, w_t = _router(x1, ada_blk, perm(lp["w_router"].T), perm(lp["router_bias"].reshape(n_e, 1)))
    tm_e = 256
    pos_t, starts, vis = _plan(eidx_t, n_e, tm_e)
    pos_flat = pos_t.T.reshape(n * TOP_K)
    bounds = jnp.concatenate([starts[:, 0], jnp.full((1,), n * TOP_K, I32)])
    xs = _dispatch(pos_flat, h2p, n)
    ys = _gmm(vis[0], vis[1], vis[2], vis[3], vis[4], bounds, xs, lp["w_exp_gate"], lp["w_exp_up"],
              lp["w_exp_down"], tm=tm_e)
    x2_p, x2_s = _combine(pos_flat, ys, w_t.T, h2p, x1, ada_blk, lp["w_sh_gate"].astype(BF16),
                          lp["w_sh_up"].astype(BF16), lp["w_sh_down"].astype(BF16),
                          lp["ln2_g"].reshape(1, d), lp["ln2_b"].reshape(1, d),
                          alpha=depth_alpha, n_p=n_p)

    k_new, v_new = (kv_p[0], kv_s[0]), (kv_p[1], kv_s[1])
    logf = logf_t.T
    ssm_p = _from_state_layout(xl_p, g_n, p_n)
    ssm_s = _from_state_layout(xl_s, g_n, p_n)
    return x2_p, x2_s, k_new, v_new, logf, ssm_p, ssm_s


def kernel(x_prompt, x_sample, cache_k, cache_v, cache_logf, state_ssm_re, state_ssm_im, c_prompt, c_sample, w_ada, b_ada, w_in, b_f, ssm_lambda_re, ssm_lambda_im, ssm_log_dt, ssm_b_re, ssm_b_im, ssm_c_re, ssm_c_im, ssm_d, w_glu, w_fox_o, w_out, ln1_g, ln1_b, w_router, router_bias, w_exp_gate, w_exp_up, w_exp_down, w_sh_gate, w_sh_up, w_sh_down, ln2_g, ln2_b):
    bp, tp, d = x_prompt.shape
    bs, ts, _ = x_sample.shape
    depth = w_ada.shape[0]
    n_p, n_s = bp * tp, bs * ts
    n_h = d // 2 // FOX_HEAD_DIM
    assert tp % ADA_BLOCK == 0 and ts == ADA_BLOCK
    alpha = (2.0 * depth) ** 0.25

    x_p, x_s = x_prompt.reshape(n_p, d), x_sample.reshape(n_s, d)
    c_all = jnp.concatenate([c_prompt, c_sample], axis=0)
    c_pad = jnp.pad(c_all, ((0, -(bp + bs) % 16), (0, 0)))
    blk_batch = np.concatenate([np.repeat(np.arange(bp), tp // ADA_BLOCK),
                                bp + np.repeat(np.arange(bs), ts // ADA_BLOCK)])
    dims = (bp, tp, bs, ts, d)
    outs_p, outs_s = [], []
    for l in range(depth):
        lp = dict(w_ada=w_ada[l], b_ada=b_ada[l], w_in=w_in[l], b_f=b_f[l],
                  ssm_lambda_re=ssm_lambda_re[l], ssm_lambda_im=ssm_lambda_im[l],
                  ssm_log_dt=ssm_log_dt[l], ssm_b_re=ssm_b_re[l], ssm_b_im=ssm_b_im[l],
                  ssm_c_re=ssm_c_re[l], ssm_c_im=ssm_c_im[l], ssm_d=ssm_d[l], w_glu=w_glu[l],
                  w_fox_o=w_fox_o[l], w_out=w_out[l], ln1_g=ln1_g[l], ln1_b=ln1_b[l],
                  w_router=w_router[l], router_bias=router_bias[l], w_exp_gate=w_exp_gate[l],
                  w_exp_up=w_exp_up[l], w_exp_down=w_exp_down[l], w_sh_gate=w_sh_gate[l],
                  w_sh_up=w_sh_up[l], w_sh_down=w_sh_down[l], ln2_g=ln2_g[l], ln2_b=ln2_b[l],
                  alpha=alpha)
        x_p, x_s, k_new, v_new, logf, ssm_p, ssm_s = _layer(
            x_p, x_s, c_pad, blk_batch, cache_k, cache_v, l, cache_logf[l],
            state_ssm_re[l].astype(F32), state_ssm_im[l].astype(F32), lp, dims)
        hd = FOX_HEAD_DIM
        outs_p.append((k_new[0].reshape(bp, tp, n_h, hd), v_new[0].reshape(bp, tp, n_h, hd),
                       logf[:n_p].reshape(bp, tp, n_h), ssm_p[0], ssm_p[1]))
        outs_s.append((k_new[1].reshape(bs, ts, n_h, hd), v_new[1].reshape(bs, ts, n_h, hd),
                       logf[n_p:].reshape(bs, ts, n_h), ssm_s[0], ssm_s[1]))
    stack = lambda outs, i: jnp.stack([o[i] for o in outs])
    return (x_p.reshape(bp, tp, d), x_s.reshape(bs, ts, d),
            stack(outs_p, 0), stack(outs_p, 1), stack(outs_p, 2), stack(outs_p, 3), stack(outs_p, 4),
            stack(outs_s, 0), stack(outs_s, 1), stack(outs_s, 2), stack(outs_s, 3), stack(outs_s, 4))
```

```python
import functools

import jax
import jax.numpy as jnp
import numpy as np
from jax import lax
from jax.experimental import pallas as pl
from jax.experimental.pallas import tpu as pltpu

F32 = jnp.float32
BF16 = jnp.bfloat16
I32 = jnp.int32

V7X_VMEM_BYTES = 64 * 1024 * 1024
VMEM_LIMIT_BYTES = V7X_VMEM_BYTES - 8 * 1024 * 1024
LANES = 128
SUBLANES = 8

SSM_GROUP_WIDTH = 16
SSM_STATE = 64
SSM_GROUPS_PER_SLAB = 8
FOX_HEAD_DIM = 64
N_EXPERT_GROUPS = 8
TOPK_EXPERT_GROUPS = 4
TOP_K = 8
ROUTED_SCALE = 2.5
LN_EPS = 1e-5
ADA_BLOCK = 64

NT_DIMS = (((1,), (1,)), ((), ()))


def _params(sem, vmem=VMEM_LIMIT_BYTES):
    return pltpu.CompilerParams(dimension_semantics=sem, vmem_limit_bytes=vmem)


def _dot(a, b):
    return jnp.dot(a, b, preferred_element_type=F32)


def _dot_nt(a, b):
    return lax.dot_general(a, b, NT_DIMS, preferred_element_type=F32)


def _split_bf16(x):
    hi = x.astype(BF16)
    lo = (x - hi.astype(F32)).astype(BF16)
    return hi, lo


def _log_sigmoid(x):
    return jnp.minimum(x, 0.0) - jnp.log1p(jnp.exp(-jnp.abs(x)))


def _gelu_tanh(x):
    c = np.float32(np.sqrt(2.0 / np.pi))
    return x * (0.5 * (1.0 + jnp.tanh(c * (x + 0.044715 * (x * x * x)))))


def _layer_norm(y, g, b):
    mu = jnp.mean(y, axis=-1, keepdims=True)
    yc = y - mu
    var = jnp.mean(yc * yc, axis=-1, keepdims=True)
    return yc * lax.rsqrt(var + LN_EPS) * g + b


def _ada_kernel(c_ref, w_ref, b_ref, o_ref):
    c = c_ref[...]
    a_hi, a_lo = _split_bf16(c * jax.nn.sigmoid(c))
    w_hi, w_lo = _split_bf16(w_ref[...])
    acc = _dot(a_hi, w_lo) + _dot(a_lo, w_hi)
    o_ref[...] = acc + _dot(a_hi, w_hi) + b_ref[...]


def _ada(c_pad, w_ada, b_ada):
    m, d = c_pad.shape
    n = w_ada.shape[1]
    tn = 1024
    return pl.pallas_call(
        _ada_kernel,
        out_shape=jax.ShapeDtypeStruct((m, n), F32),
        grid=(n // tn,),
        in_specs=[pl.BlockSpec((m, d), lambda j: (0, 0)),
                  pl.BlockSpec((d, tn), lambda j: (0, j)),
                  pl.BlockSpec((1, tn), lambda j: (0, j))],
        out_specs=pl.BlockSpec((m, tn), lambda j: (0, j)),
        compiler_params=_params(("arbitrary",)),
        name="ada",
    )(c_pad, w_ada, b_ada.reshape(1, n))


def _inproj_kernel(xp_ref, xs_ref, ada_ref, w_ref, wf_ref, bf_ref,
                   u_ref, kvp_ref, kvs_ref, g_ref, qkv_ref, lf_ref, h_scr,
                   *, nsub, q_scale, np_tiles):
    i = pl.program_id(0)
    j = pl.program_id(1)

    def modulate(x_ref):
        for s in range(nsub):
            rows = slice(s * ADA_BLOCK, (s + 1) * ADA_BLOCK)
            sh = ada_ref[s, 0:1, :]
            sc = ada_ref[s, 1:2, :]
            h_scr[rows, :] = (x_ref[rows, :] * (1.0 + sc) + sh).astype(BF16)

    @pl.when((j == 0) & (i < np_tiles))
    def _():
        modulate(xp_ref)

    @pl.when((j == 0) & (i >= np_tiles))
    def _():
        modulate(xs_ref)

    @pl.when(j == 0)
    def _():
        f_t = _dot(h_scr[...], wf_ref[...]).T
        lf_ref[...] = _log_sigmoid(f_t[:lf_ref.shape[0], :] + bf_ref[...])

    zt = _dot(h_scr[...], w_ref[...])

    @pl.when(j == 0)
    def _():
        u_ref[...] = zt

    @pl.when(j == 1)
    def _():
        qkv_ref[...] = (zt * q_scale).astype(BF16)

    @pl.when((j == 2) | (j == 3))
    def _():
        qkv_ref[...] = zt.astype(BF16)

    @pl.when(((j == 2) | (j == 3)) & (i < np_tiles))
    def _():
        kvp_ref[0] = zt

    @pl.when(((j == 2) | (j == 3)) & (i >= np_tiles))
    def _():
        kvs_ref[0] = zt

    @pl.when(j >= 4)
    def _():
        g_ref[...] = zt.astype(g_ref.dtype)


def _two_part_specs(tm, d, np_tiles, n_grid_args):
    if n_grid_args == 1:
        return [pl.BlockSpec((tm, d), lambda i: (jnp.minimum(i, np_tiles - 1), 0)),
                pl.BlockSpec((tm, d), lambda i: (jnp.maximum(i - np_tiles, 0), 0))]
    return [pl.BlockSpec((tm, d), lambda i, j: (jnp.minimum(i, np_tiles - 1), 0)),
            pl.BlockSpec((tm, d), lambda i, j: (jnp.maximum(i - np_tiles, 0), 0))]


def _inproj(x_p, x_s, ada_blk, w_main, wf_pad, bf_col, tm=512):
    n_p, d = x_p.shape
    n = n_p + x_s.shape[0]
    ds = d // 2
    h = bf_col.shape[0]
    nsub = tm // ADA_BLOCK
    ncol = w_main.shape[1] // ds
    np_tiles = n_p // tm
    kern = functools.partial(_inproj_kernel, nsub=nsub, q_scale=FOX_HEAD_DIM ** -0.5,
                             np_tiles=np_tiles)
    return pl.pallas_call(
        kern,
        out_shape=(jax.ShapeDtypeStruct((n, ds), F32),
                   jax.ShapeDtypeStruct((2, n_p, ds), F32),
                   jax.ShapeDtypeStruct((2, n - n_p, ds), F32),
                   jax.ShapeDtypeStruct((n, 2 * d), BF16),
                   jax.ShapeDtypeStruct((n, 3 * ds), BF16),
                   jax.ShapeDtypeStruct((h, n), F32)),
        grid=(n // tm, ncol),
        in_specs=_two_part_specs(tm, d, np_tiles, 2) + [
                  pl.BlockSpec((nsub, 6, d), lambda i, j: (i, 0, 0)),
                  pl.BlockSpec((d, ds), lambda i, j: (0, j)),
                  pl.BlockSpec((d, LANES), lambda i, j: (0, 0)),
                  pl.BlockSpec((h, 1), lambda i, j: (0, 0))],
        out_specs=(pl.BlockSpec((tm, ds), lambda i, j: (i, 0)),
                   pl.BlockSpec((1, tm, ds), lambda i, j: (
                       jnp.where(i < np_tiles, jnp.clip(j - 2, 0, 1), 1),
                       jnp.minimum(i, np_tiles - 1), 0)),
                   pl.BlockSpec((1, tm, ds), lambda i, j: (
                       jnp.where(i >= np_tiles, jnp.clip(j - 2, 0, 1), 0),
                       jnp.maximum(i - np_tiles, 0), 0)),
                   pl.BlockSpec((tm, ds), lambda i, j: (i, jnp.clip(j - 4, 0, ncol - 5))),
                   pl.BlockSpec((tm, ds), lambda i, j: (i, jnp.clip(j - 1, 0, 2))),
                   pl.BlockSpec((h, tm), lambda i, j: (0, i))),
        scratch_shapes=[pltpu.VMEM((tm, d), BF16)],
        compiler_params=_params(("arbitrary", "arbitrary")),
        name="inproj",
    )(x_p, x_s, ada_blk, w_main, wf_pad, bf_col)


def _ssm_disc_kernel(lr_ref, li_ref, ldt_ref, br_ref, bi_ref,
                     lbr_ref, lbi_ref, bbr_ref, bbi_ref):
    lr = jnp.minimum(lr_ref[...], -1e-4)
    li = li_ref[...]
    dt = jnp.exp(ldt_ref[...])
    er = jnp.exp(lr * dt)
    lbr = er * jnp.cos(li * dt)
    lbi = er * jnp.sin(li * dt)
    lbr_ref[...] = lbr
    lbi_ref[...] = lbi
    nr = lbr - 1.0
    den = lr * lr + li * li
    qr = (nr * lr + lbi * li) / den
    qi = (lbi * lr - nr * li) / den
    b_r = br_ref[...]
    b_i = bi_ref[...]
    bbr_ref[...] = qr * b_r - qi * b_i
    bbi_ref[...] = qr * b_i + qi * b_r


def _ssm_disc(lam_re, lam_im, log_dt, b_re, b_im):
    g, p, w = b_re.shape
    rep = lambda a: jnp.repeat(a, w, axis=1)
    shp = jax.ShapeDtypeStruct((g, p * w), F32)
    lbr, lbi, bbr, bbi = pl.pallas_call(
        _ssm_disc_kernel, out_shape=(shp, shp, shp, shp), name="ssm_disc",
    )(rep(lam_re), rep(lam_im), log_dt.reshape(g, 1),
      b_re.reshape(g, p * w), b_im.reshape(g, p * w))
    return (lbr[:, ::w], lbi[:, ::w], bbr.reshape(g, p, w), bbi.reshape(g, p, w))


def _to_state_layout(re, im):
    s, g, p = re.shape
    ns = g // SSM_GROUPS_PER_SLAB
    r = re.reshape(s, ns, 1, SSM_GROUPS_PER_SLAB * p)
    i = im.reshape(s, ns, 1, SSM_GROUPS_PER_SLAB * p)
    return jnp.concatenate([r, i], axis=2).reshape(s, 2 * g * p)


def _from_state_layout(x, g, p):
    s = x.shape[0]
    y = x.reshape(s, g // SSM_GROUPS_PER_SLAB, 2, SSM_GROUPS_PER_SLAB, p)
    return y[:, :, 0].reshape(s, g, p), y[:, :, 1].reshape(s, g, p)


def _block_diag_slabs(a):
    g, m, n = a.shape
    k = SSM_GROUPS_PER_SLAB
    a4 = a.reshape(g // k, k, m, n)
    eye = jnp.eye(k, dtype=bool)
    out = jnp.where(eye[None, :, None, :, None], a4[:, :, :, None, :], 0.0)
    return out.reshape(g // k, k * m, k * n)


def _ssm_kernel(*refs, n_refs, rpr, s_blk, tt, n_slab, sw):
    u_refs = refs[:n_refs]
    x0_ref, lam_ref, bd_ref, cd_ref, d_ref = refs[n_refs:n_refs + 5]
    o_refs = refs[n_refs + 5:2 * n_refs + 5]
    xl_ref = refs[2 * n_refs + 5]
    u_tm, bu, y_tm, st = refs[2 * n_refs + 6:]
    tb = pl.program_id(1)
    uw = SSM_GROUPS_PER_SLAB * SSM_GROUP_WIDTH
    npl = sw // LANES
    nph = npl // 2
    k_sub = SUBLANES // s_blk
    n_rows = s_blk * tt

    def seq_view(refs_, s):
        return (refs_[s], slice(None)) if n_refs == s_blk else (refs_[0], slice(s * tt, (s + 1) * tt))

    @pl.when(tb == 0)
    def _():
        st[...] = jnp.zeros(st.shape, F32)
        st[0:s_blk, :] = x0_ref[...]

    for s in range(s_blk):
        ref, rows = seq_view(u_refs, s)
        for j in range(n_slab):
            u_tm[j, pl.ds(s, tt, stride=s_blk), :] = ref[rows, j * uw:(j + 1) * uw]
    for j in range(n_slab):
        res = _dot(u_tm[j].astype(BF16), bd_ref[j])
        for q in range(npl):
            bu[j * npl + q] = res[:, q * LANES:(q + 1) * LANES]

    sub = lax.broadcasted_iota(I32, (SUBLANES, LANES), 0)
    for j in range(n_slab):
        c0 = j * sw
        a = [jnp.broadcast_to(lam_ref[:, c0 + q * LANES:c0 + (q + 1) * LANES], (SUBLANES, LANES))
             for q in range(npl)]

        def body(i, carry, j=j, a=a):
            rows = pl.ds(pl.multiple_of(i * SUBLANES, SUBLANES), SUBLANES)
            x = list(carry)
            b = [bu[j * npl + q, rows, :] for q in range(npl)]
            out = [None] * npl
            for step in range(k_sub):
                win = (sub >= step * s_blk) & (sub < (step + 1) * s_blk)
                for q in range(nph):
                    xr, xi = x[q], x[nph + q]
                    nr = a[q] * xr - a[nph + q] * xi + b[q]
                    ni = a[q] * xi + a[nph + q] * xr + b[nph + q]
                    out[q] = nr if step == 0 else jnp.where(win, nr, out[q])
                    out[nph + q] = ni if step == 0 else jnp.where(win, ni, out[nph + q])
                    if k_sub > 1:
                        nr = pltpu.roll(nr, s_blk, 0)
                        ni = pltpu.roll(ni, s_blk, 0)
                    x[q], x[nph + q] = nr, ni
            for q in range(npl):
                bu[j * npl + q, rows, :] = out[q]
            return tuple(x)

        init = tuple(st[:, c0 + q * LANES:c0 + (q + 1) * LANES] for q in range(npl))
        fin = lax.fori_loop(0, n_rows // SUBLANES, body, init, unroll=4)
        for q in range(npl):
            st[:, c0 + q * LANES:c0 + (q + 1) * LANES] = fin[q]

    for j in range(n_slab):
        xr16 = jnp.concatenate([bu[j * npl + q] for q in range(nph)], axis=1).astype(BF16)
        xi16 = jnp.concatenate([bu[j * npl + nph + q] for q in range(nph)], axis=1).astype(BF16)
        y_tm[j] = _dot(xr16, cd_ref[j, 0]) - _dot(xi16, cd_ref[j, 1])
    for s in range(s_blk):
        ref, rows = seq_view(u_refs, s)
        o_ref, o_rows = seq_view(o_refs, s)
        for j in range(n_slab):
            cols = slice(j * uw, (j + 1) * uw)
            y = y_tm[j, pl.ds(s, tt, stride=s_blk), :] + d_ref[:, cols] * ref[rows, cols]
            o_ref[o_rows, cols] = _gelu_tanh(y).astype(BF16)

    @pl.when(tb == pl.num_programs(1) - 1)
    def _():
        xl_ref[...] = st[0:s_blk, :]


def _ssm(z32, x0_lay, lam_lay, bd, cd, d_row, *, row0, n_seq, seq_len, s_blk, tt):
    n_slab = bd.shape[0]
    uw = bd.shape[1]
    sw = bd.shape[2]
    ds = n_slab * uw
    state_w = n_slab * sw
    n_tb = seq_len // tt
    n_sg = n_seq // s_blk
    if n_tb == 1:
        n_refs, rpr = 1, s_blk * tt
        assert row0 % rpr == 0
        in_maps = [lambda sg, tb: (row0 // rpr + sg, 0)]
        out_shape = [jax.ShapeDtypeStruct((n_seq * seq_len, ds), BF16)]
        out_maps = [lambda sg, tb: (sg, 0)]
    else:
        assert n_sg == 1 and row0 == 0
        n_refs, rpr = s_blk, tt
        in_maps = [functools.partial(lambda sg, tb, s: (s * n_tb + tb, 0), s=s) for s in range(s_blk)]
        out_shape = [jax.ShapeDtypeStruct((seq_len, ds), BF16)] * s_blk
        out_maps = [lambda sg, tb: (tb, 0)] * s_blk
    rows = n_refs * rpr
    kern = functools.partial(_ssm_kernel, n_refs=n_refs, rpr=rpr, s_blk=s_blk, tt=tt,
                             n_slab=n_slab, sw=sw)
    outs = pl.pallas_call(
        kern,
        out_shape=tuple(out_shape) + (jax.ShapeDtypeStruct((n_seq, state_w), F32),),
        grid=(n_sg, n_tb),
        in_specs=[pl.BlockSpec((rpr, ds), m) for m in in_maps] + [
            pl.BlockSpec((s_blk, state_w), lambda sg, tb: (sg, 0)),
            pl.BlockSpec((1, state_w), lambda sg, tb: (0, 0)),
            pl.BlockSpec(bd.shape, lambda sg, tb: (0, 0, 0)),
            pl.BlockSpec(cd.shape, lambda sg, tb: (0, 0, 0, 0)),
            pl.BlockSpec((1, ds), lambda sg, tb: (0, 0))],
        out_specs=tuple(pl.BlockSpec((rpr, ds), m) for m in out_maps) + (
            pl.BlockSpec((s_blk, state_w), lambda sg, tb: (sg, 0)),),
        scratch_shapes=[pltpu.VMEM((n_slab, rows, uw), F32),
                        pltpu.VMEM((state_w // LANES, rows, LANES), F32),
                        pltpu.VMEM((n_slab, rows, uw), F32),
                        pltpu.VMEM((SUBLANES, state_w), F32)],
        compiler_params=_params(("arbitrary", "arbitrary")),
        name="ssm",
    )(*([z32] * n_refs), x0_lay, lam_lay, bd, cd, d_row)
    return list(outs[:-1]), outs[-1]


def _cumsum_kernel(x_ref, o_ref, *, blk):
    r, t = x_ref.shape
    row = lax.broadcasted_iota(I32, (blk, blk), 0)
    col = lax.broadcasted_iota(I32, (blk, blk), 1)
    upper = jnp.where(row <= col, 1.0, 0.0).astype(BF16)
    carry = jnp.zeros((r, 1), F32)
    for c in range(t // blk):
        x = x_ref[:, c * blk:(c + 1) * blk]
        h1 = x.astype(BF16)
        r1 = x - h1.astype(F32)
        h2 = r1.astype(BF16)
        h3 = (r1 - h2.astype(F32)).astype(BF16)
        s = (_dot(h3, upper) + _dot(h2, upper)) + _dot(h1, upper) + carry
        o_ref[:, c * blk:(c + 1) * blk] = s
        carry = s[:, blk - 1:blk]


def _cumsum_lanes(x, blk=256, tr=64):
    r, t = x.shape
    tr = min(tr, r)
    return pl.pallas_call(
        functools.partial(_cumsum_kernel, blk=blk),
        out_shape=jax.ShapeDtypeStruct((r, t), F32),
        grid=(r // tr,),
        in_specs=[pl.BlockSpec((tr, t), lambda i: (i, 0))],
        out_specs=pl.BlockSpec((tr, t), lambda i: (i, 0)),
        compiler_params=_params(("arbitrary",)),
        name="cumsum",
    )(x)


def _head_lanes(nh):
    lane = lax.broadcasted_iota(I32, (1, nh * FOX_HEAD_DIM), 1)
    return [(lane >= h * FOX_HEAD_DIM) & (lane < (h + 1) * FOX_HEAD_DIM) for h in range(nh)]


def _per_head(sels, vals):
    reps = len(sels) * FOX_HEAD_DIM // LANES
    out = jnp.tile(vals[0], (1, reps)) if reps > 1 else vals[0]
    for sel, v in zip(sels[1:], vals[1:]):
        out = jnp.where(sel, jnp.tile(v, (1, reps)) if reps > 1 else v, out)
    return out


def _attn_step(q, k, v, ck, m_scr, l_scr, acc_scr, mask):
    nh = m_scr.shape[0]
    sels = _head_lanes(nh)
    pv, alphas = None, []
    for h in range(nh):
        qh = jnp.where(sels[h], q, jnp.zeros_like(q))
        s = _dot_nt(qh, k) - ck[h:h + 1, :]
        if mask is not None:
            s = jnp.where(mask, s, -jnp.inf)
        m_prev = m_scr[h]
        m_new = jnp.maximum(m_prev, jnp.max(s, axis=-1, keepdims=True))
        alpha = jnp.exp(m_prev - m_new)
        p = jnp.exp(s - m_new[:, :1])
        l_scr[h] = alpha * l_scr[h] + jnp.sum(p, axis=-1, keepdims=True)
        m_scr[h] = m_new
        pv_h = _dot(p.astype(BF16), v)
        pv = pv_h if pv is None else jnp.where(sels[h], pv_h, pv)
        alphas.append(alpha)
    acc_scr[...] = _per_head(sels, alphas) * acc_scr[...] + pv


def _attn_init(m_scr, l_scr, acc_scr):
    m_scr[...] = jnp.full(m_scr.shape, -jnp.inf, F32)
    l_scr[...] = jnp.zeros(l_scr.shape, F32)
    acc_scr[...] = jnp.zeros(acc_scr.shape, F32)


def _attn_finish(o_ref, l_scr, acc_scr):
    nh = l_scr.shape[0]
    l = _per_head(_head_lanes(nh), [l_scr[h] for h in range(nh)])
    o_ref[...] = (acc_scr[...] / l).astype(o_ref.dtype)


def _causal_mask(tq, tk):
    return (lax.broadcasted_iota(I32, (tq, tk), 1) <= lax.broadcasted_iota(I32, (tq, tk), 0))


def _attn_prompt_kernel(qt_ref, kt_ref, q_ref, k_ref, v_ref, ck_ref, o_ref, m_scr, l_scr, acc_scr,
                        *, tq):
    step = pl.program_id(2)
    qi, ki = qt_ref[step], kt_ref[step]

    @pl.when(ki == 0)
    def _():
        _attn_init(m_scr, l_scr, acc_scr)

    @pl.when(ki < qi)
    def _():
        _attn_step(q_ref[...], k_ref[...], v_ref[...], ck_ref[0, 0], m_scr, l_scr, acc_scr, None)

    @pl.when(ki == qi)
    def _():
        _attn_step(q_ref[...], k_ref[...], v_ref[...], ck_ref[0, 0], m_scr, l_scr, acc_scr,
                   _causal_mask(tq, tq))
        _attn_finish(o_ref, l_scr, acc_scr)


def _attn_prompt(qkv16, ck, *, n_batch, seq_len, n_h, nh, tq=512):
    nq = seq_len // tq
    lw = nh * FOX_HEAD_DIM
    ng = n_h // nh
    pairs = [(qi, ki) for qi in range(nq) for ki in range(qi + 1)]
    qt = jnp.asarray([p[0] for p in pairs], I32)
    kt = jnp.asarray([p[1] for p in pairs], I32)
    q_map = lambda b, g, s, qt, kt: (b * nq + qt[s], g)
    kv_map = lambda c: (lambda b, g, s, qt, kt: (b * nq + kt[s], c * ng + g))
    return pl.pallas_call(
        functools.partial(_attn_prompt_kernel, tq=tq),
        out_shape=jax.ShapeDtypeStruct((n_batch * seq_len, ng * lw), BF16),
        grid_spec=pltpu.PrefetchScalarGridSpec(
            num_scalar_prefetch=2,
            grid=(n_batch, ng, len(pairs)),
            in_specs=[pl.BlockSpec((tq, lw), q_map),
                      pl.BlockSpec((tq, lw), kv_map(1)),
                      pl.BlockSpec((tq, lw), kv_map(2)),
                      pl.BlockSpec((1, 1, nh, tq), lambda b, g, s, qt, kt: (b, g, 0, kt[s]))],
            out_specs=pl.BlockSpec((tq, lw), q_map),
            scratch_shapes=[pltpu.VMEM((nh, tq, LANES), F32), pltpu.VMEM((nh, tq, LANES), F32),
                            pltpu.VMEM((tq, lw), F32)]),
        compiler_params=_params(("arbitrary",) * 3),
        name="attn_prompt",
    )(qt, kt, qkv16, qkv16, qkv16, ck)


def _attn_sample_kernel(q_ref, kp_ref, vp_ref, kn_ref, vn_ref, ckp_ref, ckn_ref, o_ref, *, ts, nh):
    hd = FOX_HEAD_DIM
    lw = nh * hd
    lane = lax.broadcasted_iota(I32, (1, lw), 1)
    sels = [(lane >= h * hd) & (lane < (h + 1) * hd) for h in range(nh)]
    q = q_ref[...]
    qbd = jnp.concatenate([jnp.where(sels[h], q, jnp.zeros_like(q)) for h in range(nh)], axis=0)

    def update(state, k, v, ck, mask):
        m_prev, l_prev, acc = state
        s = _dot_nt(qbd, k)
        rows = []
        for h in range(nh):
            sh = s[h * ts:(h + 1) * ts, :] - ck[h:h + 1, :]
            rows.append(sh if mask is None else jnp.where(mask, sh, -jnp.inf))
        s = jnp.concatenate(rows, axis=0)
        m_new = jnp.maximum(m_prev, jnp.max(s, axis=-1, keepdims=True))
        alpha = jnp.exp(m_prev - m_new)
        p = jnp.exp(s - m_new)
        l_new = alpha * l_prev + jnp.sum(p, axis=-1, keepdims=True)
        return m_new, l_new, alpha * acc + _dot(p.astype(BF16), v)

    state = (jnp.full((nh * ts, 1), -jnp.inf, F32), jnp.zeros((nh * ts, 1), F32),
             jnp.zeros((nh * ts, lw), F32))
    state = update(state, kp_ref[0].astype(BF16), vp_ref[0].astype(BF16), ckp_ref[0, 0], None)
    _, l, acc = update(state, kn_ref[...], vn_ref[...], ckn_ref[0, 0], _causal_mask(ts, ts))
    res = acc / l
    out = res[0:ts, :]
    for h in range(1, nh):
        out = jnp.where(sels[h], res[h * ts:(h + 1) * ts, :], out)
    o_ref[...] = out.astype(o_ref.dtype)


def _attn_sample(qkv16, k_past, v_past, ck_past, ck_new, *, row0, n_batch, seq_len, n_h, nh=4):
    past = k_past.shape[1]
    lw = nh * FOX_HEAD_DIM
    ng = n_h // nh
    rb0 = row0 // seq_len
    new_map = lambda c: (lambda b, g: (rb0 + b, c * ng + g))
    return pl.pallas_call(
        functools.partial(_attn_sample_kernel, ts=seq_len, nh=nh),
        out_shape=jax.ShapeDtypeStruct((n_batch * seq_len, ng * lw), BF16),
        grid=(n_batch, ng),
        in_specs=[pl.BlockSpec((seq_len, lw), new_map(0)),
                  pl.BlockSpec((1, past, lw), lambda b, g: (b, 0, g)),
                  pl.BlockSpec((1, past, lw), lambda b, g: (b, 0, g)),
                  pl.BlockSpec((seq_len, lw), new_map(1)),
                  pl.BlockSpec((seq_len, lw), new_map(2)),
                  pl.BlockSpec((1, 1, nh, past), lambda b, g: (b, g, 0, 0)),
                  pl.BlockSpec((1, 1, nh, seq_len), lambda b, g: (b, g, 0, 0))],
        out_specs=pl.BlockSpec((seq_len, lw), lambda b, g: (b, g)),
        compiler_params=_params(("arbitrary",) * 2),
        name="attn_sample",
    )(qkv16, k_past, v_past, qkv16, qkv16, ck_past, ck_new)


def _glu_kernel(g_ref, wa_ref, wb_ref, o_ref):
    g = g_ref[...]
    o_ref[...] = (_dot(g, wa_ref[...]) * jax.nn.sigmoid(_dot(g, wb_ref[...]))).astype(o_ref.dtype)


def _glu(g16, w_glu16, tm=512, tn=1024):
    n, ds = g16.shape
    d = w_glu16.shape[1] // 2
    tn = min(tn, d)
    nb = d // tn
    return pl.pallas_call(
        _glu_kernel,
        out_shape=jax.ShapeDtypeStruct((n, d), BF16),
        grid=(n // tm, nb),
        in_specs=[pl.BlockSpec((tm, ds), lambda i, j: (i, 0)),
                  pl.BlockSpec((ds, tn), lambda i, j: (0, j)),
                  pl.BlockSpec((ds, tn), lambda i, j: (0, nb + j))],
        out_specs=pl.BlockSpec((tm, tn), lambda i, j: (i, j)),
        compiler_params=_params(("arbitrary", "arbitrary")),
        name="glu",
    )(g16, w_glu16, w_glu16)


def _post_kernel(attp_ref, atts_ref, brs_ref, gs_ref, gf_ref, xp_ref, xs_ref, ada_ref, wfo_ref,
                 wo_ref, lg_ref, lb_ref, o_ref, *, nsub, alpha, np_tiles):
    i = pl.program_id(0)
    att = jnp.where(i < np_tiles, attp_ref[...], atts_ref[...])
    br_fox = _dot(att, wfo_ref[...])
    merged = (jax.nn.sigmoid(gs_ref[...].astype(F32)) * brs_ref[...].astype(F32)
              + jax.nn.sigmoid(gf_ref[...].astype(F32)) * br_fox)
    mix = _dot(merged.astype(BF16), wo_ref[...])

    def finish(x_ref):
        for s in range(nsub):
            rows = slice(s * ADA_BLOCK, (s + 1) * ADA_BLOCK)
            g1 = ada_ref[s, 2:3, :]
            y = alpha * x_ref[rows, :] + (1.0 + g1) * mix[rows, :]
            o_ref[rows, :] = _layer_norm(y, lg_ref[...], lb_ref[...])

    @pl.when(i < np_tiles)
    def _():
        finish(xp_ref)

    @pl.when(i >= np_tiles)
    def _():
        finish(xs_ref)


def _post(attn_p, attn_s, br_ssm, gates, x_p, x_s, ada_blk, w_fox16, w_out16, ln_g, ln_b, *,
          alpha, tm=256):
    n_p, d = x_p.shape
    n = n_p + x_s.shape[0]
    ds = d // 2
    nsub = tm // ADA_BLOCK
    np_tiles = n_p // tm
    return pl.pallas_call(
        functools.partial(_post_kernel, nsub=nsub, alpha=alpha, np_tiles=np_tiles),
        out_shape=jax.ShapeDtypeStruct((n, d), F32),
        grid=(n // tm,),
        in_specs=_two_part_specs(tm, ds, np_tiles, 1) + [
                  pl.BlockSpec((tm, d), lambda i: (i, 0)),
                  pl.BlockSpec((tm, d), lambda i: (i, 0)),
                  pl.BlockSpec((tm, d), lambda i: (i, 1)),
                  ] + _two_part_specs(tm, d, np_tiles, 1) + [
                  pl.BlockSpec((nsub, 6, d), lambda i: (i, 0, 0)),
                  pl.BlockSpec((ds, d), lambda i: (0, 0)),
                  pl.BlockSpec((d, d), lambda i: (0, 0)),
                  pl.BlockSpec((1, d), lambda i: (0, 0)),
                  pl.BlockSpec((1, d), lambda i: (0, 0))],
        out_specs=pl.BlockSpec((tm, d), lambda i: (i, 0)),
        compiler_params=_params(("arbitrary",)),
        name="post_mix",
    )(attn_p, attn_s, br_ssm, gates, gates, x_p, x_s, ada_blk, w_fox16, w_out16, ln_g, ln_b)


def _pack_halves(x):
    c = x.shape[1] // 2
    return pltpu.pack_elementwise([x[:, :c], x[:, c:]], packed_dtype=BF16)


def _unpack_halves(w):
    return tuple(pltpu.unpack_elementwise(w, index=i, packed_dtype=BF16, unpacked_dtype=F32)
                 for i in range(2))


def _store_token_tiles(ref, words):
    m, c = words.shape
    nsl = c // LANES
    for s in range(nsl):
        ref[pl.ds(s, m, stride=nsl), :] = words[:, s * LANES:(s + 1) * LANES]


def _load_token_tiles(ref, m):
    nsl = ref.shape[0] // m
    return jnp.concatenate([ref[pl.ds(s, m, stride=nsl), :] for s in range(nsl)], axis=1)


def _token_rows(t, nsl):
    start = t * nsl
    if nsl % SUBLANES == 0:
        start = pl.multiple_of(start, SUBLANES)
    return pl.ds(start, nsl)


def _router_kernel(x_ref, ada_ref, wr_ref, rb_ref, hp_ref, e_ref, w_ref, h_scr, *, nsub):
    ng = N_EXPERT_GROUPS
    for s in range(nsub):
        rows = slice(s * ADA_BLOCK, (s + 1) * ADA_BLOCK)
        h_scr[rows, :] = x_ref[rows, :] * (1.0 + ada_ref[s, 4:5, :]) + ada_ref[s, 3:4, :]
    _store_token_tiles(hp_ref, _pack_halves(h_scr[...]))
    h_hi, h_lo = _split_bf16(h_scr[...])
    w_hi, w_lo = _split_bf16(wr_ref[...])
    logits = (_dot_nt(w_hi, h_lo) + _dot_nt(w_lo, h_hi)) + _dot_nt(w_hi, h_hi)
    scores = jax.nn.sigmoid(logits)
    sel = scores + rb_ref[...]
    gsz = sel.shape[0] // ng
    tm = sel.shape[1]
    xs = [sel[j * ng:(j + 1) * ng, :] for j in range(gsz)]
    sc = [scores[j * ng:(j + 1) * ng, :] for j in range(gsz)]
    neg = -jnp.inf

    def lmax(v):
        out = v[0]
        for a in v[1:]:
            out = jnp.maximum(out, a)
        return out

    def lmin(v):
        out = v[0]
        for a in v[1:]:
            out = jnp.minimum(out, a)
        return out

    m1 = lmax(xs)
    i1 = lmin([jnp.where(xs[j] == m1, j, gsz) for j in range(gsz)])
    m2 = lmax([jnp.where(i1 == j, neg, xs[j]) for j in range(gsz)])
    cur = m1 + m2
    giota = lax.broadcasted_iota(I32, (ng, tm), 0)
    gsel = jnp.zeros((ng, tm), F32)
    for _ in range(TOPK_EXPERT_GROUPS):
        m = jnp.max(cur, axis=0, keepdims=True)
        gi = jnp.min(jnp.where(cur == m, giota, ng), axis=0, keepdims=True)
        hit = giota == gi
        gsel = jnp.where(hit, 1.0, gsel)
        cur = jnp.where(hit, neg, cur)
    gmask = gsel > 0.0
    xs = [jnp.where(gmask, x, neg) for x in xs]
    eid = [giota * gsz + j for j in range(gsz)]
    n_e = ng * gsz
    vals = []
    for r in range(TOP_K):
        m = jnp.max(lmax(xs), axis=0, keepdims=True)
        ci = jnp.min(lmin([jnp.where(xs[j] == m, eid[j], n_e) for j in range(gsz)]),
                     axis=0, keepdims=True)
        hits = [eid[j] == ci for j in range(gsz)]
        v = sum(jnp.where(hits[j], sc[j], 0.0) for j in range(gsz))
        vals.append(jnp.sum(v, axis=0, keepdims=True))
        xs = [jnp.where(hits[j], neg, xs[j]) for j in range(gsz)]
        e_ref[r:r + 1, :] = ci
    tot = sum(vals)
    for r in range(TOP_K):
        w_ref[r:r + 1, :] = vals[r] / tot * ROUTED_SCALE


def _router(x1, ada_blk, wr_perm, rb_perm, tm=512):
    n, d = x1.shape
    e = wr_perm.shape[0]
    nsub = tm // ADA_BLOCK
    return pl.pallas_call(
        functools.partial(_router_kernel, nsub=nsub),
        out_shape=(jax.ShapeDtypeStruct((n * (d // 2 // LANES), LANES), jnp.uint32),
                   jax.ShapeDtypeStruct((TOP_K, n), I32),
                   jax.ShapeDtypeStruct((TOP_K, n), F32)),
        grid=(n // tm,),
        in_specs=[pl.BlockSpec((tm, d), lambda i: (i, 0)),
                  pl.BlockSpec((nsub, 6, d), lambda i: (i, 0, 0)),
                  pl.BlockSpec((e, d), lambda i: (0, 0)),
                  pl.BlockSpec((e, 1), lambda i: (0, 0))],
        out_specs=(pl.BlockSpec((tm * (d // 2 // LANES), LANES), lambda i: (i, 0)),
                   pl.BlockSpec((TOP_K, tm), lambda i: (0, i)),
                   pl.BlockSpec((TOP_K, tm), lambda i: (0, i))),
        scratch_shapes=[pltpu.VMEM((tm, d), F32)],
        compiler_params=_params(("arbitrary",)),
        name="router",
    )(x1, ada_blk, wr_perm, rb_perm)


def _plan_kernel(e_ref, pos_ref, st_ref, vis_ref, rank_scr, *, n_e, blk, tm):
    n = e_ref.shape[1]
    nblk = n // blk
    row = lax.broadcasted_iota(I32, (blk, blk), 0)
    col = lax.broadcasted_iota(I32, (blk, blk), 1)
    upper = jnp.where(row <= col, 1.0, 0.0).astype(BF16)
    eid = lax.broadcasted_iota(I32, (n_e, blk), 0)

    def count_body(cb, carry):
        cols = pl.ds(pl.multiple_of(cb * blk, blk), blk)
        e_blk = e_ref[:, cols]
        hit = jnp.zeros((n_e, blk), F32)
        for k in range(TOP_K):
            hit = hit + jnp.where(e_blk[k:k + 1, :] == eid, 1.0, 0.0)
        cs = _dot(hit.astype(BF16), upper) + carry
        rank_scr[:, cols] = cs - hit
        return cs[:, blk - 1:blk]

    counts = lax.fori_loop(0, nblk, count_body, jnp.zeros((n_e, 1), F32))

    hi = jnp.floor(counts * (1.0 / 128.0))
    lo = counts - hi * 128.0
    er = lax.broadcasted_iota(I32, (n_e, n_e), 0)
    ec = lax.broadcasted_iota(I32, (n_e, n_e), 1)
    lower = jnp.where(ec < er, 1.0, 0.0).astype(BF16)
    lower_incl = jnp.where(ec <= er, 1.0, 0.0).astype(BF16)
    wide = lambda v: jnp.broadcast_to(v, (n_e, LANES)).astype(BF16)
    starts = (_dot(lower, wide(hi)) * 128.0 + _dot(lower, wide(lo)))[:, :1]
    st_ref[...] = jnp.broadcast_to(starts, st_ref.shape).astype(I32)

    inv_tm = 1.0 / tm
    nonempty = counts > 0.0
    first_t = jnp.floor(starts * inv_tm)
    nvis = jnp.where(nonempty, jnp.floor((starts + counts - 1.0) * inv_tm) - first_t + 1.0, 0.0)
    vend = _dot(lower_incl, wide(nvis))[:, :1]
    gidx = _dot(lower_incl, wide(jnp.where(nonempty, 1.0, 0.0)))[:, :1] - 1.0
    total = jnp.max(vend, axis=0, keepdims=True)
    nv = vis_ref.shape[1]
    viota = lax.broadcasted_iota(I32, (1, nv), 1).astype(F32)
    vc = jnp.minimum(viota, total - 1.0)
    e_v = jnp.sum(jnp.where(vend <= vc, 1.0, 0.0), axis=0, keepdims=True)
    eio = lax.broadcasted_iota(I32, (n_e, nv), 0).astype(F32)
    mine = eio == e_v
    pick = lambda colv: jnp.sum(jnp.where(mine, colv, 0.0), axis=0, keepdims=True)
    tile_v = pick(first_t) + (vc - pick(vend - nvis))
    g_v = pick(gidx)
    slot_v = g_v - 2.0 * jnp.floor(g_v * 0.5)
    is_next = nonempty & (gidx == g_v + 1.0)
    nxt_v = (jnp.sum(jnp.where(is_next, eio + 1.0, 0.0), axis=0, keepdims=True) - 1.0)
    rows = [tile_v, e_v, jnp.where(viota < total, 1.0, 0.0), slot_v, nxt_v]
    rows += [jnp.zeros((1, nv), F32)] * (vis_ref.shape[0] - len(rows))
    vis_ref[...] = jnp.concatenate(rows, axis=0).astype(I32)

    def pos_body(cb, c):
        cols = pl.ds(pl.multiple_of(cb * blk, blk), blk)
        e_blk = e_ref[:, cols]
        val = rank_scr[:, cols] + starts
        for k in range(TOP_K):
            p = jnp.sum(jnp.where(e_blk[k:k + 1, :] == eid, val, 0.0), axis=0, keepdims=True)
            pos_ref[k:k + 1, cols] = p.astype(I32)
        return c

    lax.fori_loop(0, nblk, pos_body, 0)


def _plan(eidx_t, n_e, tm, blk=256):
    k, n = eidx_t.shape
    n_visits = (n * k) // tm + n_e - 1
    nv = -(-n_visits // LANES) * LANES
    pos, starts, vis = pl.pallas_call(
        functools.partial(_plan_kernel, n_e=n_e, blk=blk, tm=tm),
        out_shape=(jax.ShapeDtypeStruct((k, n), I32), jax.ShapeDtypeStruct((n_e, LANES), I32),
                   jax.ShapeDtypeStruct((SUBLANES, nv), I32)),
        scratch_shapes=[pltpu.VMEM((n_e, n), F32)],
        compiler_params=pltpu.CompilerParams(vmem_limit_bytes=VMEM_LIMIT_BYTES),
        name="moe_plan",
    )(eidx_t)
    return pos, starts, vis[:, :n_visits]


def _dispatch_kernel(pos_ref, h_ref, xs_ref, sem, *, tm):
    i = pl.program_id(0)
    nsl = h_ref.shape[0] // tm

    def issue(r, c):
        base = (i * tm + r) * TOP_K
        src = h_ref.at[_token_rows(r, nsl), :]
        for k in range(TOP_K):
            p = pos_ref[base + k]
            pltpu.make_async_copy(src, xs_ref.at[_token_rows(p, nsl), :],
                                  sem).start(priority=k % 2)
        return c

    lax.fori_loop(0, tm, issue, 0)
    for k in range(TOP_K):
        pltpu.make_async_copy(h_ref, xs_ref.at[pl.ds(0, tm * nsl), :], sem).wait()


def _dispatch(pos_flat, h2t, n, tm=512):
    rows, lanes = h2t.shape
    nsl = rows // n
    return pl.pallas_call(
        functools.partial(_dispatch_kernel, tm=tm),
        out_shape=jax.ShapeDtypeStruct((rows * TOP_K, lanes), h2t.dtype),
        grid_spec=pltpu.PrefetchScalarGridSpec(
            num_scalar_prefetch=1,
            grid=(n // tm,),
            in_specs=[pl.BlockSpec((tm * nsl, lanes), lambda i, pos: (i, 0))],
            out_specs=pl.BlockSpec(memory_space=pl.ANY),
            scratch_shapes=[pltpu.SemaphoreType.DMA(())]),
        compiler_params=_params(("arbitrary",)),
        name="moe_dispatch",
    )(pos_flat, h2t)


def _gmm_kernel(vt_ref, ve_ref, vv_ref, vs_ref, nx_ref, bd_ref,
                xs_ref, wg_hbm, wu_hbm, wd_hbm, ys_ref,
                wg32, wu32, wd32, wg16, wu16, wd16, acc, sem, *, tm, chain):
    v = pl.program_id(0)
    t = vt_ref[v]
    e = ve_ref[v]
    slot = vs_ref[v]
    pv = jnp.maximum(v - 1, 0)
    first = v == 0
    valid = vv_ref[v] == 1
    r0 = t * tm
    lo = bd_ref[e]
    hi = bd_ref[e + 1]
    whole = (r0 >= lo) & (r0 + tm <= hi)

    def weight_copies(expert, s):
        return (pltpu.make_async_copy(wg_hbm.at[expert], wg32.at[s], sem.at[s, 0]),
                pltpu.make_async_copy(wu_hbm.at[expert], wu32.at[s], sem.at[s, 1]),
                pltpu.make_async_copy(wd_hbm.at[expert], wd32.at[s], sem.at[s, 2]))

    @pl.when(first)
    def _():
        for c in weight_copies(e, slot):
            c.start()

    @pl.when(first | (e != ve_ref[pv]))
    def _():
        for c in weight_copies(e, slot):
            c.wait()
        nxt = nx_ref[v]

        @pl.when(nxt >= 0)
        def _():
            for c in weight_copies(nxt, 1 - slot):
                c.start()

        wg16[...] = wg32[slot].astype(BF16)
        wu16[...] = wu32[slot].astype(BF16)
        wd16[...] = wd32[slot].astype(BF16)

    @pl.when(valid & jnp.logical_not(whole) & (first | (t != vt_ref[pv])))
    def _():
        acc[...] = jnp.zeros(acc.shape, F32)

    @pl.when(valid)
    def _():
        xw = _load_token_tiles(xs_ref, tm)
        parts = []
        for c0 in range(0, tm, chain):
            x = jnp.concatenate(_unpack_halves(xw[c0:c0 + chain]), axis=1).astype(BF16)
            g = _dot(x, wg16[...])
            u = _dot(x, wu16[...])
            act = (g * jax.nn.sigmoid(g) * u).astype(BF16)
            parts.append(_dot(act, wd16[...]))
        y = parts[0] if len(parts) == 1 else jnp.concatenate(parts, axis=0)

        @pl.when(whole)
        def _():
            _store_token_tiles(ys_ref, _pack_halves(y))

        @pl.when(jnp.logical_not(whole))
        def _():
            row = r0 + lax.broadcasted_iota(I32, (tm, 1), 0)
            acc[...] += jnp.where((row >= lo) & (row < hi), y, 0.0)
            _store_token_tiles(ys_ref, _pack_halves(acc[...]))


def _gmm(tile, expert, valid, slot, nxt, bounds, xs, w_gate, w_up, w_down, tm=256):
    n_e, d, f = w_gate.shape
    nsl = d // 2 // LANES
    n_visits = tile.shape[0]
    row_map = lambda v, vt, ve, vv, vs, nx, bd: (vt[v], 0)
    return pl.pallas_call(
        functools.partial(_gmm_kernel, tm=tm, chain=min(tm, 256)),
        out_shape=jax.ShapeDtypeStruct(xs.shape, xs.dtype),
        grid_spec=pltpu.PrefetchScalarGridSpec(
            num_scalar_prefetch=6,
            grid=(n_visits,),
            in_specs=[pl.BlockSpec((tm * nsl, LANES), row_map),
                      pl.BlockSpec(memory_space=pl.ANY),
                      pl.BlockSpec(memory_space=pl.ANY),
                      pl.BlockSpec(memory_space=pl.ANY)],
            out_specs=pl.BlockSpec((tm * nsl, LANES), row_map),
            scratch_shapes=[pltpu.VMEM((2, d, f), F32), pltpu.VMEM((2, d, f), F32),
                            pltpu.VMEM((2, f, d), F32),
                            pltpu.VMEM((d, f), BF16), pltpu.VMEM((d, f), BF16),
                            pltpu.VMEM((f, d), BF16),
                            pltpu.VMEM((tm, d), F32),
                            pltpu.SemaphoreType.DMA((2, 3))]),
        compiler_params=_params(("arbitrary",)),
        name="moe_experts",
    )(tile, expert, valid, slot, nxt, bounds, xs, w_gate, w_up, w_down)


def _combine_kernel(pos_ref, ys_ref, w_ref, hp_ref, x_ref, ada_ref, wsg_ref, wsu_ref, wsd_ref,
                    lg_ref, lb_ref, op_ref, os_ref, buf, ffn_scr, sem, *, tm, nsub, alpha, np_tiles):
    i = pl.program_id(0)
    nsl = hp_ref.shape[0] // tm

    def issue(r, c):
        base = (i * tm + r) * TOP_K
        dst_rows = _token_rows(r, nsl)
        for k in range(TOP_K):
            p = pos_ref[base + k]
            pltpu.make_async_copy(ys_ref.at[_token_rows(p, nsl), :], buf.at[k, dst_rows, :],
                                  sem).start(priority=k % 2)
        return c

    lax.fori_loop(0, tm, issue, 0)

    h16 = jnp.concatenate(_unpack_halves(_load_token_tiles(hp_ref, tm)), axis=1).astype(BF16)
    g = _dot(h16, wsg_ref[...])
    u = _dot(h16, wsu_ref[...])
    shared = _dot((g * jax.nn.sigmoid(g) * u).astype(BF16), wsd_ref[...])

    for k in range(TOP_K):
        pltpu.make_async_copy(ys_ref.at[pl.ds(0, tm * nsl), :], buf.at[k], sem).wait()
    ffn_scr[...] = shared
    half = ffn_scr.shape[1] // 2
    rblk = min(tm, 128)
    for r0 in range(0, tm, rblk):
        wb = [jnp.broadcast_to(w_ref[r0:r0 + rblk, k:k + 1], (rblk, LANES)) for k in range(TOP_K)]
        for s in range(nsl):
            lo = hi = None
            for k in range(TOP_K):
                a, b = _unpack_halves(buf[k, pl.ds(r0 * nsl + s, rblk, stride=nsl), :])
                lo = wb[k] * a if lo is None else lo + wb[k] * a
                hi = wb[k] * b if hi is None else hi + wb[k] * b
            ffn_scr[r0:r0 + rblk, s * LANES:(s + 1) * LANES] += lo
            ffn_scr[r0:r0 + rblk, half + s * LANES:half + (s + 1) * LANES] += hi

    def finish(o_ref):
        for s in range(nsub):
            rows = slice(s * ADA_BLOCK, (s + 1) * ADA_BLOCK)
            g2 = ada_ref[s, 5:6, :]
            y = alpha * x_ref[rows, :] + (1.0 + g2) * ffn_scr[rows, :]
            o_ref[rows, :] = _layer_norm(y, lg_ref[...], lb_ref[...])

    @pl.when(i < np_tiles)
    def _():
        finish(op_ref)

    @pl.when(i >= np_tiles)
    def _():
        finish(os_ref)


def _combine(pos_flat, ys, w_tok, h2p, x1, ada_blk, wsg16, wsu16, wsd16, ln_g, ln_b, *,
             alpha, n_p, tm=256):
    n, d = x1.shape
    nsl = d // 2 // LANES
    f = wsg16.shape[1]
    nsub = tm // ADA_BLOCK
    np_tiles = n_p // tm
    return pl.pallas_call(
        functools.partial(_combine_kernel, tm=tm, nsub=nsub, alpha=alpha, np_tiles=np_tiles),
        out_shape=(jax.ShapeDtypeStruct((n_p, d), F32), jax.ShapeDtypeStruct((n - n_p, d), F32)),
        grid_spec=pltpu.PrefetchScalarGridSpec(
            num_scalar_prefetch=1,
            grid=(n // tm,),
            in_specs=[pl.BlockSpec(memory_space=pl.ANY),
                      pl.BlockSpec((tm, TOP_K), lambda i, pos: (i, 0)),
                      pl.BlockSpec((tm * nsl, LANES), lambda i, pos: (i, 0)),
                      pl.BlockSpec((tm, d), lambda i, pos: (i, 0)),
                      pl.BlockSpec((nsub, 6, d), lambda i, pos: (i, 0, 0)),
                      pl.BlockSpec((d, f), lambda i, pos: (0, 0)),
                      pl.BlockSpec((d, f), lambda i, pos: (0, 0)),
                      pl.BlockSpec((f, d), lambda i, pos: (0, 0)),
                      pl.BlockSpec((1, d), lambda i, pos: (0, 0)),
                      pl.BlockSpec((1, d), lambda i, pos: (0, 0))],
            out_specs=(pl.BlockSpec((tm, d), lambda i, pos: (jnp.minimum(i, np_tiles - 1), 0)),
                       pl.BlockSpec((tm, d), lambda i, pos: (jnp.maximum(i - np_tiles, 0), 0))),
            scratch_shapes=[pltpu.VMEM((TOP_K, tm * nsl, LANES), ys.dtype),
                            pltpu.VMEM((tm, d), F32),
                            pltpu.SemaphoreType.DMA(())]),
        compiler_params=_params(("arbitrary",)),
        name="moe_combine",
    )(pos_flat, ys, w_tok, h2p, x1, ada_blk, wsg16, wsu16, wsd16, ln_g, ln_b)


def _layer(x_p, x_s, c_pad, blk_batch, cache_k, cache_v, layer, past_logf, st_re, st_im, lp, dims):
    bp, tp, bs, ts, d = dims
    n_p, n_s = bp * tp, bs * ts
    n = n_p + n_s
    ds = d // 2
    n_h = ds // FOX_HEAD_DIM
    g_n, p_n = lp["ssm_lambda_re"].shape
    n_e = lp["w_router"].shape[1]
    depth_alpha = lp["alpha"]

    ada = _ada(c_pad, lp["w_ada"], lp["b_ada"])
    ada_blk = ada.reshape(ada.shape[0], 6, d)[blk_batch]

    w_in = lp["w_in"]
    w_main = jnp.concatenate([w_in[:, :4 * ds], w_in[:, 4 * ds + n_h:]], axis=1).astype(BF16)
    wf_pad = jnp.pad(w_in[:, 4 * ds:4 * ds + n_h], ((0, 0), (0, LANES - n_h))).astype(BF16)
    zu, kv_p, kv_s, gates, qkv16, logf_t = _inproj(x_p, x_s, ada_blk, w_main, wf_pad,
                                                   lp["b_f"].reshape(n_h, 1))

    lbr, lbi, bbr, bbi = _ssm_disc(lp["ssm_lambda_re"], lp["ssm_lambda_im"], lp["ssm_log_dt"],
                                   lp["ssm_b_re"], lp["ssm_b_im"])
    lam_lay = _to_state_layout(lbr[None], lbi[None])
    bd = jnp.concatenate([_block_diag_slabs(bbr.transpose(0, 2, 1)),
                          _block_diag_slabs(bbi.transpose(0, 2, 1))], axis=2).astype(BF16)
    cd = jnp.stack([_block_diag_slabs(lp["ssm_c_re"].transpose(0, 2, 1)),
                    _block_diag_slabs(lp["ssm_c_im"].transpose(0, 2, 1))], axis=1).astype(BF16)
    d_row = lp["ssm_d"].reshape(1, ds)
    x0_p = jnp.zeros((bp, 2 * g_n * p_n), F32)
    x0_s = _to_state_layout(st_re, st_im)
    g_p, xl_p = _ssm(zu, x0_p, lam_lay, bd, cd, d_row, row0=0, n_seq=bp, seq_len=tp,
                     s_blk=bp, tt=min(128, tp))
    g_s, xl_s = _ssm(zu, x0_s, lam_lay, bd, cd, d_row, row0=n_p, n_seq=bs, seq_len=ts,
                     s_blk=min(8, bs), tt=ts)
    g16 = jnp.concatenate(g_p + g_s, axis=0)
    br_ssm = _glu(g16, lp["w_glu"].astype(BF16))

    lf_p = logf_t[:, :n_p].reshape(n_h, bp, tp).transpose(1, 0, 2)
    lf_s = logf_t[:, n_p:].reshape(n_h, bs, ts).transpose(1, 0, 2)
    nh_p = 4 if n_h % 4 == 0 else 2
    ck_p = _cumsum_lanes(lf_p.reshape(bp * n_h, tp)).reshape(bp, n_h // nh_p, nh_p, tp)
    past = past_logf.shape[1]
    cat = jnp.concatenate([past_logf.astype(F32).transpose(0, 2, 1), lf_s], axis=2)
    width = -(-(past + ts) // 256) * 256
    cat = jnp.pad(cat, ((0, 0), (0, 0), (0, width - past - ts)))
    nh_s = 4
    ck_s = _cumsum_lanes(cat.reshape(bs * n_h, width)).reshape(bs, n_h // nh_s, nh_s, width)
    attn_p = _attn_prompt(qkv16, ck_p, n_batch=bp, seq_len=tp, n_h=n_h, nh=nh_p, tq=min(512, tp))
    attn_s = _attn_sample(qkv16, cache_k[layer].reshape(bs, past, ds),
                          cache_v[layer].reshape(bs, past, ds),
                          ck_s[..., :past], ck_s[..., past:past + ts],
                          row0=n_p, n_batch=bs, seq_len=ts, n_h=n_h, nh=nh_s)

    x1 = _post(attn_p, attn_s, br_ssm, gates, x_p, x_s, ada_blk, lp["w_fox_o"].astype(BF16),
               lp["w_out"].astype(BF16), lp["ln1_g"].reshape(1, d), lp["ln1_b"].reshape(1, d),
               alpha=depth_alpha)

    gsz = n_e // N_EXPERT_GROUPS
    perm = lambda a: a.reshape(N_EXPERT_GROUPS, gsz, -1).transpose(1, 0, 2).reshape(n_e, -1)
    h2p, eidx_t, w_t = _router(x1, ada_blk, perm(lp["w_router"].T), perm(lp["router_bias"].reshape(n_e, 1)))
    tm_e = 256
    pos_t, starts, vis = _plan(eidx_t, n_e, tm_e)
    pos_flat = pos_t.T.reshape(n * TOP_K)
    bounds = jnp.concatenate([starts[:, 0], jnp.full((1,), n * TOP_K, I32)])
    xs = _dispatch(pos_flat, h2p, n)
    ys = _gmm(vis[0], vis[1], vis[2], vis[3], vis[4], bounds, xs, lp["w_exp_gate"], lp["w_exp_up"],
              lp["w_exp_down"], tm=tm_e)
    x2_p, x2_s = _combine(pos_flat, ys, w_t.T, h2p, x1, ada_blk, lp["w_sh_gate"].astype(BF16),
                          lp["w_sh_up"].astype(BF16), lp["w_sh_down"].astype(BF16),
                          lp["ln2_g"].reshape(1, d), lp["ln2_b"].reshape(1, d),
                          alpha=depth_alpha, n_p=n_p)

    k_new, v_new = (kv_p[0], kv_s[0]), (kv_p[1], kv_s[1])
    logf = logf_t.T
    ssm_p = _from_state_layout(xl_p, g_n, p_n)
    ssm_s = _from_state_layout(xl_s, g_n, p_n)
    return x2_p, x2_s, k_new, v_new, logf, ssm_p, ssm_s


def kernel(x_prompt, x_sample, cache_k, cache_v, cache_logf, state_ssm_re, state_ssm_im, c_prompt, c_sample, w_ada, b_ada, w_in, b_f, ssm_lambda_re, ssm_lambda_im, ssm_log_dt, ssm_b_re, ssm_b_im, ssm_c_re, ssm_c_im, ssm_d, w_glu, w_fox_o, w_out, ln1_g, ln1_b, w_router, router_bias, w_exp_gate, w_exp_up, w_exp_down, w_sh_gate, w_sh_up, w_sh_down, ln2_g, ln2_b):
    bp, tp, d = x_prompt.shape
    bs, ts, _ = x_sample.shape
    depth = w_ada.shape[0]
    n_p, n_s = bp * tp, bs * ts
    n_h = d // 2 // FOX_HEAD_DIM
    assert tp % ADA_BLOCK == 0 and ts == ADA_BLOCK
    alpha = (2.0 * depth) ** 0.25

    x_p, x_s = x_prompt.reshape(n_p, d), x_sample.reshape(n_s, d)
    c_all = jnp.concatenate([c_prompt, c_sample], axis=0)
    c_pad = jnp.pad(c_all, ((0, -(bp + bs) % 16), (0, 0)))
    blk_batch = np.concatenate([np.repeat(np.arange(bp), tp // ADA_BLOCK),
                                bp + np.repeat(np.arange(bs), ts // ADA_BLOCK)])
    dims = (bp, tp, bs, ts, d)
    outs_p, outs_s = [], []
    for l in range(depth):
        lp = dict(w_ada=w_ada[l], b_ada=b_ada[l], w_in=w_in[l], b_f=b_f[l],
                  ssm_lambda_re=ssm_lambda_re[l], ssm_lambda_im=ssm_lambda_im[l],
                  ssm_log_dt=ssm_log_dt[l], ssm_b_re=ssm_b_re[l], ssm_b_im=ssm_b_im[l],
                  ssm_c_re=ssm_c_re[l], ssm_c_im=ssm_c_im[l], ssm_d=ssm_d[l], w_glu=w_glu[l],
                  w_fox_o=w_fox_o[l], w_out=w_out[l], ln1_g=ln1_g[l], ln1_b=ln1_b[l],
                  w_router=w_router[l], router_bias=router_bias[l], w_exp_gate=w_exp_gate[l],
                  w_exp_up=w_exp_up[l], w_exp_down=w_exp_down[l], w_sh_gate=w_sh_gate[l],
                  w_sh_up=w_sh_up[l], w_sh_down=w_sh_down[l], ln2_g=ln2_g[l], ln2_b=ln2_b[l],
                  alpha=alpha)
        x_p, x_s, k_new, v_new, logf, ssm_p, ssm_s = _layer(
            x_p, x_s, c_pad, blk_batch, cache_k, cache_v, l, cache_logf[l],
            state_ssm_re[l].astype(F32), state_ssm_im[l].astype(F32), lp, dims)
        hd = FOX_HEAD_DIM
        outs_p.append((k_new[0].reshape(bp, tp, n_h, hd), v_new[0].reshape(bp, tp, n_h, hd),
                       logf[:n_p].reshape(bp, tp, n_h), ssm_p[0], ssm_p[1]))
        outs_s.append((k_new[1].reshape(bs, ts, n_h, hd), v_new[1].reshape(bs, ts, n_h, hd),
                       logf[n_p:].reshape(bs, ts, n_h), ssm_s[0], ssm_s[1]))
    stack = lambda outs, i: jnp.stack([o[i] for o in outs])
    return (x_p.reshape(bp, tp, d), x_s.reshape(bs, ts, d),
            stack(outs_p, 0), stack(outs_p, 1), stack(outs_p, 2), stack(outs_p, 3), stack(outs_p, 4),
            stack(outs_s, 0), stack(outs_s, 1), stack(outs_s, 2), stack(outs_s, 3), stack(outs_s, 4))
```
